```python
import math
import jax, jax.numpy as jnp
from jax import lax
import numpy as np

D_MODEL = 2048
BATCH = 4
SEQ = 4096
DEPTH = 2

HEAD_DIM = 128
GRID_W = 64
BLOCK_Q = 128
RMS_EPS = 1e-6
NEG_INF = -1e30

T5_BUCKETS = 32
T5_MAX_DIST = 1024

DILATED_PATTERNS = ((128, 1), (512, 4), (2048, 16))
A_GROUPS = len(DILATED_PATTERNS)
A_HEADS = 8
B_HEADS = 4
B_QK_DIM = 128
B_V_DIM = 2 * B_QK_DIM
C_HEADS = 8
C_KV_HEADS = 2
ROPE_THETA = 10000.0
ROPE_AXIS_DIM = HEAD_DIM // 2
D_HEADS = 8
NA_ROWS = 8
NA_COLS = 16
D_FF = 5632
CONV_W = 3

N_T5_HEADS = A_GROUPS * A_HEADS + B_HEADS
EVEN_IN = A_GROUPS * 3 * A_HEADS * HEAD_DIM + B_HEADS * (4 * B_QK_DIM + B_V_DIM)
EVEN_OUT = A_HEADS * HEAD_DIM + B_HEADS * B_V_DIM
ODD_IN = (C_HEADS + 2 * C_KV_HEADS) * HEAD_DIM + 3 * D_HEADS * HEAD_DIM
ODD_OUT = (C_HEADS + D_HEADS) * HEAD_DIM

kernel_name = "hybrid_dilated_diff_axial_na_encoder"


def rmsnorm(x, gain):
    xf = x.astype(jnp.float32)
    y = xf * lax.rsqrt(jnp.mean(xf * xf, axis=-1, keepdims=True) + RMS_EPS)
    return (y * gain.astype(jnp.float32)).astype(x.dtype)


def t5_bucket(rel):
    nb = T5_BUCKETS // 2
    max_exact = nb // 2
    ret = jnp.where(rel > 0, nb, 0)
    n = jnp.abs(rel)
    n_f = jnp.maximum(n, 1).astype(jnp.float32)
    large = max_exact + (jnp.log(n_f / max_exact) / math.log(T5_MAX_DIST / max_exact)
                         * (nb - max_exact)).astype(jnp.int32)
    large = jnp.minimum(large, nb - 1)
    return ret + jnp.where(n < max_exact, n, large)


def query_blocks(q):
    b, s = q.shape[:2]
    nb = s // BLOCK_Q
    qb = jnp.moveaxis(q.reshape((b, nb, BLOCK_Q) + q.shape[2:]), 1, 0)
    return qb, jnp.arange(nb, dtype=jnp.int32) * BLOCK_Q


def merge_blocks(o):
    o = jnp.moveaxis(o, 0, 1)
    return o.reshape((o.shape[0], o.shape[1] * o.shape[2]) + o.shape[3:])


def dilated_window_attention(q, k, v, bias_cols, window, dilation):
    s = q.shape[1]
    half = window // (2 * dilation)
    offs = dilation * jnp.arange(-half, half + 1, dtype=jnp.int32)
    rel_bias = bias_cols[t5_bucket(offs)].T.astype(jnp.float32)
    scale = HEAD_DIM ** -0.5
    qb, starts = query_blocks(q)

    def block(args):
        qi, start = args
        pos = start + jnp.arange(BLOCK_Q, dtype=jnp.int32)
        idx = pos[:, None] + offs[None, :]
        valid = (idx >= 0) & (idx < s)
        idx = jnp.clip(idx, 0, s - 1)
        kg = k[:, idx]
        vg = v[:, idx]
        logits = (jnp.einsum("bqhd,bqkhd->bhqk", qi, kg).astype(jnp.float32) * scale
                  + rel_bias[None, :, None, :])
        logits = jnp.where(valid[None, None], logits, NEG_INF)
        m = jnp.max(logits, axis=-1, keepdims=True)
        p = jnp.exp(logits - m)
        denom = jnp.sum(p, axis=-1)
        o = (jnp.einsum("bhqk,bqkhd->bqhd", p, vg.astype(jnp.float32))
             / jnp.transpose(denom, (0, 2, 1))[..., None])
        lse = m[..., 0] + jnp.log(denom)
        return o, jnp.transpose(lse, (0, 2, 1))

    o, lse = lax.map(block, (qb, starts))
    return merge_blocks(o), merge_blocks(lse)


def differential_attention(q, k, v, bias_cols, lam):
    s = q.shape[1]
    scale = B_QK_DIM ** -0.5
    key_pos = jnp.arange(s, dtype=jnp.int32)
    vf = v.astype(jnp.float32)
    qb, starts = query_blocks(q)

    def block(args):
        qi, start = args
        pos = start + jnp.arange(BLOCK_Q, dtype=jnp.int32)
        bias = bias_cols[t5_bucket(key_pos[None, :] - pos[:, None])]
        bias = jnp.transpose(bias, (2, 0, 1)).astype(jnp.float32)
        logits = (jnp.einsum("bqhmd,bkhmd->bhmqk", qi, k).astype(jnp.float32) * scale
                  + bias[None, :, None])
        p = jax.nn.softmax(logits, axis=-1)
        w = p[:, :, 0] - lam * p[:, :, 1]
        return jnp.einsum("bhqk,bkhd->bqhd", w, vf)

    return merge_blocks(lax.map(block, (qb, starts)))


def axial_rope_tables(s):
    t = jnp.arange(s, dtype=jnp.int32)
    row = (t // GRID_W).astype(jnp.float32)
    col = (t % GRID_W).astype(jnp.float32)
    inv_freq = ROPE_THETA ** (-(jnp.arange(0, ROPE_AXIS_DIM, 2, dtype=jnp.float32) / ROPE_AXIS_DIM))
    ang = jnp.concatenate([row[:, None] * inv_freq[None], col[:, None] * inv_freq[None]], axis=-1)
    return jnp.cos(ang), jnp.sin(ang)


def apply_rope(x, cos, sin):
    xp = x.astype(jnp.float32).reshape(x.shape[:-1] + (HEAD_DIM // 2, 2))
    x0, x1 = xp[..., 0], xp[..., 1]
    c = cos[None, :, None, :]
    sn = sin[None, :, None, :]
    out = jnp.stack([x0 * c - x1 * sn, x0 * sn + x1 * c], axis=-1)
    return out.reshape(x.shape).astype(x.dtype)


def gqa_block_attention(q, k, v):
    b, s = q.shape[:2]
    g = C_HEADS // C_KV_HEADS
    scale = HEAD_DIM ** -0.5
    vf = v.astype(jnp.float32)
    qb, _ = query_blocks(q.reshape(b, s, C_KV_HEADS, g, HEAD_DIM))

    def block(qi):
        logits = jnp.einsum("bqngd,bknd->bngqk", qi, k).astype(jnp.float32) * scale
        p = jax.nn.softmax(logits, axis=-1)
        return jnp.einsum("bngqk,bknd->bqngd", p, vf)

    o = merge_blocks(lax.map(block, qb))
    return o.reshape(b, s, C_HEADS * HEAD_DIM)


def neighbourhood_attention(q, k, v, rpb, rows):
    b, s, h, dh = q.shape
    kr = min(NA_ROWS, rows)
    kc = NA_COLS
    scale = dh ** -0.5
    kg = k.reshape(b, rows, GRID_W, h, dh)
    vg = v.reshape(b, rows, GRID_W, h, dh)
    qrows = jnp.moveaxis(q.reshape(b, rows, GRID_W, h, dh), 1, 0)
    cols = jnp.arange(GRID_W, dtype=jnp.int32)
    col_start = jnp.clip(cols - kc // 2, 0, GRID_W - kc)
    col_idx = col_start[:, None] + jnp.arange(kc, dtype=jnp.int32)[None, :]
    dc = col_idx - cols[:, None] + (NA_COLS - 1)

    def row(args):
        qi, i = args
        rs = jnp.clip(i - kr // 2, 0, rows - kr)
        kband = lax.dynamic_slice_in_dim(kg, rs, kr, axis=1)
        vband = lax.dynamic_slice_in_dim(vg, rs, kr, axis=1)
        kq = kband[:, :, col_idx]
        vq = vband[:, :, col_idx]
        dr = rs + jnp.arange(kr, dtype=jnp.int32) - i + (NA_ROWS - 1)
        bias = rpb[:, dr][:, :, dc]
        bias = jnp.transpose(bias, (0, 2, 1, 3)).astype(jnp.float32)
        logits = (jnp.einsum("bjhd,brjchd->bhjrc", qi, kq).astype(jnp.float32) * scale
                  + bias[None])
        p = jax.nn.softmax(logits.reshape(b, h, GRID_W, kr * kc), axis=-1).reshape(b, h, GRID_W, kr, kc)
        return jnp.einsum("bhjrc,brjchd->bjhd", p, vq.astype(jnp.float32))

    o = lax.map(row, (qrows, jnp.arange(rows, dtype=jnp.int32)))
    return jnp.moveaxis(o, 0, 1).reshape(b, s, h * dh)


def even_mixer(y, w_in, w_out, lq1, lk1, lq2, lk2, subln, t5_table, lambda_init):
    b, s, _ = y.shape
    proj = jnp.einsum("bsd,de->bse", y, w_in)
    a_cols = A_GROUPS * 3 * A_HEADS * HEAD_DIM
    pa = proj[..., :a_cols].reshape(b, s, A_GROUPS, 3, A_HEADS, HEAD_DIM)
    outs, lses = [], []
    for gi, (window, dilation) in enumerate(DILATED_PATTERNS):
        cols = t5_table[:, gi * A_HEADS:(gi + 1) * A_HEADS]
        o, lse = dilated_window_attention(pa[:, :, gi, 0], pa[:, :, gi, 1], pa[:, :, gi, 2],
                                          cols, window, dilation)
        outs.append(o)
        lses.append(lse)
    alpha = jax.nn.softmax(jnp.stack(lses), axis=0)[..., None]
    out_a = jnp.sum(alpha * jnp.stack(outs), axis=0).reshape(b, s, A_HEADS * HEAD_DIM)

    pb = proj[..., a_cols:]
    qk_cols = B_HEADS * 2 * B_QK_DIM
    q_b = pb[..., :qk_cols].reshape(b, s, B_HEADS, 2, B_QK_DIM)
    k_b = pb[..., qk_cols:2 * qk_cols].reshape(b, s, B_HEADS, 2, B_QK_DIM)
    v_b = pb[..., 2 * qk_cols:].reshape(b, s, B_HEADS, B_V_DIM)
    f32 = jnp.float32
    lam = (jnp.exp(jnp.sum(lq1.astype(f32) * lk1.astype(f32)))
           - jnp.exp(jnp.sum(lq2.astype(f32) * lk2.astype(f32))) + lambda_init)
    o_b = differential_attention(q_b, k_b, v_b, t5_table[:, A_GROUPS * A_HEADS:], lam)
    o_b = rmsnorm(o_b, subln) * (1.0 - lambda_init)
    out_b = o_b.reshape(b, s, B_HEADS * B_V_DIM)
    mixed = jnp.concatenate([out_a, out_b], axis=-1).astype(y.dtype)
    return jnp.einsum("bse,ed->bsd", mixed, w_out)


def odd_mixer(y, w_in, w_out, q_norm, k_norm, rpb, rows):
    b, s, _ = y.shape
    proj = jnp.einsum("bsd,de->bse", y, w_in)
    nq = C_HEADS * HEAD_DIM
    nkv = C_KV_HEADS * HEAD_DIM
    q_c = proj[..., :nq].reshape(b, s, C_HEADS, HEAD_DIM)
    k_c = proj[..., nq:nq + nkv].reshape(b, s, C_KV_HEADS, HEAD_DIM)
    v_c = proj[..., nq + nkv:nq + 2 * nkv].reshape(b, s, C_KV_HEADS, HEAD_DIM)
    pd = proj[..., nq + 2 * nkv:].reshape(b, s, 3, D_HEADS, HEAD_DIM)
    cos, sin = axial_rope_tables(s)
    q_c = apply_rope(rmsnorm(q_c, q_norm), cos, sin)
    k_c = apply_rope(rmsnorm(k_c, k_norm), cos, sin)
    o_c = gqa_block_attention(q_c, k_c, v_c)
    o_d = neighbourhood_attention(pd[:, :, 0], pd[:, :, 1], pd[:, :, 2], rpb, rows)
    mixed = jnp.concatenate([o_c, o_d], axis=-1).astype(y.dtype)
    return jnp.einsum("bse,ed->bsd", mixed, w_out)


def conv_ffn(x, w_up, conv_w, conv_b, w_down):
    h = jnp.einsum("bsd,df->bsf", x, w_up)
    g, u = h[..., :D_FF], h[..., D_FF:]
    gp = jnp.pad(g, ((0, 0), (1, 1), (0, 0)))
    g = conv_w[0] * gp[:, :-2] + conv_w[1] * gp[:, 1:-1] + conv_w[2] * gp[:, 2:] + conv_b
    return jnp.einsum("bsf,fd->bsd", jax.nn.gelu(g) * u, w_down)


def setup_inputs(seed: int = 0) -> dict:
    key = jax.random.key(seed)
    ks = jax.random.split(key, 21)
    ne, no = (DEPTH + 1) // 2, DEPTH // 2
    f32 = jnp.float32

    def w(k, shape, fan_in):
        return jax.random.normal(k, shape, f32) * fan_in ** -0.5

    def gain(k, shape):
        return 1.0 + 0.1 * jax.random.normal(k, shape, f32)

    return {
        "x": jax.random.normal(ks[0], (BATCH, SEQ, D_MODEL), f32),
        "ln_mix": gain(ks[1], (DEPTH, D_MODEL)),
        "ln_ffn": gain(ks[2], (DEPTH, D_MODEL)),
        "ln_final": gain(ks[3], (D_MODEL,)),
        "t5_table": 0.5 * jax.random.normal(ks[4], (T5_BUCKETS, N_T5_HEADS), f32),
        "ev_w_in": w(ks[5], (ne, D_MODEL, EVEN_IN), D_MODEL),
        "ev_w_out": w(ks[6], (ne, EVEN_OUT, D_MODEL), EVEN_OUT),
        "diff_lq1": 0.1 * jax.random.normal(ks[7], (ne, B_QK_DIM), f32),
        "diff_lk1": 0.1 * jax.random.normal(ks[8], (ne, B_QK_DIM), f32),
        "diff_lq2": 0.1 * jax.random.normal(ks[9], (ne, B_QK_DIM), f32),
        "diff_lk2": 0.1 * jax.random.normal(ks[10], (ne, B_QK_DIM), f32),
        "diff_subln": gain(ks[11], (ne, B_V_DIM)),
        "od_w_in": w(ks[12], (no, D_MODEL, ODD_IN), D_MODEL),
        "od_w_out": w(ks[13], (no, ODD_OUT, D_MODEL), ODD_OUT),
        "gqa_q_norm": gain(ks[14], (no, HEAD_DIM)),
        "gqa_k_norm": gain(ks[15], (no, HEAD_DIM)),
        "na_rpb": 0.5 * jax.random.normal(ks[16], (no, D_HEADS, 2 * NA_ROWS - 1, 2 * NA_COLS - 1), f32),
        "ffn_w_up": w(ks[17], (DEPTH, D_MODEL, 2 * D_FF), D_MODEL),
        "ffn_conv_w": jax.random.normal(ks[18], (DEPTH, CONV_W, D_FF), f32) * CONV_W ** -0.5,
        "ffn_conv_b": 0.02 * jax.random.normal(ks[19], (DEPTH, D_FF), f32),
        "ffn_w_down": w(ks[20], (DEPTH, D_FF, D_MODEL), D_FF),
    }


def reference(x, ln_mix, ln_ffn, ln_final, t5_table, ev_w_in, ev_w_out, diff_lq1, diff_lk1,
              diff_lq2, diff_lk2, diff_subln, od_w_in, od_w_out, gqa_q_norm, gqa_k_norm, na_rpb,
              ffn_w_up, ffn_conv_w, ffn_conv_b, ffn_w_down):
    rows = x.shape[1] // GRID_W
    h = x
    for layer in range(DEPTH):
        y = rmsnorm(h, ln_mix[layer])
        if layer % 2 == 0:
            e = layer // 2
            lambda_init = 0.8 - 0.6 * math.exp(-0.3 * layer)
            mix = even_mixer(y, ev_w_in[e], ev_w_out[e], diff_lq1[e], diff_lk1[e], diff_lq2[e],
                             diff_lk2[e], diff_subln[e], t5_table, lambda_init)
        else:
            o = layer // 2
            mix = odd_mixer(y, od_w_in[o], od_w_out[o], gqa_q_norm[o], gqa_k_norm[o], na_rpb[o], rows)
        h = h + mix.astype(h.dtype)
        f = conv_ffn(rmsnorm(h, ln_ffn[layer]), ffn_w_up[layer], ffn_conv_w[layer],
                     ffn_conv_b[layer], ffn_w_down[layer])
        h = h + f.astype(h.dtype)
    return rmsnorm(h, ln_final)
```

```python
import functools
import math

import numpy as np
import jax
import jax.numpy as jnp
from jax import lax
from jax.experimental import pallas as pl
from jax.experimental.pallas import tpu as pltpu

HEAD_DIM = 128
GRID_W = 64
RMS_EPS = 1e-6
NEG_INF = -1e30
T5_BUCKETS = 32
T5_MAX_DIST = 1024
DILATED_PATTERNS = ((128, 1), (512, 4), (2048, 16))
A_GROUPS = len(DILATED_PATTERNS)
A_HEADS = 8
B_HEADS = 4
B_QK_DIM = 128
B_V_DIM = 2 * B_QK_DIM
C_HEADS = 8
C_KV_HEADS = 2
ROPE_THETA = 10000.0
ROPE_AXIS_DIM = HEAD_DIM // 2
D_HEADS = 8
NA_ROWS = 8
NA_COLS = 16

LANE = 128
V7X_VMEM_LIMIT = 56 * 1024 * 1024

BF16 = jnp.bfloat16
F32 = jnp.float32


def _params(*sem):
    return pltpu.CompilerParams(dimension_semantics=sem, vmem_limit_bytes=V7X_VMEM_LIMIT)


def _rms(xf, gain):
    ms = jnp.mean(xf * xf, axis=-1, keepdims=True)
    return xf * lax.rsqrt(ms + RMS_EPS) * gain


def _dot_nt(a, b):
    return lax.dot_general(a, b, (((1,), (1,)), ((), ())), preferred_element_type=F32)


def _dot(a, b):
    return jnp.dot(a, b, preferred_element_type=F32)


def _norm_proj_kernel(x_ref, g_ref, w_ref, o_ref, xn_ref, *, n_chunks):
    @pl.when(pl.program_id(1) == 0)
    def _():
        xn_ref[...] = _rms(x_ref[...], g_ref[...]).astype(BF16)

    acc = _dot(xn_ref[...], w_ref[...])
    for c in range(n_chunks):
        o_ref[c] = acc[:, c * LANE:(c + 1) * LANE].astype(o_ref.dtype)


def _rope_chunk(y, hg, cosd, sins, even_lane):
    y = _rms(y, hg)
    partner = jnp.where(even_lane, pltpu.roll(y, LANE - 1, axis=1), pltpu.roll(y, 1, axis=1))
    return y * cosd + partner * sins


def _norm_proj_rope_kernel(x_ref, g_ref, w_ref, hg_ref, cos_ref, sin_ref, o_ref, xn_ref, *,
                           n_chunks, n_rope):
    j = pl.program_id(1)

    @pl.when(j == 0)
    def _():
        xn_ref[...] = _rms(x_ref[...], g_ref[...]).astype(BF16)

    acc = _dot(xn_ref[...], w_ref[...])

    @pl.when(j == 0)
    def _():
        cosd = cos_ref[...]
        sins = sin_ref[...]
        even_lane = (lax.broadcasted_iota(jnp.int32, cosd.shape, 1) % 2) == 0
        for c in range(n_chunks):
            y = acc[:, c * LANE:(c + 1) * LANE]
            if c < n_rope:
                y = _rope_chunk(y, hg_ref[c:c + 1, :], cosd, sins, even_lane)
            o_ref[c] = y.astype(o_ref.dtype)

    @pl.when(j > 0)
    def _():
        for c in range(n_chunks):
            o_ref[c] = acc[:, c * LANE:(c + 1) * LANE].astype(o_ref.dtype)


def norm_proj(x, gain, w, *, tm, tn, rope=None):
    t, d = x.shape
    n = w.shape[1]
    assert t % tm == 0 and n % tn == 0 and tn % LANE == 0
    n_chunks = tn // LANE
    in_specs = [
        pl.BlockSpec((tm, d), lambda i, j: (i, 0)),
        pl.BlockSpec((1, d), lambda i, j: (0, 0)),
        pl.BlockSpec((d, tn), lambda i, j: (0, j)),
    ]
    args = [x, gain.reshape(1, d), w]
    if rope is None:
        body = functools.partial(_norm_proj_kernel, n_chunks=n_chunks)
    else:
        hg, cosd, sins = rope
        n_rope = hg.shape[0]
        s = cosd.shape[0]
        assert n_rope <= n_chunks and s % tm == 0
        sb = s // tm
        in_specs += [
            pl.BlockSpec((n_rope, LANE), lambda i, j: (0, 0)),
            pl.BlockSpec((tm, LANE), lambda i, j: (i % sb, 0)),
            pl.BlockSpec((tm, LANE), lambda i, j: (i % sb, 0)),
        ]
        args += [hg, cosd, sins]
        body = functools.partial(_norm_proj_rope_kernel, n_chunks=n_chunks, n_rope=n_rope)
    return pl.pallas_call(
        body,
        grid=(t // tm, n // tn),
        in_specs=in_specs,
        out_specs=pl.BlockSpec((n_chunks, tm, LANE), lambda i, j: (j, i, 0)),
        out_shape=jax.ShapeDtypeStruct((n // LANE, t, LANE), BF16),
        scratch_shapes=[pltpu.VMEM((tm, d), BF16)],
        compiler_params=_params("parallel", "arbitrary"),
        name="norm_proj" if rope is None else "norm_proj_rope",
    )(*args)


def _t5_bucket(rel):
    nb = T5_BUCKETS // 2
    max_exact = nb // 2
    ret = jnp.where(rel > 0, nb, 0)
    n = jnp.abs(rel)
    n_f = jnp.maximum(n, 1).astype(F32)
    large = max_exact + (jnp.log(n_f / max_exact) / math.log(T5_MAX_DIST / max_exact)
                         * (nb - max_exact)).astype(jnp.int32)
    large = jnp.minimum(large, nb - 1)
    return ret + jnp.where(n < max_exact, n, large)


def _dilated_kernel(q_ref, k_ref, v_ref, bm_ref, o_ref, lse_ref, *, tq, win, half, seq_len):
    qi = pl.program_id(3)
    kstart = jnp.clip(qi * tq - half, 0, seq_len - win)
    kstart = pl.multiple_of(kstart, half)
    q = q_ref[0]
    kw = k_ref[0, pl.ds(kstart, win), :]
    vw = v_ref[0, pl.ds(kstart, win), :]
    s = _dot_nt(q, kw) * (HEAD_DIM ** -0.5) + bm_ref[0, 0]
    m = jnp.max(s, axis=-1, keepdims=True)
    p = jnp.exp(s - m)
    denom = jnp.sum(p, axis=-1, keepdims=True)
    o = _dot(p.astype(BF16), vw) / denom
    o_ref[...] = o.astype(o_ref.dtype)
    lse_ref[...] = jnp.broadcast_to(m + jnp.log(denom), lse_ref.shape)


def _dilated_bias_mask(t5_cols, dilation, half, tq, win, seq_len):
    nq = seq_len // tq
    offs = [0] if nq == 1 else [0, -half, tq - win]
    r = np.arange(tq)[:, None]
    c = np.arange(win)[None, :]
    tabs = []
    for off in offs:
        rel = (off + c) - r
        valid = np.abs(rel) <= half
        bias = t5_cols[_t5_bucket(jnp.asarray(dilation * rel, jnp.int32))]
        bias = jnp.transpose(bias, (2, 0, 1)).astype(F32)
        tabs.append(jnp.where(jnp.asarray(valid)[None], bias, NEG_INF))
    return jnp.stack(tabs)


def dilated_attention(proj, t5_cols, *, group, window, dilation, batch, seq):
    nc, t, _ = proj.shape
    seq_len = seq // dilation
    half = window // (2 * dilation)
    tq = min(256, seq_len)
    win = min(tq + 2 * half, seq_len)
    assert seq % dilation == 0 and seq_len % tq == 0 and half % 16 == 0
    assert win == seq_len or win == tq + 2 * half
    nq = seq_len // tq
    bm = _dilated_bias_mask(t5_cols, dilation, half, tq, win, seq_len)
    pv = proj.reshape(nc, batch * seq_len, dilation * LANE)
    base = group * 3 * A_HEADS

    def variant(qi):
        if nq == 1:
            return 0
        return jnp.where(qi == 0, 0, jnp.where(qi == nq - 1, 2, 1))

    out_sds = jax.ShapeDtypeStruct((batch * seq_len, dilation * A_HEADS * LANE), BF16)
    lse_sds = jax.ShapeDtypeStruct((batch * seq_len, dilation * A_HEADS * LANE), F32)
    o, lse = pl.pallas_call(
        functools.partial(_dilated_kernel, tq=tq, win=win, half=half, seq_len=seq_len),
        grid=(batch, dilation, A_HEADS, nq),
        in_specs=[
            pl.BlockSpec((1, tq, LANE), lambda b, r, h, qi: (base + h, b * nq + qi, r)),
            pl.BlockSpec((1, seq_len, LANE), lambda b, r, h, qi: (base + A_HEADS + h, b, r)),
            pl.BlockSpec((1, seq_len, LANE), lambda b, r, h, qi: (base + 2 * A_HEADS + h, b, r)),
            pl.BlockSpec((1, 1, tq, win), lambda b, r, h, qi: (variant(qi), h, 0, 0)),
        ],
        out_specs=[
            pl.BlockSpec((tq, LANE), lambda b, r, h, qi: (b * nq + qi, r * A_HEADS + h)),
            pl.BlockSpec((tq, LANE), lambda b, r, h, qi: (b * nq + qi, r * A_HEADS + h)),
        ],
        out_shape=[out_sds, lse_sds],
        compiler_params=_params("parallel", "parallel", "parallel", "arbitrary"),
        name=f"dilated_attn_g{group}",
    )(pv, pv, pv, bm)
    return o.reshape(t, A_HEADS * LANE), lse.reshape(t, A_HEADS * LANE)


def _online_step(s, vc, m, l, acc):
    m_new = jnp.maximum(m, jnp.max(s, axis=-1, keepdims=True))
    alpha = jnp.exp(m - m_new)
    p = jnp.exp(s - m_new)
    l_new = alpha * l + jnp.sum(p, axis=-1, keepdims=True)
    acc_new = alpha * acc + _dot(p.astype(BF16), vc)
    return m_new, l_new, acc_new


T5_BAND = T5_MAX_DIST


def _diff_band(tq):
    lo = -((T5_BAND + LANE - 1) // LANE) - 1
    hi = (T5_BAND + tq - 1 + LANE - 1) // LANE
    return lo, hi


def _diff_kernel(q_ref, k_ref, v_ref, tz_ref, lq1_ref, lk1_ref, lq2_ref, lk2_ref, sub_ref, o_ref, *,
                 tq, tk, seq, lambda_init):
    qi = pl.program_id(2)
    lo, hi = _diff_band(tq)
    nk = seq // tk
    sub = tk // LANE
    scale = B_QK_DIM ** -0.5
    q_chunk0 = qi * (tq // LANE)

    def attend(mi):
        q = q_ref[mi]

        def body(kj, carry):
            m, l, acc = carry
            ks = pl.multiple_of(kj * tk, tk)
            kc = k_ref[mi, pl.ds(ks, tk), :]
            vc = jnp.concatenate([v_ref[0, pl.ds(ks, tk), :], v_ref[1, pl.ds(ks, tk), :]], axis=1)
            dc0 = kj * sub - q_chunk0
            bias = jnp.concatenate(
                [tz_ref[0, jnp.clip(dc0 + cc, lo, hi) - lo] for cc in range(sub)], axis=1)
            s = _dot_nt(q, kc) * scale + bias
            return _online_step(s, vc, m, l, acc)

        init = (jnp.full((tq, 1), -jnp.inf, F32), jnp.zeros((tq, 1), F32),
                jnp.zeros((tq, B_V_DIM), F32))
        m, l, acc = lax.fori_loop(0, nk, body, init)
        return acc / l

    o1 = attend(0)
    o2 = attend(1)
    lam = (jnp.exp(jnp.sum(lq1_ref[...] * lk1_ref[...], axis=-1, keepdims=True))
           - jnp.exp(jnp.sum(lq2_ref[...] * lk2_ref[...], axis=-1, keepdims=True)) + lambda_init)
    o = o1 - lam * o2
    o_ref[...] = (_rms(o, sub_ref[...]) * (1.0 - lambda_init)).astype(o_ref.dtype)


def _diff_toeplitz(t5_cols, tq):
    lo, hi = _diff_band(tq)
    j = np.arange(lo, hi + 1)[:, None, None]
    r = np.arange(tq)[None, :, None]
    c = np.arange(LANE)[None, None, :]
    rel = LANE * j + c - r
    tz = t5_cols[_t5_bucket(jnp.asarray(rel, jnp.int32))]
    return jnp.transpose(tz, (3, 0, 1, 2)).astype(F32)


def diff_attention(proj, t5_cols, lq1, lk1, lq2, lk2, subln, *, chunk0, batch, seq, lambda_init):
    nc, t, _ = proj.shape
    tq, tk = 256, 512
    assert seq % tq == 0 and seq % tk == 0 and chunk0 % 2 == 0
    nq = seq // tq
    tz = _diff_toeplitz(t5_cols, tq)
    n_off = tz.shape[1]
    qb, kb, vb = chunk0 // 2, chunk0 // 2 + B_HEADS, chunk0 // 2 + 2 * B_HEADS
    vec = lambda a: a.reshape(1, -1).astype(F32)
    vspec = lambda n: pl.BlockSpec((1, n), lambda b, h, qi: (0, 0))
    return pl.pallas_call(
        functools.partial(_diff_kernel, tq=tq, tk=tk, seq=seq, lambda_init=lambda_init),
        grid=(batch, B_HEADS, nq),
        in_specs=[
            pl.BlockSpec((2, tq, LANE), lambda b, h, qi: (qb + h, b * nq + qi, 0)),
            pl.BlockSpec((2, seq, LANE), lambda b, h, qi: (kb + h, b, 0)),
            pl.BlockSpec((2, seq, LANE), lambda b, h, qi: (vb + h, b, 0)),
            pl.BlockSpec((1, n_off, tq, LANE), lambda b, h, qi: (h, 0, 0, 0)),
            vspec(B_QK_DIM), vspec(B_QK_DIM), vspec(B_QK_DIM), vspec(B_QK_DIM), vspec(B_V_DIM),
        ],
        out_specs=pl.BlockSpec((tq, B_V_DIM), lambda b, h, qi: (b * nq + qi, h)),
        out_shape=jax.ShapeDtypeStruct((t, B_HEADS * B_V_DIM), BF16),
        compiler_params=_params("parallel", "parallel", "arbitrary"),
        name="diff_attn",
    )(proj, proj, proj, tz, vec(lq1), vec(lk1), vec(lq2), vec(lk2), vec(subln))


def _gqa_kernel(q_ref, k_ref, v_ref, o_ref, *, tq, tk, seq, group):
    nk = seq // tk
    scale = HEAD_DIM ** -0.5
    q = q_ref[...].reshape(group * tq, HEAD_DIM)

    def body(kj, carry):
        m, l, acc = carry
        ks = pl.multiple_of(kj * tk, tk)
        s = _dot_nt(q, k_ref[0, pl.ds(ks, tk), :]) * scale
        return _online_step(s, v_ref[0, pl.ds(ks, tk), :], m, l, acc)

    init = (jnp.full((group * tq, 1), -jnp.inf, F32), jnp.zeros((group * tq, 1), F32),
            jnp.zeros((group * tq, HEAD_DIM), F32))
    m, l, acc = lax.fori_loop(0, nk, body, init)
    o = acc / l
    for g in range(group):
        o_ref[:, g * HEAD_DIM:(g + 1) * HEAD_DIM] = o[g * tq:(g + 1) * tq].astype(o_ref.dtype)


def gqa_attention(proj, *, batch, seq):
    nc, t, _ = proj.shape
    tq, tk = 128, 512
    group = C_HEADS // C_KV_HEADS
    nq = seq // tq
    return pl.pallas_call(
        functools.partial(_gqa_kernel, tq=tq, tk=tk, seq=seq, group=group),
        grid=(batch, C_KV_HEADS, nq),
        in_specs=[
            pl.BlockSpec((group, tq, LANE), lambda b, n, qi: (n, b * nq + qi, 0)),
            pl.BlockSpec((1, seq, LANE), lambda b, n, qi: (C_HEADS + n, b, 0)),
            pl.BlockSpec((1, seq, LANE), lambda b, n, qi: (C_HEADS + C_KV_HEADS + n, b, 0)),
        ],
        out_specs=pl.BlockSpec((tq, group * HEAD_DIM), lambda b, n, qi: (b * nq + qi, n)),
        out_shape=jax.ShapeDtypeStruct((t, C_HEADS * HEAD_DIM), BF16),
        compiler_params=_params("parallel", "parallel", "arbitrary"),
        name="gqa_attn",
    )(proj, proj, proj)


NA_QROWS = 4
NA_KROWS = NA_QROWS + NA_ROWS


def _na_kernel(q_ref, k_ref, v_ref, bm_ref, o_ref, *, rows):
    blk = pl.program_id(2)
    wrow = jnp.clip(blk * NA_QROWS - NA_ROWS // 2, 0, rows - NA_KROWS)
    kstart = pl.multiple_of(wrow * GRID_W, NA_QROWS * GRID_W)
    nkeys = NA_KROWS * GRID_W
    kw = k_ref[0, pl.ds(kstart, nkeys), :]
    vw = v_ref[0, pl.ds(kstart, nkeys), :]
    s = _dot_nt(q_ref[0], kw) * (HEAD_DIM ** -0.5) + bm_ref[0, 0]
    m = jnp.max(s, axis=-1, keepdims=True)
    p = jnp.exp(s - m)
    denom = jnp.sum(p, axis=-1, keepdims=True)
    o_ref[...] = (_dot(p.astype(BF16), vw) / denom).astype(o_ref.dtype)


def _na_bias_mask(rpb, rows):
    kr = min(NA_ROWS, rows)
    nblk = rows // NA_QROWS
    tabs = []
    for blk in (0, 1, nblk - 1):
        wrow = int(np.clip(blk * NA_QROWS - NA_ROWS // 2, 0, rows - NA_KROWS))
        qi = blk * NA_QROWS + np.arange(NA_QROWS)[:, None, None, None]
        qj = np.arange(GRID_W)[None, :, None, None]
        ki = wrow + np.arange(NA_KROWS)[None, None, :, None]
        kj = np.arange(GRID_W)[None, None, None, :]
        rs = np.clip(qi - kr // 2, 0, rows - kr)
        cs = np.clip(qj - NA_COLS // 2, 0, GRID_W - NA_COLS)
        valid = (ki >= rs) & (ki < rs + kr) & (kj >= cs) & (kj < cs + NA_COLS)
        dr = np.clip(ki - qi + (NA_ROWS - 1), 0, 2 * NA_ROWS - 2) + 0 * kj + 0 * qj
        dc = np.clip(kj - qj + (NA_COLS - 1), 0, 2 * NA_COLS - 2) + 0 * ki + 0 * qi
        shape = (NA_QROWS * GRID_W, NA_KROWS * GRID_W)
        valid = np.broadcast_to(valid, (NA_QROWS, GRID_W, NA_KROWS, GRID_W)).reshape(shape)
        bias = rpb[:, jnp.asarray(dr.reshape(shape)), jnp.asarray(dc.reshape(shape))].astype(F32)
        tabs.append(jnp.where(jnp.asarray(valid)[None], bias, NEG_INF))
    return jnp.stack(tabs)


def na_attention(proj, rpb, *, chunk0, batch, seq):
    nc, t, _ = proj.shape
    rows = seq // GRID_W
    assert rows % NA_QROWS == 0 and rows >= NA_KROWS and rows // NA_QROWS >= 3
    nblk = rows // NA_QROWS
    tq = NA_QROWS * GRID_W
    nkeys = NA_KROWS * GRID_W
    bm = _na_bias_mask(rpb, rows)

    def variant(blk):
        return jnp.where(blk == 0, 0, jnp.where(blk == nblk - 1, 2, 1))

    return pl.pallas_call(
        functools.partial(_na_kernel, rows=rows),
        grid=(batch, D_HEADS, nblk),
        in_specs=[
            pl.BlockSpec((1, tq, LANE), lambda b, h, blk: (chunk0 + h, b * nblk + blk, 0)),
            pl.BlockSpec((1, seq, LANE), lambda b, h, blk: (chunk0 + D_HEADS + h, b, 0)),
            pl.BlockSpec((1, seq, LANE), lambda b, h, blk: (chunk0 + 2 * D_HEADS + h, b, 0)),
            pl.BlockSpec((1, 1, tq, nkeys), lambda b, h, blk: (variant(blk), h, 0, 0)),
        ],
        out_specs=pl.BlockSpec((tq, LANE), lambda b, h, blk: (b * nblk + blk, h)),
        out_shape=jax.ShapeDtypeStruct((t, D_HEADS * HEAD_DIM), BF16),
        compiler_params=_params("parallel", "parallel", "arbitrary"),
        name="na_attn",
    )(proj, proj, proj, bm)


def _even_out_kernel(o1_ref, o2_ref, o3_ref, l1_ref, l2_ref, l3_ref, ob_ref, w_ref, h_ref, out_ref,
                     mix_ref, *, na):
    @pl.when(pl.program_id(1) == 0)
    def _():
        l1, l2, l3 = l1_ref[...], l2_ref[...], l3_ref[...]
        lmax = jnp.maximum(jnp.maximum(l1, l2), l3)
        e1, e2, e3 = jnp.exp(l1 - lmax), jnp.exp(l2 - lmax), jnp.exp(l3 - lmax)
        num = (e1 * o1_ref[...].astype(F32) + e2 * o2_ref[...].astype(F32)
               + e3 * o3_ref[...].astype(F32))
        mix_ref[:, :na] = (num / (e1 + e2 + e3)).astype(BF16)
        mix_ref[:, na:] = ob_ref[...]

    out_ref[...] = h_ref[...] + _dot(mix_ref[...], w_ref[...])


def even_out_proj(os, lses, ob, w, h, *, tm, tn):
    t, d = h.shape
    na, nb = os[0].shape[1], ob.shape[1]
    row = lambda n: pl.BlockSpec((tm, n), lambda i, j: (i, 0))
    return pl.pallas_call(
        functools.partial(_even_out_kernel, na=na),
        grid=(t // tm, d // tn),
        in_specs=[row(na)] * 6 + [row(nb),
                                  pl.BlockSpec((na + nb, tn), lambda i, j: (0, j)),
                                  pl.BlockSpec((tm, tn), lambda i, j: (i, j))],
        out_specs=pl.BlockSpec((tm, tn), lambda i, j: (i, j)),
        out_shape=jax.ShapeDtypeStruct((t, d), F32),
        scratch_shapes=[pltpu.VMEM((tm, na + nb), BF16)],
        compiler_params=_params("parallel", "arbitrary"),
        name="even_out_proj",
    )(*os, *lses, ob, w, h)


def _odd_out_kernel(oc_ref, od_ref, w_ref, h_ref, out_ref, *, nc):
    out_ref[...] = (h_ref[...] + _dot(oc_ref[...], w_ref[:nc, :]) + _dot(od_ref[...], w_ref[nc:, :]))


def odd_out_proj(oc, od, w, h, *, tm, tn):
    t, d = h.shape
    nc, nd = oc.shape[1], od.shape[1]
    return pl.pallas_call(
        functools.partial(_odd_out_kernel, nc=nc),
        grid=(t // tm, d // tn),
        in_specs=[pl.BlockSpec((tm, nc), lambda i, j: (i, 0)),
                  pl.BlockSpec((tm, nd), lambda i, j: (i, 0)),
                  pl.BlockSpec((nc + nd, tn), lambda i, j: (0, j)),
                  pl.BlockSpec((tm, tn), lambda i, j: (i, j))],
        out_specs=pl.BlockSpec((tm, tn), lambda i, j: (i, j)),
        out_shape=jax.ShapeDtypeStruct((t, d), F32),
        compiler_params=_params("parallel", "arbitrary"),
        name="odd_out_proj",
    )(oc, od, w, h)


FFN_HALO = 16


def _gelu_tanh(x):
    return 0.5 * x * (1.0 + jnp.tanh(math.sqrt(2.0 / math.pi) * (x + 0.044715 * (x * x * x))))


def _ffn_kernel(x_ref, xp_ref, xnx_ref, g_ref, wg_ref, wu_ref, cw_ref, cb_ref, wd_ref, gf_ref, out_ref,
                xn_ref, *, tm, tiles_per_seq, final_norm):
    i = pl.program_id(0)
    f = pl.program_id(1)
    nf = pl.num_programs(1)
    hl = FFN_HALO

    @pl.when(f == 0)
    def _():
        gain = g_ref[...]
        keep_prev = (i % tiles_per_seq != 0).astype(F32)
        keep_next = (i % tiles_per_seq != tiles_per_seq - 1).astype(F32)
        xn_ref[:hl, :] = (_rms(xp_ref[...], gain) * keep_prev).astype(BF16)
        xn_ref[hl:hl + tm, :] = _rms(x_ref[...], gain).astype(BF16)
        xn_ref[hl + tm:, :] = (_rms(xnx_ref[...], gain) * keep_next).astype(BF16)
        out_ref[...] = x_ref[...]

    ge = _dot(xn_ref[...], wg_ref[...])
    cw = cw_ref[...]
    g = (cw[0:1, :] * ge[hl - 1:hl - 1 + tm] + cw[1:2, :] * ge[hl:hl + tm]
         + cw[2:3, :] * ge[hl + 1:hl + 1 + tm] + cb_ref[...])
    u = _dot(xn_ref[hl:hl + tm, :], wu_ref[...])
    act = (_gelu_tanh(g) * u).astype(BF16)
    out_ref[...] += _dot(act, wd_ref[...])

    if final_norm:
        @pl.when(f == nf - 1)
        def _():
            out_ref[...] = _rms(out_ref[...], gf_ref[...])


def conv_ffn_block(h, gain, w_up, conv_w, conv_b, w_down, final_gain, *, seq, tm, tf, final_norm):
    t, d = h.shape
    ff = w_down.shape[0]
    assert t % tm == 0 and seq % tm == 0 and ff % tf == 0 and tm % FFN_HALO == 0
    nfb = ff // tf
    hb = tm // FFN_HALO
    last_halo = t // FFN_HALO - 1
    return pl.pallas_call(
        functools.partial(_ffn_kernel, tm=tm, tiles_per_seq=seq // tm, final_norm=final_norm),
        grid=(t // tm, nfb),
        in_specs=[
            pl.BlockSpec((tm, d), lambda i, f: (i, 0)),
            pl.BlockSpec((FFN_HALO, d), lambda i, f: (jnp.maximum(i * hb - 1, 0), 0)),
            pl.BlockSpec((FFN_HALO, d), lambda i, f: (jnp.minimum((i + 1) * hb, last_halo), 0)),
            pl.BlockSpec((1, d), lambda i, f: (0, 0)),
            pl.BlockSpec((d, tf), lambda i, f: (0, f)),
            pl.BlockSpec((d, tf), lambda i, f: (0, nfb + f)),
            pl.BlockSpec((3, tf), lambda i, f: (0, f)),
            pl.BlockSpec((1, tf), lambda i, f: (0, f)),
            pl.BlockSpec((tf, d), lambda i, f: (f, 0)),
            pl.BlockSpec((1, d), lambda i, f: (0, 0)),
        ],
        out_specs=pl.BlockSpec((tm, d), lambda i, f: (i, 0)),
        out_shape=jax.ShapeDtypeStruct((t, d), F32),
        scratch_shapes=[pltpu.VMEM((tm + 2 * FFN_HALO, d), BF16)],
        compiler_params=_params("parallel", "arbitrary"),
        name="conv_ffn",
    )(h, h, h, gain.reshape(1, d), w_up, w_up, conv_w, conv_b.reshape(1, ff), w_down,
      final_gain.reshape(1, d))


def _rope_tables(seq):
    tpos = jnp.arange(seq, dtype=jnp.int32)
    row = (tpos // GRID_W).astype(F32)
    col = (tpos % GRID_W).astype(F32)
    inv_freq = ROPE_THETA ** (-(jnp.arange(0, ROPE_AXIS_DIM, 2, dtype=F32) / ROPE_AXIS_DIM))
    ang = jnp.concatenate([row[:, None] * inv_freq[None], col[:, None] * inv_freq[None]], axis=-1)
    cos, sin = jnp.cos(ang), jnp.sin(ang)
    cos_dup = jnp.repeat(cos, 2, axis=-1)
    sin_signed = jnp.stack([-sin, sin], axis=-1).reshape(seq, HEAD_DIM)
    return cos_dup, sin_signed


def kernel(x, ln_mix, ln_ffn, ln_final, t5_table, ev_w_in, ev_w_out, diff_lq1, diff_lk1, diff_lq2,
           diff_lk2, diff_subln, od_w_in, od_w_out, gqa_q_norm, gqa_k_norm, na_rpb, ffn_w_up,
           ffn_conv_w, ffn_conv_b, ffn_w_down):
    batch, seq, d = x.shape
    depth = ln_mix.shape[0]
    t = batch * seq
    h = x.reshape(t, d)
    a_chunks = A_GROUPS * 3 * A_HEADS

    for layer in range(depth):
        if layer % 2 == 0:
            e = layer // 2
            lambda_init = 0.8 - 0.6 * math.exp(-0.3 * layer)
            proj = norm_proj(h, ln_mix[layer], ev_w_in[e].astype(BF16), tm=1024, tn=1024)
            os, lses = [], []
            for gi, (window, dilation) in enumerate(DILATED_PATTERNS):
                o, lse = dilated_attention(proj, t5_table[:, gi * A_HEADS:(gi + 1) * A_HEADS],
                                           group=gi, window=window, dilation=dilation,
                                           batch=batch, seq=seq)
                os.append(o)
                lses.append(lse)
            ob = diff_attention(proj, t5_table[:, A_GROUPS * A_HEADS:], diff_lq1[e], diff_lk1[e],
                                diff_lq2[e], diff_lk2[e], diff_subln[e], chunk0=a_chunks,
                                batch=batch, seq=seq, lambda_init=lambda_init)
            h = even_out_proj(os, lses, ob, ev_w_out[e].astype(BF16), h, tm=512, tn=1024)
        else:
            o = layer // 2
            n_rope = C_HEADS + C_KV_HEADS
            head_gain = jnp.concatenate([jnp.tile(gqa_q_norm[o][None], (C_HEADS, 1)),
                                         jnp.tile(gqa_k_norm[o][None], (C_KV_HEADS, 1))], axis=0)
            cos_dup, sin_signed = _rope_tables(seq)
            proj = norm_proj(h, ln_mix[layer], od_w_in[o].astype(BF16), tm=1024, tn=1536,
                             rope=(head_gain.astype(F32), cos_dup, sin_signed))
            oc = gqa_attention(proj, batch=batch, seq=seq)
            od = na_attention(proj, na_rpb[o], chunk0=n_rope + C_KV_HEADS, batch=batch, seq=seq)
            h = odd_out_proj(oc, od, od_w_out[o].astype(BF16), h, tm=512, tn=1024)
        h = conv_ffn_block(h, ln_ffn[layer], ffn_w_up[layer].astype(BF16), ffn_conv_w[layer],
                           ffn_conv_b[layer], ffn_w_down[layer].astype(BF16), ln_final,
                           seq=seq, tm=512, tf=512, final_norm=(layer == depth - 1))
    return h.reshape(batch, seq, d)
```

```python
import functools
import math

import numpy as np
import jax
import jax.numpy as jnp
from jax import lax
from jax.experimental import pallas as pl
from jax.experimental.pallas import tpu as pltpu

HEAD_DIM = 128
GRID_W = 64
RMS_EPS = 1e-6
NEG_INF = -1e30
T5_BUCKETS = 32
T5_MAX_DIST = 1024
DILATED_PATTERNS = ((128, 1), (512, 4), (2048, 16))
A_GROUPS = len(DILATED_PATTERNS)
A_HEADS = 8
B_HEADS = 4
B_QK_DIM = 128
B_V_DIM = 2 * B_QK_DIM
C_HEADS = 8
C_KV_HEADS = 2
ROPE_THETA = 10000.0
ROPE_AXIS_DIM = HEAD_DIM // 2
D_HEADS = 8
NA_ROWS = 8
NA_COLS = 16

LANE = 128
V7X_VMEM_LIMIT = 56 * 1024 * 1024

BF16 = jnp.bfloat16
F32 = jnp.float32


def _params(*sem):
    return pltpu.CompilerParams(dimension_semantics=sem, vmem_limit_bytes=V7X_VMEM_LIMIT)


def _rms(xf, gain):
    ms = jnp.mean(xf * xf, axis=-1, keepdims=True)
    return xf * lax.rsqrt(ms + RMS_EPS) * gain


def _dot_nt(a, b):
    return lax.dot_general(a, b, (((1,), (1,)), ((), ())), preferred_element_type=F32)


def _dot(a, b):
    return jnp.dot(a, b, preferred_element_type=F32)


def _lanes(x, width):
    return x if width == LANE else jnp.concatenate([x] * (width // LANE), axis=1)


def _t5_bucket(rel):
    nb = T5_BUCKETS // 2
    max_exact = nb // 2
    ret = jnp.where(rel > 0, nb, 0)
    n = jnp.abs(rel)
    n_f = jnp.maximum(n, 1).astype(F32)
    large = max_exact + (jnp.log(n_f / max_exact) / math.log(T5_MAX_DIST / max_exact)
                         * (nb - max_exact)).astype(jnp.int32)
    large = jnp.minimum(large, nb - 1)
    return ret + jnp.where(n < max_exact, n, large)


def _toeplitz(fv, rows, cols):
    n = rows + cols - 1
    assert fv.shape[-1] == n
    v = jnp.concatenate([fv[..., rows - 1:], fv[..., :rows - 1]], axis=-1)
    flat = jnp.tile(v, rows)[..., :rows * (n - 1)]
    return flat.reshape(fv.shape[:-1] + (rows, n - 1))[..., :cols]


def _store_chunks(acc, o_ref, n_chunks):
    for c in range(n_chunks):
        o_ref[c] = acc[:, c * LANE:(c + 1) * LANE].astype(o_ref.dtype)


def _even_proj_kernel(x_ref, g_ref, w_ref, nat_ref, *rest, plan, n_chunks, tm):
    dil_refs, (xn_ref, slab_ref) = rest[:-2], rest[-2:]
    j = pl.program_id(1)

    @pl.when(j == 0)
    def _():
        xn_ref[...] = _rms(x_ref[...], g_ref[...]).astype(BF16)

    acc = _dot(xn_ref[...], w_ref[...])

    for lo, hi, dilation, out_idx in plan:
        @pl.when((j >= lo) & (j < hi))
        def _(dilation=dilation, out_idx=out_idx):
            if dilation == 1:
                _store_chunks(acc, nat_ref, n_chunks)
                return
            o_ref = dil_refs[out_idx]
            per = tm // dilation
            for c in range(n_chunks):
                slab_ref[c] = acc[:, c * LANE:(c + 1) * LANE]
            for c in range(n_chunks):
                for r in range(dilation):
                    o_ref[c, 0, r] = slab_ref[c, pl.ds(r, per, stride=dilation), :].astype(o_ref.dtype)


def even_norm_proj(x, gain, w, *, batch, seq, tm, tn):
    t, d = x.shape
    n = w.shape[1]
    n_chunks = tn // LANE
    group_cols = 3 * A_HEADS * HEAD_DIM
    assert group_cols % tn == 0 and n % tn == 0 and seq % tm == 0 and t == batch * seq
    gt = group_cols // tn
    nj = n // tn
    tiles_per_seq = seq // tm
    dils = [dil for _, dil in DILATED_PATTERNS]
    assert dils[0] == 1 and all(tm % (16 * dil) == 0 for dil in dils)
    plan = [(0, gt, 1, None)]
    for gi in range(1, A_GROUPS):
        plan.append((gi * gt, (gi + 1) * gt, dils[gi], gi - 1))
    plan.append((A_GROUPS * gt, nj, 1, None))
    nat_tiles = gt + (nj - A_GROUPS * gt)

    def nat_map(i, j):
        return (jnp.where(j < gt, j, jnp.where(j < A_GROUPS * gt, gt - 1, j - (A_GROUPS - 1) * gt)), i, 0)

    def dil_map(gi):
        return lambda i, j: (jnp.clip(j - gi * gt, 0, gt - 1), i // tiles_per_seq, 0, i % tiles_per_seq, 0)

    out_specs = [pl.BlockSpec((n_chunks, tm, LANE), nat_map)]
    out_shape = [jax.ShapeDtypeStruct((nat_tiles * n_chunks, t, LANE), BF16)]
    for gi in range(1, A_GROUPS):
        dil = dils[gi]
        out_specs.append(pl.BlockSpec((n_chunks, 1, dil, tm // dil, LANE), dil_map(gi)))
        out_shape.append(jax.ShapeDtypeStruct((gt * n_chunks, batch, dil, seq // dil, LANE), BF16))
    return pl.pallas_call(
        functools.partial(_even_proj_kernel, plan=tuple(plan), n_chunks=n_chunks, tm=tm),
        grid=(t // tm, nj),
        in_specs=[
            pl.BlockSpec((tm, d), lambda i, j: (i, 0)),
            pl.BlockSpec((1, d), lambda i, j: (0, 0)),
            pl.BlockSpec((d, tn), lambda i, j: (0, j)),
        ],
        out_specs=out_specs,
        out_shape=out_shape,
        scratch_shapes=[pltpu.VMEM((tm, d), BF16), pltpu.VMEM((n_chunks, tm, LANE), F32)],
        compiler_params=_params("parallel", "arbitrary"),
        name="even_norm_proj",
    )(x, gain.reshape(1, d), w)


def _rope_chunk(y, hg, cosd, sins, even_lane):
    y = _rms(y, hg)
    partner = jnp.where(even_lane, pltpu.roll(y, LANE - 1, axis=1), pltpu.roll(y, 1, axis=1))
    return y * cosd + partner * sins


def _odd_proj_kernel(x_ref, g_ref, w_ref, hg_ref, cos_ref, sin_ref, o_ref, xn_ref, *,
                     n_chunks, n_rope):
    j = pl.program_id(1)

    @pl.when(j == 0)
    def _():
        xn_ref[...] = _rms(x_ref[...], g_ref[...]).astype(BF16)

    acc = _dot(xn_ref[...], w_ref[...])

    @pl.when(j == 0)
    def _():
        cosd = cos_ref[...]
        sins = sin_ref[...]
        even_lane = (lax.broadcasted_iota(jnp.int32, cosd.shape, 1) % 2) == 0
        for c in range(n_chunks):
            y = acc[:, c * LANE:(c + 1) * LANE]
            if c < n_rope:
                y = _rope_chunk(y, hg_ref[c:c + 1, :], cosd, sins, even_lane)
            o_ref[c] = y.astype(o_ref.dtype)

    @pl.when(j > 0)
    def _():
        _store_chunks(acc, o_ref, n_chunks)


def odd_norm_proj(x, gain, w, head_gain, cos_dup, sin_signed, *, tm, tn):
    t, d = x.shape
    n = w.shape[1]
    assert t % tm == 0 and n % tn == 0 and tn % LANE == 0
    n_chunks = tn // LANE
    n_rope = head_gain.shape[0]
    s = cos_dup.shape[0]
    assert n_rope <= n_chunks and s % tm == 0
    sb = s // tm
    return pl.pallas_call(
        functools.partial(_odd_proj_kernel, n_chunks=n_chunks, n_rope=n_rope),
        grid=(t // tm, n // tn),
        in_specs=[
            pl.BlockSpec((tm, d), lambda i, j: (i, 0)),
            pl.BlockSpec((1, d), lambda i, j: (0, 0)),
            pl.BlockSpec((d, tn), lambda i, j: (0, j)),
            pl.BlockSpec((n_rope, LANE), lambda i, j: (0, 0)),
            pl.BlockSpec((tm, LANE), lambda i, j: (i % sb, 0)),
            pl.BlockSpec((tm, LANE), lambda i, j: (i % sb, 0)),
        ],
        out_specs=pl.BlockSpec((n_chunks, tm, LANE), lambda i, j: (j, i, 0)),
        out_shape=jax.ShapeDtypeStruct((n // LANE, t, LANE), BF16),
        scratch_shapes=[pltpu.VMEM((tm, d), BF16)],
        compiler_params=_params("parallel", "arbitrary"),
        name="odd_norm_proj",
    )(x, gain.reshape(1, d), w, head_gain, cos_dup, sin_signed)


A_TQ = 256


def _a_geometry(seq, window, dilation):
    sub = seq // dilation
    half = window // (2 * dilation)
    tq = min(A_TQ, sub)
    win = min(tq + 2 * half, sub)
    assert seq % dilation == 0 and sub % tq == 0 and half % 16 == 0 and win % LANE == 0
    assert win == sub or win == tq + 2 * half
    return sub, half, tq, win


def _dilated_kernel(*refs, geoms):
    ng = len(geoms)
    qkv = refs[:3 * ng]
    bms = refs[3 * ng:4 * ng]
    o_ref = refs[4 * ng]
    m_s, l_s, acc_s = refs[4 * ng + 1:]
    scale = HEAD_DIM ** -0.5

    order = sorted(range(ng), key=lambda g: -geoms[g][0])
    for pos, g in enumerate(order):
        dilation, sub, half, tq, win = geoms[g]
        q_ref, k_ref, v_ref = qkv[3 * g:3 * g + 3]
        bm_ref = bms[g]
        nq = sub // tq
        first = pos == 0

        def block(it, carry, dilation=dilation, sub=sub, half=half, tq=tq, win=win, nq=nq,
                  q_ref=q_ref, k_ref=k_ref, v_ref=v_ref, bm_ref=bm_ref, first=first):
            r = it // nq
            qi = it % nq
            q0 = pl.multiple_of(qi * tq, tq)
            kstart = pl.multiple_of(jnp.clip(q0 - half, 0, sub - win), half)
            if dilation == 1:
                q = q_ref[0, pl.ds(q0, tq), :]
                kw = k_ref[0, pl.ds(kstart, win), :]
                vw = v_ref[0, pl.ds(kstart, win), :]
                rows = pl.ds(q0, tq)
            else:
                q = q_ref[0, 0, r, pl.ds(q0, tq), :]
                kw = k_ref[0, 0, r, pl.ds(kstart, win), :]
                vw = v_ref[0, 0, r, pl.ds(kstart, win), :]
                rows = pl.ds(q0 * dilation + r, tq, stride=dilation)
            variant = 0 if nq == 1 else jnp.where(qi == 0, 0, jnp.where(qi == nq - 1, 2, 1))
            s = _dot_nt(q, kw) * scale + bm_ref[variant, 0]
            mb = jnp.max(s, axis=-1, keepdims=True)
            if first:
                p = jnp.exp(s - mb)
                m_s[rows, :] = jnp.broadcast_to(mb, (tq, LANE))
                l_s[rows, :] = jnp.broadcast_to(jnp.sum(p, axis=-1, keepdims=True), (tq, LANE))
                acc_s[rows, :] = _dot(p.astype(BF16), vw)
            else:
                m_old = m_s[rows, :]
                m_new = jnp.maximum(m_old, mb)
                alpha = jnp.exp(m_old - m_new)
                p = jnp.exp(s - _lanes(m_new, win))
                m_s[rows, :] = m_new
                l_s[rows, :] = alpha * l_s[rows, :] + jnp.sum(p, axis=-1, keepdims=True)
                acc_s[rows, :] = alpha * acc_s[rows, :] + _dot(p.astype(BF16), vw)
            return carry

        lax.fori_loop(0, dilation * nq, block, 0)

    o_ref[...] = (acc_s[...] / l_s[...]).astype(o_ref.dtype)


def _dilated_bias_mask(t5_cols, dilation, half, tq, win, sub):
    nq = sub // tq
    offs = [0] if nq == 1 else [0, -half, tq - win]
    tabs = []
    for off in offs:
        x = np.arange(tq + win - 1) - (tq - 1) + off
        fv = jnp.where(jnp.asarray(np.abs(x) <= half)[None],
                       t5_cols[_t5_bucket(jnp.asarray(dilation * x, jnp.int32))].T.astype(F32), NEG_INF)
        tabs.append(_toeplitz(fv, tq, win))
    return jnp.stack(tabs)


def dilated_attention(nat, dil_projs, t5_table, *, batch, seq):
    t = batch * seq
    geoms, args, in_specs, bms = [], [], [], []
    for gi, (window, dilation) in enumerate(DILATED_PATTERNS):
        sub, half, tq, win = _a_geometry(seq, window, dilation)
        geoms.append((dilation, sub, half, tq, win))
        for part in range(3):
            if dilation == 1:
                args.append(nat)
                in_specs.append(pl.BlockSpec(
                    (1, seq, LANE), lambda b, h, part=part: (part * A_HEADS + h, b, 0)))
            else:
                args.append(dil_projs[gi - 1])
                in_specs.append(pl.BlockSpec(
                    (1, 1, dilation, sub, LANE),
                    lambda b, h, part=part: (part * A_HEADS + h, b, 0, 0, 0)))
        bms.append(_dilated_bias_mask(t5_table[:, gi * A_HEADS:(gi + 1) * A_HEADS],
                                      dilation, half, tq, win, sub))
    for bm in bms:
        args.append(bm)
        in_specs.append(pl.BlockSpec((bm.shape[0], 1) + bm.shape[2:], lambda b, h: (0, h, 0, 0)))
    return pl.pallas_call(
        functools.partial(_dilated_kernel, geoms=tuple(geoms)),
        grid=(batch, A_HEADS),
        in_specs=in_specs,
        out_specs=pl.BlockSpec((seq, LANE), lambda b, h: (b, h)),
        out_shape=jax.ShapeDtypeStruct((t, A_HEADS * LANE), BF16),
        scratch_shapes=[pltpu.VMEM((seq, LANE), F32)] * 3,
        compiler_params=_params("parallel", "arbitrary"),
        name="dilated_attn",
    )(*args)


def _online_step(s, vc, m, l, acc):
    m_new = jnp.maximum(m, jnp.max(s, axis=-1, keepdims=True))
    alpha = jnp.exp(m - m_new)
    p = jnp.exp(s - m_new)
    l_new = alpha * l + jnp.sum(p, axis=-1, keepdims=True)
    acc_new = alpha * acc + _dot(p.astype(BF16), vc)
    return m_new, l_new, acc_new


def _online_init(rows, width):
    return (jnp.full((rows, 1), -jnp.inf, F32), jnp.zeros((rows, 1), F32),
            jnp.zeros((rows, width), F32))


T5_BAND = T5_MAX_DIST


def _diff_band(tq):
    lo = -((T5_BAND + LANE - 1) // LANE) - 1
    hi = (T5_BAND + tq - 1 + LANE - 1) // LANE
    return lo, hi


def _diff_kernel(q_ref, k_ref, v_ref, tz_ref, lq1_ref, lk1_ref, lq2_ref, lk2_ref, sub_ref, o_ref, *,
                 tq, tk, seq, lambda_init):
    qi = pl.program_id(2)
    lo, hi = _diff_band(tq)
    nk = seq // tk
    sub = tk // LANE
    scale = B_QK_DIM ** -0.5
    q_chunk0 = qi * (tq // LANE)
    q1 = q_ref[0]
    q2 = q_ref[1]

    def body(kj, carry):
        st1, st2 = carry
        ks = pl.multiple_of(kj * tk, tk)
        vc = jnp.concatenate([v_ref[0, pl.ds(ks, tk), :], v_ref[1, pl.ds(ks, tk), :]], axis=1)
        dc0 = kj * sub - q_chunk0
        bias = jnp.concatenate(
            [tz_ref[0, jnp.clip(dc0 + cc, lo, hi) - lo] for cc in range(sub)], axis=1)
        s1 = _dot_nt(q1, k_ref[0, pl.ds(ks, tk), :]) * scale + bias
        s2 = _dot_nt(q2, k_ref[1, pl.ds(ks, tk), :]) * scale + bias
        return _online_step(s1, vc, *st1), _online_step(s2, vc, *st2)

    init = _online_init(tq, B_V_DIM)
    (_, l1, acc1), (_, l2, acc2) = lax.fori_loop(0, nk, body, (init, init))
    lam = (jnp.exp(jnp.sum(lq1_ref[...] * lk1_ref[...], axis=-1, keepdims=True))
           - jnp.exp(jnp.sum(lq2_ref[...] * lk2_ref[...], axis=-1, keepdims=True)) + lambda_init)
    o = acc1 / l1 - lam * (acc2 / l2)
    o_ref[...] = (_rms(o, sub_ref[...]) * (1.0 - lambda_init)).astype(o_ref.dtype)


def _diff_toeplitz(t5_cols, tq):
    lo, hi = _diff_band(tq)
    n_off = hi - lo + 1
    x = np.arange(tq + n_off * LANE - 1) - (tq - 1) + LANE * lo
    fv = t5_cols[_t5_bucket(jnp.asarray(x, jnp.int32))].T.astype(F32)
    tz = _toeplitz(fv, tq, n_off * LANE)
    return jnp.transpose(tz.reshape(-1, tq, n_off, LANE), (0, 2, 1, 3))


def diff_attention(proj, t5_cols, lq1, lk1, lq2, lk2, subln, *, chunk0, batch, seq, lambda_init):
    nc, t, _ = proj.shape
    tq, tk = 256, 512
    assert seq % tq == 0 and seq % tk == 0 and chunk0 % 2 == 0
    nq = seq // tq
    tz = _diff_toeplitz(t5_cols, tq)
    n_off = tz.shape[1]
    qb, kb, vb = chunk0 // 2, chunk0 // 2 + B_HEADS, chunk0 // 2 + 2 * B_HEADS
    vec = lambda a: a.reshape(1, -1).astype(F32)
    vspec = lambda n: pl.BlockSpec((1, n), lambda b, h, qi: (0, 0))
    return pl.pallas_call(
        functools.partial(_diff_kernel, tq=tq, tk=tk, seq=seq, lambda_init=lambda_init),
        grid=(batch, B_HEADS, nq),
        in_specs=[
            pl.BlockSpec((2, tq, LANE), lambda b, h, qi: (qb + h, b * nq + qi, 0)),
            pl.BlockSpec((2, seq, LANE), lambda b, h, qi: (kb + h, b, 0)),
            pl.BlockSpec((2, seq, LANE), lambda b, h, qi: (vb + h, b, 0)),
            pl.BlockSpec((1, n_off, tq, LANE), lambda b, h, qi: (h, 0, 0, 0)),
            vspec(B_QK_DIM), vspec(B_QK_DIM), vspec(B_QK_DIM), vspec(B_QK_DIM), vspec(B_V_DIM),
        ],
        out_specs=pl.BlockSpec((tq, B_V_DIM), lambda b, h, qi: (b * nq + qi, h)),
        out_shape=jax.ShapeDtypeStruct((t, B_HEADS * B_V_DIM), BF16),
        compiler_params=_params("parallel", "parallel", "arbitrary"),
        name="diff_attn",
    )(proj, proj, proj, tz, vec(lq1), vec(lk1), vec(lq2), vec(lk2), vec(subln))


def _gqa_kernel(q_ref, k_ref, v_ref, o_ref, *, tq, tk, seq, group):
    nk = seq // tk
    scale = HEAD_DIM ** -0.5
    half = group // 2
    qa = q_ref[:half].reshape(half * tq, HEAD_DIM)
    qb = q_ref[half:].reshape((group - half) * tq, HEAD_DIM)

    def body(kj, carry):
        sta, stb = carry
        ks = pl.multiple_of(kj * tk, tk)
        kc = k_ref[0, pl.ds(ks, tk), :]
        vc = v_ref[0, pl.ds(ks, tk), :]
        return (_online_step(_dot_nt(qa, kc) * scale, vc, *sta),
                _online_step(_dot_nt(qb, kc) * scale, vc, *stb))

    (_, la, acca), (_, lb, accb) = lax.fori_loop(
        0, nk, body, (_online_init(half * tq, HEAD_DIM), _online_init((group - half) * tq, HEAD_DIM)))
    oa = acca / la
    ob = accb / lb
    for g in range(group):
        src = oa[g * tq:(g + 1) * tq] if g < half else ob[(g - half) * tq:(g - half + 1) * tq]
        o_ref[:, g * HEAD_DIM:(g + 1) * HEAD_DIM] = src.astype(o_ref.dtype)


def gqa_attention(proj, *, batch, seq):
    nc, t, _ = proj.shape
    tq, tk = 128, 512
    group = C_HEADS // C_KV_HEADS
    nq = seq // tq
    return pl.pallas_call(
        functools.partial(_gqa_kernel, tq=tq, tk=tk, seq=seq, group=group),
        grid=(batch, C_KV_HEADS, nq),
        in_specs=[
            pl.BlockSpec((group, tq, LANE), lambda b, n, qi: (n, b * nq + qi, 0)),
            pl.BlockSpec((1, seq, LANE), lambda b, n, qi: (C_HEADS + n, b, 0)),
            pl.BlockSpec((1, seq, LANE), lambda b, n, qi: (C_HEADS + C_KV_HEADS + n, b, 0)),
        ],
        out_specs=pl.BlockSpec((tq, group * HEAD_DIM), lambda b, n, qi: (b * nq + qi, n)),
        out_shape=jax.ShapeDtypeStruct((t, C_HEADS * HEAD_DIM), BF16),
        compiler_params=_params("parallel", "parallel", "arbitrary"),
        name="gqa_attn",
    )(proj, proj, proj)


NA_QROWS = 4
NA_KROWS = NA_QROWS + NA_ROWS


def _na_kernel(q_ref, k_ref, v_ref, bm_ref, o_ref, *, rows):
    blk = pl.program_id(2)
    wrow = jnp.clip(blk * NA_QROWS - NA_ROWS // 2, 0, rows - NA_KROWS)
    kstart = pl.multiple_of(wrow * GRID_W, NA_QROWS * GRID_W)
    nkeys = NA_KROWS * GRID_W
    kw = k_ref[0, pl.ds(kstart, nkeys), :]
    vw = v_ref[0, pl.ds(kstart, nkeys), :]
    s = _dot_nt(q_ref[0], kw) * (HEAD_DIM ** -0.5) + bm_ref[0, 0]
    m = jnp.max(s, axis=-1, keepdims=True)
    p = jnp.exp(s - m)
    denom = jnp.sum(p, axis=-1, keepdims=True)
    o_ref[...] = (_dot(p.astype(BF16), vw) / denom).astype(o_ref.dtype)


def _na_bias_mask(rpb, rows):
    kr = min(NA_ROWS, rows)
    nblk = rows // NA_QROWS
    nheads = rpb.shape[0]
    pad = GRID_W - NA_COLS
    col_blocks = _toeplitz(jnp.pad(rpb.astype(F32), ((0, 0), (0, 0), (pad, pad))), GRID_W, GRID_W)
    qj = np.arange(GRID_W)[:, None]
    kj = np.arange(GRID_W)[None, :]
    cs = np.clip(qj - NA_COLS // 2, 0, GRID_W - NA_COLS)
    col_ok = jnp.asarray((kj >= cs) & (kj < cs + NA_COLS))
    col_blocks = jnp.where(col_ok[None, None], col_blocks, NEG_INF)
    masked = jnp.full((nheads, GRID_W, GRID_W), NEG_INF, F32)
    tabs = []
    for blk in (0, 1, nblk - 1):
        wrow = int(np.clip(blk * NA_QROWS - NA_ROWS // 2, 0, rows - NA_KROWS))
        row_strips = []
        for a in range(NA_QROWS):
            qi = blk * NA_QROWS + a
            rs = int(np.clip(qi - kr // 2, 0, rows - kr))
            strip = []
            for c in range(NA_KROWS):
                ki = wrow + c
                strip.append(col_blocks[:, ki - qi + NA_ROWS - 1] if rs <= ki < rs + kr else masked)
            row_strips.append(jnp.concatenate(strip, axis=-1))
        tabs.append(jnp.concatenate(row_strips, axis=-2))
    return jnp.stack(tabs)


def na_attention(proj, rpb, *, chunk0, batch, seq):
    nc, t, _ = proj.shape
    rows = seq // GRID_W
    assert rows % NA_QROWS == 0 and rows >= NA_KROWS and rows // NA_QROWS >= 3
    nblk = rows // NA_QROWS
    tq = NA_QROWS * GRID_W
    nkeys = NA_KROWS * GRID_W
    bm = _na_bias_mask(rpb, rows)

    def variant(blk):
        return jnp.where(blk == 0, 0, jnp.where(blk == nblk - 1, 2, 1))

    return pl.pallas_call(
        functools.partial(_na_kernel, rows=rows),
        grid=(batch, D_HEADS, nblk),
        in_specs=[
            pl.BlockSpec((1, tq, LANE), lambda b, h, blk: (chunk0 + h, b * nblk + blk, 0)),
            pl.BlockSpec((1, seq, LANE), lambda b, h, blk: (chunk0 + D_HEADS + h, b, 0)),
            pl.BlockSpec((1, seq, LANE), lambda b, h, blk: (chunk0 + 2 * D_HEADS + h, b, 0)),
            pl.BlockSpec((1, 1, tq, nkeys), lambda b, h, blk: (variant(blk), h, 0, 0)),
        ],
        out_specs=pl.BlockSpec((tq, LANE), lambda b, h, blk: (b * nblk + blk, h)),
        out_shape=jax.ShapeDtypeStruct((t, D_HEADS * HEAD_DIM), BF16),
        compiler_params=_params("parallel", "parallel", "arbitrary"),
        name="na_attn",
    )(proj, proj, proj, bm)


def _out_proj_kernel(a_ref, b_ref, w_ref, h_ref, out_ref, *, na):
    out_ref[...] = h_ref[...] + _dot(a_ref[...], w_ref[:na, :]) + _dot(b_ref[...], w_ref[na:, :])


def out_proj(a, b, w, h, *, tm, tn):
    t, d = h.shape
    na, nb = a.shape[1], b.shape[1]
    return pl.pallas_call(
        functools.partial(_out_proj_kernel, na=na),
        grid=(t // tm, d // tn),
        in_specs=[pl.BlockSpec((tm, na), lambda i, j: (i, 0)),
                  pl.BlockSpec((tm, nb), lambda i, j: (i, 0)),
                  pl.BlockSpec((na + nb, tn), lambda i, j: (0, j)),
                  pl.BlockSpec((tm, tn), lambda i, j: (i, j))],
        out_specs=pl.BlockSpec((tm, tn), lambda i, j: (i, j)),
        out_shape=jax.ShapeDtypeStruct((t, d), F32),
        compiler_params=_params("parallel", "arbitrary"),
        name="out_proj",
    )(a, b, w, h)


FFN_HALO = 16


def _gelu_tanh(x):
    return 0.5 * x * (1.0 + jnp.tanh(math.sqrt(2.0 / math.pi) * (x + 0.044715 * (x * x * x))))


def _ffn_kernel(x_ref, xp_ref, xnx_ref, g_ref, wg_ref, wu_ref, cw_ref, cb_ref, wd_ref, gf_ref, out_ref,
                xn_ref, *, tm, tiles_per_seq, final_norm):
    i = pl.program_id(0)
    f = pl.program_id(1)
    nf = pl.num_programs(1)
    hl = FFN_HALO

    @pl.when(f == 0)
    def _():
        gain = g_ref[...]
        keep_prev = jnp.where(i % tiles_per_seq != 0, 1.0, 0.0)
        keep_next = jnp.where(i % tiles_per_seq != tiles_per_seq - 1, 1.0, 0.0)
        xn_ref[:hl, :] = (_rms(xp_ref[...], gain) * keep_prev).astype(BF16)
        xn_ref[hl:hl + tm, :] = _rms(x_ref[...], gain).astype(BF16)
        xn_ref[hl + tm:, :] = (_rms(xnx_ref[...], gain) * keep_next).astype(BF16)
        out_ref[...] = x_ref[...]

    ge = _dot(xn_ref[...], wg_ref[...])
    cw = cw_ref[...]
    g = (cw[0:1, :] * ge[hl - 1:hl - 1 + tm] + cw[1:2, :] * ge[hl:hl + tm]
         + cw[2:3, :] * ge[hl + 1:hl + 1 + tm] + cb_ref[...])
    u = _dot(xn_ref[hl:hl + tm, :], wu_ref[...])
    act = (_gelu_tanh(g) * u).astype(BF16)
    out_ref[...] += _dot(act, wd_ref[...])

    if final_norm:
        @pl.when(f == nf - 1)
        def _():
            out_ref[...] = _rms(out_ref[...], gf_ref[...])


def conv_ffn_block(h, gain, w_up, conv_w, conv_b, w_down, final_gain, *, seq, tm, tf, final_norm):
    t, d = h.shape
    ff = w_down.shape[0]
    assert t % tm == 0 and seq % tm == 0 and ff % tf == 0 and tm % FFN_HALO == 0
    nfb = ff // tf
    hb = tm // FFN_HALO
    last_halo = t // FFN_HALO - 1
    return pl.pallas_call(
        functools.partial(_ffn_kernel, tm=tm, tiles_per_seq=seq // tm, final_norm=final_norm),
        grid=(t // tm, nfb),
        in_specs=[
            pl.BlockSpec((tm, d), lambda i, f: (i, 0)),
            pl.BlockSpec((FFN_HALO, d), lambda i, f: (jnp.maximum(i * hb - 1, 0), 0)),
            pl.BlockSpec((FFN_HALO, d), lambda i, f: (jnp.minimum((i + 1) * hb, last_halo), 0)),
            pl.BlockSpec((1, d), lambda i, f: (0, 0)),
            pl.BlockSpec((d, tf), lambda i, f: (0, f)),
            pl.BlockSpec((d, tf), lambda i, f: (0, nfb + f)),
            pl.BlockSpec((3, tf), lambda i, f: (0, f)),
            pl.BlockSpec((1, tf), lambda i, f: (0, f)),
            pl.BlockSpec((tf, d), lambda i, f: (f, 0)),
            pl.BlockSpec((1, d), lambda i, f: (0, 0)),
        ],
        out_specs=pl.BlockSpec((tm, d), lambda i, f: (i, 0)),
        out_shape=jax.ShapeDtypeStruct((t, d), F32),
        scratch_shapes=[pltpu.VMEM((tm + 2 * FFN_HALO, d), BF16)],
        compiler_params=_params("parallel", "arbitrary"),
        name="conv_ffn",
    )(h, h, h, gain.reshape(1, d), w_up, w_up, conv_w, conv_b.reshape(1, ff), w_down,
      final_gain.reshape(1, d))


def _rope_tables(seq):
    tpos = jnp.arange(seq, dtype=jnp.int32)
    row = (tpos // GRID_W).astype(F32)
    col = (tpos % GRID_W).astype(F32)
    inv_freq = ROPE_THETA ** (-(jnp.arange(0, ROPE_AXIS_DIM, 2, dtype=F32) / ROPE_AXIS_DIM))
    ang = jnp.concatenate([row[:, None] * inv_freq[None], col[:, None] * inv_freq[None]], axis=-1)
    cos, sin = jnp.cos(ang), jnp.sin(ang)
    cos_dup = jnp.repeat(cos, 2, axis=-1)
    sin_signed = jnp.stack([-sin, sin], axis=-1).reshape(seq, HEAD_DIM)
    return cos_dup, sin_signed


def kernel(x, ln_mix, ln_ffn, ln_final, t5_table, ev_w_in, ev_w_out, diff_lq1, diff_lk1, diff_lq2,
           diff_lk2, diff_subln, od_w_in, od_w_out, gqa_q_norm, gqa_k_norm, na_rpb, ffn_w_up,
           ffn_conv_w, ffn_conv_b, ffn_w_down):
    batch, seq, d = x.shape
    depth = ln_mix.shape[0]
    t = batch * seq
    h = x.reshape(t, d)

    for layer in range(depth):
        if layer % 2 == 0:
            e = layer // 2
            lambda_init = 0.8 - 0.6 * math.exp(-0.3 * layer)
            nat, *dil_projs = even_norm_proj(h, ln_mix[layer], ev_w_in[e].astype(BF16),
                                             batch=batch, seq=seq, tm=1024, tn=1024)
            oa = dilated_attention(nat, dil_projs, t5_table, batch=batch, seq=seq)
            ob = diff_attention(nat, t5_table[:, A_GROUPS * A_HEADS:], diff_lq1[e], diff_lk1[e],
                                diff_lq2[e], diff_lk2[e], diff_subln[e], chunk0=3 * A_HEADS,
                                batch=batch, seq=seq, lambda_init=lambda_init)
            h = out_proj(oa, ob, ev_w_out[e].astype(BF16), h, tm=512, tn=1024)
        else:
            o = layer // 2
            n_rope = C_HEADS + C_KV_HEADS
            head_gain = jnp.concatenate([jnp.tile(gqa_q_norm[o][None], (C_HEADS, 1)),
                                         jnp.tile(gqa_k_norm[o][None], (C_KV_HEADS, 1))], axis=0)
            cos_dup, sin_signed = _rope_tables(seq)
            proj = odd_norm_proj(h, ln_mix[layer], od_w_in[o].astype(BF16), head_gain.astype(F32),
                                 cos_dup, sin_signed, tm=1024, tn=1536)
            oc = gqa_attention(proj, batch=batch, seq=seq)
            od = na_attention(proj, na_rpb[o], chunk0=n_rope + C_KV_HEADS, batch=batch, seq=seq)
            h = out_proj(oc, od, od_w_out[o].astype(BF16), h, tm=512, tn=1024)
        h = conv_ffn_block(h, ln_ffn[layer], ffn_w_up[layer].astype(BF16), ffn_conv_w[layer],
                           ffn_conv_b[layer], ffn_w_down[layer].astype(BF16), ln_final,
                           seq=seq, tm=512, tf=512, final_norm=(layer == depth - 1))
    return h.reshape(batch, seq, d)
```

```python
import functools
import math

import numpy as np
import jax
import jax.numpy as jnp
from jax import lax
from jax.experimental import pallas as pl
from jax.experimental.pallas import tpu as pltpu

HEAD_DIM = 128
GRID_W = 64
RMS_EPS = 1e-6
NEG_INF = -1e30
T5_BUCKETS = 32
T5_MAX_DIST = 1024
DILATED_PATTERNS = ((128, 1), (512, 4), (2048, 16))
A_GROUPS = len(DILATED_PATTERNS)
A_HEADS = 8
B_HEADS = 4
B_QK_DIM = 128
B_V_DIM = 2 * B_QK_DIM
C_HEADS = 8
C_KV_HEADS = 2
ROPE_THETA = 10000.0
ROPE_AXIS_DIM = HEAD_DIM // 2
D_HEADS = 8
NA_ROWS = 8
NA_COLS = 16

LANE = 128
V7X_VMEM_LIMIT = 56 * 1024 * 1024

BF16 = jnp.bfloat16
F32 = jnp.float32

LOG2E = math.log2(math.e)
QUERY_SCALE = HEAD_DIM ** -0.5 * LOG2E


def _params(*sem):
    return pltpu.CompilerParams(dimension_semantics=sem, vmem_limit_bytes=V7X_VMEM_LIMIT)


def _rms(xf, gain):
    ms = jnp.mean(xf * xf, axis=-1, keepdims=True)
    return xf * lax.rsqrt(ms + RMS_EPS) * gain


def _dot_nt(a, b):
    return lax.dot_general(a, b, (((1,), (1,)), ((), ())), preferred_element_type=F32)


def _dot(a, b):
    return jnp.dot(a, b, preferred_element_type=F32)


def _lanes(x, width):
    return x if width == LANE else jnp.concatenate([x] * (width // LANE), axis=1)


def _t5_bucket(rel):
    nb = T5_BUCKETS // 2
    max_exact = nb // 2
    ret = jnp.where(rel > 0, nb, 0)
    n = jnp.abs(rel)
    n_f = jnp.maximum(n, 1).astype(F32)
    large = max_exact + (jnp.log(n_f / max_exact) / math.log(T5_MAX_DIST / max_exact)
                         * (nb - max_exact)).astype(jnp.int32)
    large = jnp.minimum(large, nb - 1)
    return ret + jnp.where(n < max_exact, n, large)


def _toeplitz(fv, rows, cols):
    n = rows + cols - 1
    assert fv.shape[-1] == n
    v = jnp.concatenate([fv[..., rows - 1:], fv[..., :rows - 1]], axis=-1)
    flat = jnp.tile(v, rows)[..., :rows * (n - 1)]
    return flat.reshape(fv.shape[:-1] + (rows, n - 1))[..., :cols]


def _store_chunks(acc, o_ref, n_chunks):
    for c in range(n_chunks):
        o_ref[c] = acc[:, c * LANE:(c + 1) * LANE].astype(o_ref.dtype)


def _even_proj_kernel(x_ref, g_ref, w_ref, nat_ref, *rest, plan, n_chunks, tm, qkv_period):
    dil_refs, (xn_ref, slab_ref) = rest[:-2], rest[-2:]
    j = pl.program_id(1)

    @pl.when(j == 0)
    def _():
        xn_ref[...] = _rms(x_ref[...], g_ref[...]).astype(BF16)

    acc = _dot(xn_ref[...], w_ref[...]) * jnp.where(j % qkv_period == 0, QUERY_SCALE, 1.0)

    for lo, hi, dilation, out_idx in plan:
        @pl.when((j >= lo) & (j < hi))
        def _(dilation=dilation, out_idx=out_idx):
            if dilation == 1:
                _store_chunks(acc, nat_ref, n_chunks)
                return
            o_ref = dil_refs[out_idx]
            per = tm // dilation
            for c in range(n_chunks):
                slab_ref[c] = acc[:, c * LANE:(c + 1) * LANE]
            for c in range(n_chunks):
                for r in range(dilation):
                    o_ref[c, 0, r] = slab_ref[c, pl.ds(r, per, stride=dilation), :].astype(o_ref.dtype)


def even_norm_proj(x, gain, w, *, batch, seq, tm, tn):
    t, d = x.shape
    n = w.shape[1]
    n_chunks = tn // LANE
    group_cols = 3 * A_HEADS * HEAD_DIM
    assert group_cols % tn == 0 and n % tn == 0 and seq % tm == 0 and t == batch * seq
    assert tn == A_HEADS * HEAD_DIM == B_HEADS * 2 * B_QK_DIM and B_QK_DIM == HEAD_DIM
    gt = group_cols // tn
    nj = n // tn
    tiles_per_seq = seq // tm
    dils = [dil for _, dil in DILATED_PATTERNS]
    assert dils[0] == 1 and all(tm % (16 * dil) == 0 for dil in dils)
    plan = [(0, gt, 1, None)]
    for gi in range(1, A_GROUPS):
        plan.append((gi * gt, (gi + 1) * gt, dils[gi], gi - 1))
    plan.append((A_GROUPS * gt, nj, 1, None))
    nat_tiles = gt + (nj - A_GROUPS * gt)

    def nat_map(i, j):
        return (jnp.where(j < gt, j, jnp.where(j < A_GROUPS * gt, gt - 1, j - (A_GROUPS - 1) * gt)), i, 0)

    def dil_map(gi):
        return lambda i, j: (jnp.clip(j - gi * gt, 0, gt - 1), i // tiles_per_seq, 0, i % tiles_per_seq, 0)

    out_specs = [pl.BlockSpec((n_chunks, tm, LANE), nat_map)]
    out_shape = [jax.ShapeDtypeStruct((nat_tiles * n_chunks, t, LANE), BF16)]
    for gi in range(1, A_GROUPS):
        dil = dils[gi]
        out_specs.append(pl.BlockSpec((n_chunks, 1, dil, tm // dil, LANE), dil_map(gi)))
        out_shape.append(jax.ShapeDtypeStruct((gt * n_chunks, batch, dil, seq // dil, LANE), BF16))
    return pl.pallas_call(
        functools.partial(_even_proj_kernel, plan=tuple(plan), n_chunks=n_chunks, tm=tm, qkv_period=gt),
        grid=(t // tm, nj),
        in_specs=[
            pl.BlockSpec((tm, d), lambda i, j: (i, 0)),
            pl.BlockSpec((1, d), lambda i, j: (0, 0)),
            pl.BlockSpec((d, tn), lambda i, j: (0, j)),
        ],
        out_specs=out_specs,
        out_shape=out_shape,
        scratch_shapes=[pltpu.VMEM((tm, d), BF16), pltpu.VMEM((n_chunks, tm, LANE), F32)],
        compiler_params=_params("parallel", "arbitrary"),
        name="even_norm_proj",
    )(x, gain.reshape(1, d), w)


def _rope_chunk(y, hg, cosd, sins, even_lane):
    y = _rms(y, hg)
    partner = jnp.where(even_lane, pltpu.roll(y, LANE - 1, axis=1), pltpu.roll(y, 1, axis=1))
    return y * cosd + partner * sins


def _odd_proj_kernel(x_ref, g_ref, w_ref, hg_ref, cos_ref, sin_ref, o_ref, xn_ref, *,
                     n_chunks, n_rope, n_tiles, query_chunks):
    j = pl.program_id(1)

    @pl.when(j == 0)
    def _():
        xn_ref[...] = _rms(x_ref[...], g_ref[...]).astype(BF16)

    acc = _dot(xn_ref[...], w_ref[...])

    for jt in range(n_tiles):
        @pl.when(j == jt)
        def _(jt=jt):
            if jt * n_chunks < n_rope:
                cosd = cos_ref[...]
                sins = sin_ref[...]
                even_lane = (lax.broadcasted_iota(jnp.int32, cosd.shape, 1) % 2) == 0
            for c in range(n_chunks):
                gc = jt * n_chunks + c
                y = acc[:, c * LANE:(c + 1) * LANE]
                if gc < n_rope:
                    y = _rope_chunk(y, hg_ref[gc:gc + 1, :], cosd, sins, even_lane)
                if gc in query_chunks:
                    y = y * QUERY_SCALE
                o_ref[c] = y.astype(o_ref.dtype)


def odd_norm_proj(x, gain, w, head_gain, cos_dup, sin_signed, *, query_chunks, tm, tn):
    t, d = x.shape
    n = w.shape[1]
    assert t % tm == 0 and n % tn == 0 and tn % LANE == 0
    n_chunks = tn // LANE
    n_rope = head_gain.shape[0]
    s = cos_dup.shape[0]
    assert s % tm == 0
    sb = s // tm
    return pl.pallas_call(
        functools.partial(_odd_proj_kernel, n_chunks=n_chunks, n_rope=n_rope, n_tiles=n // tn,
                          query_chunks=frozenset(query_chunks)),
        grid=(t // tm, n // tn),
        in_specs=[
            pl.BlockSpec((tm, d), lambda i, j: (i, 0)),
            pl.BlockSpec((1, d), lambda i, j: (0, 0)),
            pl.BlockSpec((d, tn), lambda i, j: (0, j)),
            pl.BlockSpec((n_rope, LANE), lambda i, j: (0, 0)),
            pl.BlockSpec((tm, LANE), lambda i, j: (i % sb, 0)),
            pl.BlockSpec((tm, LANE), lambda i, j: (i % sb, 0)),
        ],
        out_specs=pl.BlockSpec((n_chunks, tm, LANE), lambda i, j: (j, i, 0)),
        out_shape=jax.ShapeDtypeStruct((n // LANE, t, LANE), BF16),
        scratch_shapes=[pltpu.VMEM((tm, d), BF16)],
        compiler_params=_params("parallel", "arbitrary"),
        name="odd_norm_proj",
    )(x, gain.reshape(1, d), w, head_gain, cos_dup, sin_signed)


A_TQ = 256


def _a_geometry(seq, window, dilation):
    sub = seq // dilation
    half = window // (2 * dilation)
    tq = min(A_TQ, sub)
    win = min(tq + 2 * half, sub)
    assert seq % dilation == 0 and sub % tq == 0 and half % 16 == 0 and win % LANE == 0
    assert win == sub or win == tq + 2 * half
    return sub, half, tq, win


def _dilated_kernel(*refs, geoms):
    ng = len(geoms)
    qkv = refs[:3 * ng]
    bms = refs[3 * ng:4 * ng]
    o_ref = refs[4 * ng]
    m_s, l_s, acc_s = refs[4 * ng + 1:]

    order = sorted(range(ng), key=lambda g: -geoms[g][0])
    for pos, g in enumerate(order):
        dilation, sub, half, tq, win = geoms[g]
        q_ref, k_ref, v_ref = qkv[3 * g:3 * g + 3]
        bm_ref = bms[g]
        nq = sub // tq
        first = pos == 0

        def block(it, carry, dilation=dilation, sub=sub, half=half, tq=tq, win=win, nq=nq,
                  q_ref=q_ref, k_ref=k_ref, v_ref=v_ref, bm_ref=bm_ref, first=first):
            r = it // nq
            qi = it % nq
            q0 = pl.multiple_of(qi * tq, tq)
            kstart = pl.multiple_of(jnp.clip(q0 - half, 0, sub - win), half)
            if dilation == 1:
                q = q_ref[0, pl.ds(q0, tq), :]
                kw = k_ref[0, pl.ds(kstart, win), :]
                vw = v_ref[0, pl.ds(kstart, win), :]
                rows = pl.ds(q0, tq)
            else:
                q = q_ref[0, 0, r, pl.ds(q0, tq), :]
                kw = k_ref[0, 0, r, pl.ds(kstart, win), :]
                vw = v_ref[0, 0, r, pl.ds(kstart, win), :]
                rows = pl.ds(q0 * dilation + r, tq, stride=dilation)
            variant = 0 if nq == 1 else jnp.where(qi == 0, 0, jnp.where(qi == nq - 1, 2, 1))
            s = _dot_nt(q, kw) + bm_ref[variant, 0]
            mb = jnp.max(s, axis=-1, keepdims=True)
            if first:
                p = jnp.exp2(s - mb)
                m_s[rows, :] = jnp.broadcast_to(mb, (tq, LANE))
                l_s[rows, :] = jnp.broadcast_to(jnp.sum(p, axis=-1, keepdims=True), (tq, LANE))
                acc_s[rows, :] = _dot(p.astype(BF16), vw)
            else:
                m_old = m_s[rows, :]
                m_new = jnp.maximum(m_old, mb)
                alpha = jnp.exp2(m_old - m_new)
                p = jnp.exp2(s - _lanes(m_new, win))
                m_s[rows, :] = m_new
                l_s[rows, :] = alpha * l_s[rows, :] + jnp.sum(p, axis=-1, keepdims=True)
                acc_s[rows, :] = alpha * acc_s[rows, :] + _dot(p.astype(BF16), vw)
            return carry

        lax.fori_loop(0, dilation * nq, block, 0)

    o_ref[...] = (acc_s[...] / l_s[...]).astype(o_ref.dtype)


def _dilated_bias_mask(t5_cols, dilation, half, tq, win, sub):
    nq = sub // tq
    offs = [0] if nq == 1 else [0, -half, tq - win]
    tabs = []
    for off in offs:
        x = np.arange(tq + win - 1) - (tq - 1) + off
        fv = jnp.where(jnp.asarray(np.abs(x) <= half)[None],
                       LOG2E * t5_cols[_t5_bucket(jnp.asarray(dilation * x, jnp.int32))].T.astype(F32), NEG_INF)
        tabs.append(_toeplitz(fv, tq, win))
    return jnp.stack(tabs)


def dilated_attention(nat, dil_projs, t5_table, *, batch, seq):
    t = batch * seq
    geoms, args, in_specs, bms = [], [], [], []
    for gi, (window, dilation) in enumerate(DILATED_PATTERNS):
        sub, half, tq, win = _a_geometry(seq, window, dilation)
        geoms.append((dilation, sub, half, tq, win))
        for part in range(3):
            if dilation == 1:
                args.append(nat)
                in_specs.append(pl.BlockSpec(
                    (1, seq, LANE), lambda b, h, part=part: (part * A_HEADS + h, b, 0)))
            else:
                args.append(dil_projs[gi - 1])
                in_specs.append(pl.BlockSpec(
                    (1, 1, dilation, sub, LANE),
                    lambda b, h, part=part: (part * A_HEADS + h, b, 0, 0, 0)))
        bms.append(_dilated_bias_mask(t5_table[:, gi * A_HEADS:(gi + 1) * A_HEADS],
                                      dilation, half, tq, win, sub))
    for bm in bms:
        args.append(bm)
        in_specs.append(pl.BlockSpec((bm.shape[0], 1) + bm.shape[2:], lambda b, h: (0, h, 0, 0)))
    return pl.pallas_call(
        functools.partial(_dilated_kernel, geoms=tuple(geoms)),
        grid=(batch, A_HEADS),
        in_specs=in_specs,
        out_specs=pl.BlockSpec((seq, LANE), lambda b, h: (b, h)),
        out_shape=jax.ShapeDtypeStruct((t, A_HEADS * LANE), BF16),
        scratch_shapes=[pltpu.VMEM((seq, LANE), F32)] * 3,
        compiler_params=_params("parallel", "arbitrary"),
        name="dilated_attn",
    )(*args)


KEY_SPLITS = 4


def _partial_softmax(s, v):
    m = jnp.max(s, axis=-1, keepdims=True)
    p = jnp.exp2(s - m)
    return m, jnp.sum(p, axis=-1, keepdims=True), _dot(p.astype(BF16), v)


def _merge_partials(parts):
    m = parts[0][0]
    for pm, _, _ in parts[1:]:
        m = jnp.maximum(m, pm)
    weights = [jnp.exp2(pm - m) for pm, _, _ in parts]
    denom = sum(w * pl_ for w, (_, pl_, _) in zip(weights, parts))
    acc = sum(w * pa for w, (_, _, pa) in zip(weights, parts))
    return acc / denom


T5_BAND = T5_MAX_DIST


def _diff_band(tq):
    lo = -((T5_BAND + LANE - 1) // LANE) - 1
    hi = (T5_BAND + tq - 1 + LANE - 1) // LANE
    return lo, hi


def _diff_kernel(q_ref, k_ref, v_ref, tz_ref, lq1_ref, lk1_ref, lq2_ref, lk2_ref, sub_ref, o_ref, *,
                 tq, seq, lambda_init):
    qi = pl.program_id(2)
    lo, hi = _diff_band(tq)
    q_chunk0 = qi * (tq // LANE)
    span = seq // KEY_SPLITS

    def attend(mi):
        parts = []
        for part in range(KEY_SPLITS):
            k0 = part * span
            bias = jnp.concatenate(
                [tz_ref[0, jnp.clip(cc - q_chunk0, lo, hi) - lo]
                 for cc in range(k0 // LANE, (k0 + span) // LANE)], axis=1)
            s = _dot_nt(q_ref[mi], k_ref[mi, k0:k0 + span, :]) + bias
            vc = jnp.concatenate([v_ref[0, k0:k0 + span, :], v_ref[1, k0:k0 + span, :]], axis=1)
            parts.append(_partial_softmax(s, vc))
        return _merge_partials(parts)

    lam = (jnp.exp(jnp.sum(lq1_ref[...] * lk1_ref[...], axis=-1, keepdims=True))
           - jnp.exp(jnp.sum(lq2_ref[...] * lk2_ref[...], axis=-1, keepdims=True)) + lambda_init)
    o = attend(0) - lam * attend(1)
    o_ref[...] = (_rms(o, sub_ref[...]) * (1.0 - lambda_init)).astype(o_ref.dtype)


def _diff_toeplitz(t5_cols, tq):
    lo, hi = _diff_band(tq)
    n_off = hi - lo + 1
    x = np.arange(tq + n_off * LANE - 1) - (tq - 1) + LANE * lo
    fv = LOG2E * t5_cols[_t5_bucket(jnp.asarray(x, jnp.int32))].T.astype(F32)
    tz = _toeplitz(fv, tq, n_off * LANE)
    return jnp.transpose(tz.reshape(-1, tq, n_off, LANE), (0, 2, 1, 3))


def diff_attention(proj, t5_cols, lq1, lk1, lq2, lk2, subln, *, chunk0, batch, seq, lambda_init):
    nc, t, _ = proj.shape
    tq = 256
    assert seq % tq == 0 and seq % (KEY_SPLITS * LANE) == 0 and chunk0 % 2 == 0
    nq = seq // tq
    tz = _diff_toeplitz(t5_cols, tq)
    n_off = tz.shape[1]
    qb, kb, vb = chunk0 // 2, chunk0 // 2 + B_HEADS, chunk0 // 2 + 2 * B_HEADS
    vec = lambda a: a.reshape(1, -1).astype(F32)
    vspec = lambda n: pl.BlockSpec((1, n), lambda b, h, qi: (0, 0))
    return pl.pallas_call(
        functools.partial(_diff_kernel, tq=tq, seq=seq, lambda_init=lambda_init),
        grid=(batch, B_HEADS, nq),
        in_specs=[
            pl.BlockSpec((2, tq, LANE), lambda b, h, qi: (qb + h, b * nq + qi, 0)),
            pl.BlockSpec((2, seq, LANE), lambda b, h, qi: (kb + h, b, 0)),
            pl.BlockSpec((2, seq, LANE), lambda b, h, qi: (vb + h, b, 0)),
            pl.BlockSpec((1, n_off, tq, LANE), lambda b, h, qi: (h, 0, 0, 0)),
            vspec(B_QK_DIM), vspec(B_QK_DIM), vspec(B_QK_DIM), vspec(B_QK_DIM), vspec(B_V_DIM),
        ],
        out_specs=pl.BlockSpec((tq, B_V_DIM), lambda b, h, qi: (b * nq + qi, h)),
        out_shape=jax.ShapeDtypeStruct((t, B_HEADS * B_V_DIM), BF16),
        compiler_params=_params("parallel", "parallel", "arbitrary"),
        name="diff_attn",
    )(proj, proj, proj, tz, vec(lq1), vec(lk1), vec(lq2), vec(lk2), vec(subln))


def _gqa_kernel(q_ref, k_ref, v_ref, o_ref, *, tq, seq, group):
    span = seq // KEY_SPLITS
    q = q_ref[...].reshape(group * tq, HEAD_DIM)
    parts = []
    for part in range(KEY_SPLITS):
        k0 = part * span
        parts.append(_partial_softmax(_dot_nt(q, k_ref[0, k0:k0 + span, :]), v_ref[0, k0:k0 + span, :]))
    o = _merge_partials(parts)
    for g in range(group):
        o_ref[:, g * HEAD_DIM:(g + 1) * HEAD_DIM] = o[g * tq:(g + 1) * tq].astype(o_ref.dtype)


def gqa_attention(proj, *, batch, seq):
    nc, t, _ = proj.shape
    tq = 256
    group = C_HEADS // C_KV_HEADS
    nq = seq // tq
    return pl.pallas_call(
        functools.partial(_gqa_kernel, tq=tq, seq=seq, group=group),
        grid=(batch, C_KV_HEADS, nq),
        in_specs=[
            pl.BlockSpec((group, tq, LANE), lambda b, n, qi: (n, b * nq + qi, 0)),
            pl.BlockSpec((1, seq, LANE), lambda b, n, qi: (C_HEADS + n, b, 0)),
            pl.BlockSpec((1, seq, LANE), lambda b, n, qi: (C_HEADS + C_KV_HEADS + n, b, 0)),
        ],
        out_specs=pl.BlockSpec((tq, group * HEAD_DIM), lambda b, n, qi: (b * nq + qi, n)),
        out_shape=jax.ShapeDtypeStruct((t, C_HEADS * HEAD_DIM), BF16),
        compiler_params=_params("parallel", "parallel", "arbitrary"),
        name="gqa_attn",
    )(proj, proj, proj)


NA_QROWS = 4
NA_KROWS = NA_QROWS + NA_ROWS


def _na_kernel(q_ref, k_ref, v_ref, bm_ref, o_ref, *, rows):
    blk = pl.program_id(2)
    wrow = jnp.clip(blk * NA_QROWS - NA_ROWS // 2, 0, rows - NA_KROWS)
    kstart = pl.multiple_of(wrow * GRID_W, NA_QROWS * GRID_W)
    nkeys = NA_KROWS * GRID_W
    kw = k_ref[0, pl.ds(kstart, nkeys), :]
    vw = v_ref[0, pl.ds(kstart, nkeys), :]
    s = _dot_nt(q_ref[0], kw) + bm_ref[0, 0]
    m = jnp.max(s, axis=-1, keepdims=True)
    p = jnp.exp2(s - m)
    denom = jnp.sum(p, axis=-1, keepdims=True)
    o_ref[...] = (_dot(p.astype(BF16), vw) / denom).astype(o_ref.dtype)


def _na_bias_mask(rpb, rows):
    kr = min(NA_ROWS, rows)
    nblk = rows // NA_QROWS
    nheads = rpb.shape[0]
    pad = GRID_W - NA_COLS
    col_blocks = _toeplitz(jnp.pad(LOG2E * rpb.astype(F32), ((0, 0), (0, 0), (pad, pad))), GRID_W, GRID_W)
    qj = np.arange(GRID_W)[:, None]
    kj = np.arange(GRID_W)[None, :]
    cs = np.clip(qj - NA_COLS // 2, 0, GRID_W - NA_COLS)
    col_ok = jnp.asarray((kj >= cs) & (kj < cs + NA_COLS))
    col_blocks = jnp.where(col_ok[None, None], col_blocks, NEG_INF)
    masked = jnp.full((nheads, GRID_W, GRID_W), NEG_INF, F32)
    tabs = []
    for blk in (0, 1, nblk - 1):
        wrow = int(np.clip(blk * NA_QROWS - NA_ROWS // 2, 0, rows - NA_KROWS))
        row_strips = []
        for a in range(NA_QROWS):
            qi = blk * NA_QROWS + a
            rs = int(np.clip(qi - kr // 2, 0, rows - kr))
            strip = []
            for c in range(NA_KROWS):
                ki = wrow + c
                strip.append(col_blocks[:, ki - qi + NA_ROWS - 1] if rs <= ki < rs + kr else masked)
            row_strips.append(jnp.concatenate(strip, axis=-1))
        tabs.append(jnp.concatenate(row_strips, axis=-2))
    return jnp.stack(tabs)


def na_attention(proj, rpb, *, chunk0, batch, seq):
    nc, t, _ = proj.shape
    rows = seq // GRID_W
    assert rows % NA_QROWS == 0 and rows >= NA_KROWS and rows // NA_QROWS >= 3
    nblk = rows // NA_QROWS
    tq = NA_QROWS * GRID_W
    nkeys = NA_KROWS * GRID_W
    bm = _na_bias_mask(rpb, rows)

    def variant(blk):
        return jnp.where(blk == 0, 0, jnp.where(blk == nblk - 1, 2, 1))

    return pl.pallas_call(
        functools.partial(_na_kernel, rows=rows),
        grid=(batch, D_HEADS, nblk),
        in_specs=[
            pl.BlockSpec((1, tq, LANE), lambda b, h, blk: (chunk0 + h, b * nblk + blk, 0)),
            pl.BlockSpec((1, seq, LANE), lambda b, h, blk: (chunk0 + D_HEADS + h, b, 0)),
            pl.BlockSpec((1, seq, LANE), lambda b, h, blk: (chunk0 + 2 * D_HEADS + h, b, 0)),
            pl.BlockSpec((1, 1, tq, nkeys), lambda b, h, blk: (variant(blk), h, 0, 0)),
        ],
        out_specs=pl.BlockSpec((tq, LANE), lambda b, h, blk: (b * nblk + blk, h)),
        out_shape=jax.ShapeDtypeStruct((t, D_HEADS * HEAD_DIM), BF16),
        compiler_params=_params("parallel", "parallel", "arbitrary"),
        name="na_attn",
    )(proj, proj, proj, bm)


def _out_proj_kernel(a_ref, b_ref, w_ref, h_ref, out_ref, *, na):
    out_ref[...] = h_ref[...] + _dot(a_ref[...], w_ref[:na, :]) + _dot(b_ref[...], w_ref[na:, :])


def out_proj(a, b, w, h, *, tm, tn):
    t, d = h.shape
    na, nb = a.shape[1], b.shape[1]
    return pl.pallas_call(
        functools.partial(_out_proj_kernel, na=na),
        grid=(t // tm, d // tn),
        in_specs=[pl.BlockSpec((tm, na), lambda i, j: (i, 0)),
                  pl.BlockSpec((tm, nb), lambda i, j: (i, 0)),
                  pl.BlockSpec((na + nb, tn), lambda i, j: (0, j)),
                  pl.BlockSpec((tm, tn), lambda i, j: (i, j))],
        out_specs=pl.BlockSpec((tm, tn), lambda i, j: (i, j)),
        out_shape=jax.ShapeDtypeStruct((t, d), F32),
        compiler_params=_params("parallel", "arbitrary"),
        name="out_proj",
    )(a, b, w, h)


FFN_HALO = 16


def _gelu_tanh(x):
    return 0.5 * x * (1.0 + jnp.tanh(math.sqrt(2.0 / math.pi) * (x + 0.044715 * (x * x * x))))


def _ffn_kernel(x_ref, xp_ref, xnx_ref, g_ref, wg_ref, wu_ref, cw_ref, cb_ref, wd_ref, gf_ref, out_ref,
                xn_ref, *, tm, tiles_per_seq, final_norm):
    i = pl.program_id(0)
    f = pl.program_id(1)
    nf = pl.num_programs(1)
    hl = FFN_HALO

    @pl.when(f == 0)
    def _():
        gain = g_ref[...]
        keep_prev = jnp.where(i % tiles_per_seq != 0, 1.0, 0.0)
        keep_next = jnp.where(i % tiles_per_seq != tiles_per_seq - 1, 1.0, 0.0)
        xn_ref[:hl, :] = (_rms(xp_ref[...], gain) * keep_prev).astype(BF16)
        xn_ref[hl:hl + tm, :] = _rms(x_ref[...], gain).astype(BF16)
        xn_ref[hl + tm:, :] = (_rms(xnx_ref[...], gain) * keep_next).astype(BF16)
        out_ref[...] = x_ref[...]

    ge = _dot(xn_ref[...], wg_ref[...])
    cw = cw_ref[...]
    g = (cw[0:1, :] * ge[hl - 1:hl - 1 + tm] + cw[1:2, :] * ge[hl:hl + tm]
         + cw[2:3, :] * ge[hl + 1:hl + 1 + tm] + cb_ref[...])
    u = _dot(xn_ref[hl:hl + tm, :], wu_ref[...])
    act = (_gelu_tanh(g) * u).astype(BF16)
    out_ref[...] += _dot(act, wd_ref[...])

    if final_norm:
        @pl.when(f == nf - 1)
        def _():
            out_ref[...] = _rms(out_ref[...], gf_ref[...])


def conv_ffn_block(h, gain, w_up, conv_w, conv_b, w_down, final_gain, *, seq, tm, tf, final_norm):
    t, d = h.shape
    ff = w_down.shape[0]
    assert t % tm == 0 and seq % tm == 0 and ff % tf == 0 and tm % FFN_HALO == 0
    nfb = ff // tf
    hb = tm // FFN_HALO
    last_halo = t // FFN_HALO - 1
    return pl.pallas_call(
        functools.partial(_ffn_kernel, tm=tm, tiles_per_seq=seq // tm, final_norm=final_norm),
        grid=(t // tm, nfb),
        in_specs=[
            pl.BlockSpec((tm, d), lambda i, f: (i, 0)),
            pl.BlockSpec((FFN_HALO, d), lambda i, f: (jnp.maximum(i * hb - 1, 0), 0)),
            pl.BlockSpec((FFN_HALO, d), lambda i, f: (jnp.minimum((i + 1) * hb, last_halo), 0)),
            pl.BlockSpec((1, d), lambda i, f: (0, 0)),
            pl.BlockSpec((d, tf), lambda i, f: (0, f)),
            pl.BlockSpec((d, tf), lambda i, f: (0, nfb + f)),
            pl.BlockSpec((3, tf), lambda i, f: (0, f)),
            pl.BlockSpec((1, tf), lambda i, f: (0, f)),
            pl.BlockSpec((tf, d), lambda i, f: (f, 0)),
            pl.BlockSpec((1, d), lambda i, f: (0, 0)),
        ],
        out_specs=pl.BlockSpec((tm, d), lambda i, f: (i, 0)),
        out_shape=jax.ShapeDtypeStruct((t, d), F32),
        scratch_shapes=[pltpu.VMEM((tm + 2 * FFN_HALO, d), BF16)],
        compiler_params=_params("parallel", "arbitrary"),
        name="conv_ffn",
    )(h, h, h, gain.reshape(1, d), w_up, w_up, conv_w, conv_b.reshape(1, ff), w_down,
      final_gain.reshape(1, d))


def _rope_tables(seq):
    tpos = jnp.arange(seq, dtype=jnp.int32)
    row = (tpos // GRID_W).astype(F32)
    col = (tpos % GRID_W).astype(F32)
    inv_freq = ROPE_THETA ** (-(jnp.arange(0, ROPE_AXIS_DIM, 2, dtype=F32) / ROPE_AXIS_DIM))
    ang = jnp.concatenate([row[:, None] * inv_freq[None], col[:, None] * inv_freq[None]], axis=-1)
    cos, sin = jnp.cos(ang), jnp.sin(ang)
    cos_dup = jnp.repeat(cos, 2, axis=-1)
    sin_signed = jnp.stack([-sin, sin], axis=-1).reshape(seq, HEAD_DIM)
    return cos_dup, sin_signed


def kernel(x, ln_mix, ln_ffn, ln_final, t5_table, ev_w_in, ev_w_out, diff_lq1, diff_lk1, diff_lq2,
           diff_lk2, diff_subln, od_w_in, od_w_out, gqa_q_norm, gqa_k_norm, na_rpb, ffn_w_up,
           ffn_conv_w, ffn_conv_b, ffn_w_down):
    batch, seq, d = x.shape
    depth = ln_mix.shape[0]
    t = batch * seq
    h = x.reshape(t, d)

    for layer in range(depth):
        if layer % 2 == 0:
            e = layer // 2
            lambda_init = 0.8 - 0.6 * math.exp(-0.3 * layer)
            nat, *dil_projs = even_norm_proj(h, ln_mix[layer], ev_w_in[e].astype(BF16),
                                             batch=batch, seq=seq, tm=1024, tn=1024)
            oa = dilated_attention(nat, dil_projs, t5_table, batch=batch, seq=seq)
            ob = diff_attention(nat, t5_table[:, A_GROUPS * A_HEADS:], diff_lq1[e], diff_lk1[e],
                                diff_lq2[e], diff_lk2[e], diff_subln[e], chunk0=3 * A_HEADS,
                                batch=batch, seq=seq, lambda_init=lambda_init)
            h = out_proj(oa, ob, ev_w_out[e].astype(BF16), h, tm=512, tn=1024)
        else:
            o = layer // 2
            n_rope = C_HEADS + C_KV_HEADS
            head_gain = jnp.concatenate([jnp.tile(gqa_q_norm[o][None], (C_HEADS, 1)),
                                         jnp.tile(gqa_k_norm[o][None], (C_KV_HEADS, 1))], axis=0)
            cos_dup, sin_signed = _rope_tables(seq)
            na_chunk0 = n_rope + C_KV_HEADS
            query_chunks = list(range(C_HEADS)) + list(range(na_chunk0, na_chunk0 + D_HEADS))
            proj = odd_norm_proj(h, ln_mix[layer], od_w_in[o].astype(BF16), head_gain.astype(F32),
                                 cos_dup, sin_signed, query_chunks=query_chunks, tm=1024, tn=1536)
            oc = gqa_attention(proj, batch=batch, seq=seq)
            od = na_attention(proj, na_rpb[o], chunk0=na_chunk0, batch=batch, seq=seq)
            h = out_proj(oc, od, od_w_out[o].astype(BF16), h, tm=512, tn=1024)
        h = conv_ffn_block(h, ln_ffn[layer], ffn_w_up[layer].astype(BF16), ffn_conv_w[layer],
                           ffn_conv_b[layer], ffn_w_down[layer].astype(BF16), ln_final,
                           seq=seq, tm=512, tf=512, final_norm=(layer == depth - 1))
    return h.reshape(batch, seq, d)
```

```python
import functools
import math

import numpy as np
import jax
import jax.numpy as jnp
from jax import lax
from jax.experimental import pallas as pl
from jax.experimental.pallas import tpu as pltpu

HEAD_DIM = 128
GRID_W = 64
RMS_EPS = 1e-6
NEG_INF = -1e30
T5_BUCKETS = 32
T5_MAX_DIST = 1024
DILATED_PATTERNS = ((128, 1), (512, 4), (2048, 16))
A_GROUPS = len(DILATED_PATTERNS)
A_HEADS = 8
B_HEADS = 4
B_QK_DIM = 128
B_V_DIM = 2 * B_QK_DIM
C_HEADS = 8
C_KV_HEADS = 2
ROPE_THETA = 10000.0
ROPE_AXIS_DIM = HEAD_DIM // 2
D_HEADS = 8
NA_ROWS = 8
NA_COLS = 16

LANE = 128
V7X_VMEM_LIMIT = 56 * 1024 * 1024

BF16 = jnp.bfloat16
F32 = jnp.float32

LOG2E = math.log2(math.e)
QUERY_SCALE = HEAD_DIM ** -0.5 * LOG2E


def _params(*sem):
    return pltpu.CompilerParams(dimension_semantics=sem, vmem_limit_bytes=V7X_VMEM_LIMIT)


def _rms(xf, gain):
    ms = jnp.mean(xf * xf, axis=-1, keepdims=True)
    return xf * lax.rsqrt(ms + RMS_EPS) * gain


def _dot_nt(a, b):
    return lax.dot_general(a, b, (((1,), (1,)), ((), ())), preferred_element_type=F32)


def _dot(a, b):
    return jnp.dot(a, b, preferred_element_type=F32)


def _lanes(x, width):
    return x if width == LANE else jnp.concatenate([x] * (width // LANE), axis=1)


def _t5_bucket(rel):
    nb = T5_BUCKETS // 2
    max_exact = nb // 2
    ret = jnp.where(rel > 0, nb, 0)
    n = jnp.abs(rel)
    n_f = jnp.maximum(n, 1).astype(F32)
    large = max_exact + (jnp.log(n_f / max_exact) / math.log(T5_MAX_DIST / max_exact)
                         * (nb - max_exact)).astype(jnp.int32)
    large = jnp.minimum(large, nb - 1)
    return ret + jnp.where(n < max_exact, n, large)


def _toeplitz(fv, rows, cols):
    n = rows + cols - 1
    assert fv.shape[-1] == n
    v = jnp.concatenate([fv[..., rows - 1:], fv[..., :rows - 1]], axis=-1)
    flat = jnp.tile(v, rows)[..., :rows * (n - 1)]
    return flat.reshape(fv.shape[:-1] + (rows, n - 1))[..., :cols]


def _store_chunks(acc, o_ref, n_chunks):
    for c in range(n_chunks):
        o_ref[c] = acc[:, c * LANE:(c + 1) * LANE].astype(o_ref.dtype)


def _even_proj_kernel(x_ref, g_ref, w_ref, nat_ref, *rest, plan, n_chunks, tm, qkv_period):
    dil_refs, (xn_ref, slab_ref) = rest[:-2], rest[-2:]
    j = pl.program_id(1)

    @pl.when(j == 0)
    def _():
        xn_ref[...] = _rms(x_ref[...], g_ref[...]).astype(BF16)

    factor = jnp.where(j % qkv_period == 0, QUERY_SCALE, 1.0)

    for lo, hi, dilation, out_idx in plan:
        @pl.when((j >= lo) & (j < hi))
        def _(dilation=dilation, out_idx=out_idx):
            acc = _dot(xn_ref[...], w_ref[...]) * factor
            if dilation == 1:
                _store_chunks(acc, nat_ref, n_chunks)
                return
            o_ref = dil_refs[out_idx]
            per = tm // dilation
            for c in range(n_chunks):
                slab_ref[c] = acc[:, c * LANE:(c + 1) * LANE]
            for c in range(n_chunks):
                for r in range(dilation):
                    o_ref[c, 0, r] = slab_ref[c, pl.ds(r, per, stride=dilation), :].astype(o_ref.dtype)


def even_norm_proj(x, gain, w, *, batch, seq, tm, tn):
    t, d = x.shape
    n = w.shape[1]
    n_chunks = tn // LANE
    group_cols = 3 * A_HEADS * HEAD_DIM
    assert group_cols % tn == 0 and n % tn == 0 and seq % tm == 0 and t == batch * seq
    assert tn == A_HEADS * HEAD_DIM == B_HEADS * 2 * B_QK_DIM and B_QK_DIM == HEAD_DIM
    gt = group_cols // tn
    nj = n // tn
    tiles_per_seq = seq // tm
    dils = [dil for _, dil in DILATED_PATTERNS]
    assert dils[0] == 1 and all(tm % (16 * dil) == 0 for dil in dils)
    plan = [(0, gt, 1, None)]
    for gi in range(1, A_GROUPS):
        plan.append((gi * gt, (gi + 1) * gt, dils[gi], gi - 1))
    plan.append((A_GROUPS * gt, nj, 1, None))
    nat_tiles = gt + (nj - A_GROUPS * gt)

    def nat_map(i, j):
        return (jnp.where(j < gt, j, jnp.where(j < A_GROUPS * gt, gt - 1, j - (A_GROUPS - 1) * gt)), i, 0)

    def dil_map(gi):
        return lambda i, j: (jnp.clip(j - gi * gt, 0, gt - 1), i // tiles_per_seq, 0, i % tiles_per_seq, 0)

    out_specs = [pl.BlockSpec((n_chunks, tm, LANE), nat_map)]
    out_shape = [jax.ShapeDtypeStruct((nat_tiles * n_chunks, t, LANE), BF16)]
    for gi in range(1, A_GROUPS):
        dil = dils[gi]
        out_specs.append(pl.BlockSpec((n_chunks, 1, dil, tm // dil, LANE), dil_map(gi)))
        out_shape.append(jax.ShapeDtypeStruct((gt * n_chunks, batch, dil, seq // dil, LANE), BF16))
    return pl.pallas_call(
        functools.partial(_even_proj_kernel, plan=tuple(plan), n_chunks=n_chunks, tm=tm, qkv_period=gt),
        grid=(t // tm, nj),
        in_specs=[
            pl.BlockSpec((tm, d), lambda i, j: (i, 0)),
            pl.BlockSpec((1, d), lambda i, j: (0, 0)),
            pl.BlockSpec((d, tn), lambda i, j: (0, j)),
        ],
        out_specs=out_specs,
        out_shape=out_shape,
        scratch_shapes=[pltpu.VMEM((tm, d), BF16), pltpu.VMEM((n_chunks, tm, LANE), F32)],
        compiler_params=_params("parallel", "arbitrary"),
        name="even_norm_proj",
    )(x, gain.reshape(1, d), w)


def _rope_chunk(y, hg, cosd, sins, even_lane):
    y = _rms(y, hg)
    partner = jnp.where(even_lane, pltpu.roll(y, LANE - 1, axis=1), pltpu.roll(y, 1, axis=1))
    return y * cosd + partner * sins


def _odd_proj_kernel(x_ref, g_ref, w_ref, hg_ref, cos_ref, sin_ref, o_ref, xn_ref, *,
                     n_chunks, n_rope, n_tiles, query_chunks):
    j = pl.program_id(1)

    @pl.when(j == 0)
    def _():
        xn_ref[...] = _rms(x_ref[...], g_ref[...]).astype(BF16)

    acc = _dot(xn_ref[...], w_ref[...])

    for jt in range(n_tiles):
        @pl.when(j == jt)
        def _(jt=jt):
            if jt * n_chunks < n_rope:
                cosd = cos_ref[...]
                sins = sin_ref[...]
                even_lane = (lax.broadcasted_iota(jnp.int32, cosd.shape, 1) % 2) == 0
            for c in range(n_chunks):
                gc = jt * n_chunks + c
                y = acc[:, c * LANE:(c + 1) * LANE]
                if gc < n_rope:
                    y = _rope_chunk(y, hg_ref[gc:gc + 1, :], cosd, sins, even_lane)
                if gc in query_chunks:
                    y = y * QUERY_SCALE
                o_ref[c] = y.astype(o_ref.dtype)


def odd_norm_proj(x, gain, w, head_gain, cos_dup, sin_signed, *, query_chunks, tm, tn):
    t, d = x.shape
    n = w.shape[1]
    assert t % tm == 0 and n % tn == 0 and tn % LANE == 0
    n_chunks = tn // LANE
    n_rope = head_gain.shape[0]
    s = cos_dup.shape[0]
    assert s % tm == 0
    sb = s // tm
    return pl.pallas_call(
        functools.partial(_odd_proj_kernel, n_chunks=n_chunks, n_rope=n_rope, n_tiles=n // tn,
                          query_chunks=frozenset(query_chunks)),
        grid=(t // tm, n // tn),
        in_specs=[
            pl.BlockSpec((tm, d), lambda i, j: (i, 0)),
            pl.BlockSpec((1, d), lambda i, j: (0, 0)),
            pl.BlockSpec((d, tn), lambda i, j: (0, j)),
            pl.BlockSpec((n_rope, LANE), lambda i, j: (0, 0)),
            pl.BlockSpec((tm, LANE), lambda i, j: (i % sb, 0)),
            pl.BlockSpec((tm, LANE), lambda i, j: (i % sb, 0)),
        ],
        out_specs=pl.BlockSpec((n_chunks, tm, LANE), lambda i, j: (j, i, 0)),
        out_shape=jax.ShapeDtypeStruct((n // LANE, t, LANE), BF16),
        scratch_shapes=[pltpu.VMEM((tm, d), BF16)],
        compiler_params=_params("parallel", "arbitrary"),
        name="odd_norm_proj",
    )(x, gain.reshape(1, d), w, head_gain, cos_dup, sin_signed)


A_TQ = 256
A_BLOCKS_PER_ITER = 8


def _a_geometry(seq, window, dilation):
    sub = seq // dilation
    half = window // (2 * dilation)
    tq = min(A_TQ, sub)
    win = min(tq + 2 * half, sub)
    assert seq % dilation == 0 and sub % tq == 0 and half % 16 == 0 and win % LANE == 0
    assert win == sub or win == tq + 2 * half
    return sub, half, tq, win


def _dilated_kernel(*refs, geoms):
    ng = len(geoms)
    qkv = refs[:3 * ng]
    bms = refs[3 * ng:4 * ng]
    o_ref = refs[4 * ng]
    m_s, l_s, acc_s = refs[4 * ng + 1:]

    order = sorted(range(ng), key=lambda g: -geoms[g][0])
    for pos, g in enumerate(order):
        dilation, sub, half, tq, win = geoms[g]
        q_ref, k_ref, v_ref = qkv[3 * g:3 * g + 3]
        bm_ref = bms[g]
        nq = sub // tq
        first = pos == 0
        nblocks = dilation * nq
        batch = math.gcd(A_BLOCKS_PER_ITER, nblocks)

        def one_block(it, dilation=dilation, sub=sub, half=half, tq=tq, win=win, nq=nq,
                      q_ref=q_ref, k_ref=k_ref, v_ref=v_ref, bm_ref=bm_ref, first=first):
            r = it // nq
            qi = it % nq
            q0 = pl.multiple_of(qi * tq, tq)
            kstart = pl.multiple_of(jnp.clip(q0 - half, 0, sub - win), half)
            if dilation == 1:
                q = q_ref[0, pl.ds(q0, tq), :]
                kw = k_ref[0, pl.ds(kstart, win), :]
                vw = v_ref[0, pl.ds(kstart, win), :]
                rows = pl.ds(q0, tq)
            else:
                q = q_ref[0, 0, r, pl.ds(q0, tq), :]
                kw = k_ref[0, 0, r, pl.ds(kstart, win), :]
                vw = v_ref[0, 0, r, pl.ds(kstart, win), :]
                rows = pl.ds(q0 * dilation + r, tq, stride=dilation)
            variant = 0 if nq == 1 else jnp.where(qi == 0, 0, jnp.where(qi == nq - 1, 2, 1))
            s = _dot_nt(q, kw) + bm_ref[variant, 0]
            mb = jnp.max(s, axis=-1, keepdims=True)
            if first:
                p = jnp.exp2(s - mb)
                return (rows, jnp.broadcast_to(mb, (tq, LANE)),
                        jnp.broadcast_to(jnp.sum(p, axis=-1, keepdims=True), (tq, LANE)),
                        _dot(p.astype(BF16), vw))
            m_old = m_s[rows, :]
            m_new = jnp.maximum(m_old, mb)
            alpha = jnp.exp2(m_old - m_new)
            p = jnp.exp2(s - _lanes(m_new, win))
            return (rows, m_new, alpha * l_s[rows, :] + jnp.sum(p, axis=-1, keepdims=True),
                    alpha * acc_s[rows, :] + _dot(p.astype(BF16), vw))

        def blocks(step, carry, batch=batch, one_block=one_block):
            results = [one_block(step * batch + i) for i in range(batch)]
            for rows, m_new, l_new, acc_new in results:
                m_s[rows, :] = m_new
                l_s[rows, :] = l_new
                acc_s[rows, :] = acc_new
            return carry

        lax.fori_loop(0, nblocks // batch, blocks, 0)

    o_ref[...] = (acc_s[...] / l_s[...]).astype(o_ref.dtype)


def _dilated_bias_mask(t5_cols, dilation, half, tq, win, sub):
    nq = sub // tq
    offs = [0] if nq == 1 else [0, -half, tq - win]
    tabs = []
    for off in offs:
        x = np.arange(tq + win - 1) - (tq - 1) + off
        fv = jnp.where(jnp.asarray(np.abs(x) <= half)[None],
                       LOG2E * t5_cols[_t5_bucket(jnp.asarray(dilation * x, jnp.int32))].T.astype(F32), NEG_INF)
        tabs.append(_toeplitz(fv, tq, win))
    return jnp.stack(tabs)


def dilated_attention(nat, dil_projs, t5_table, *, batch, seq):
    t = batch * seq
    geoms, args, in_specs, bms = [], [], [], []
    for gi, (window, dilation) in enumerate(DILATED_PATTERNS):
        sub, half, tq, win = _a_geometry(seq, window, dilation)
        geoms.append((dilation, sub, half, tq, win))
        for part in range(3):
            if dilation == 1:
                args.append(nat)
                in_specs.append(pl.BlockSpec(
                    (1, seq, LANE), lambda b, h, part=part: (part * A_HEADS + h, b, 0)))
            else:
                args.append(dil_projs[gi - 1])
                in_specs.append(pl.BlockSpec(
                    (1, 1, dilation, sub, LANE),
                    lambda b, h, part=part: (part * A_HEADS + h, b, 0, 0, 0)))
        bms.append(_dilated_bias_mask(t5_table[:, gi * A_HEADS:(gi + 1) * A_HEADS],
                                      dilation, half, tq, win, sub))
    for bm in bms:
        args.append(bm)
        in_specs.append(pl.BlockSpec((bm.shape[0], 1) + bm.shape[2:], lambda b, h: (0, h, 0, 0)))
    return pl.pallas_call(
        functools.partial(_dilated_kernel, geoms=tuple(geoms)),
        grid=(batch, A_HEADS),
        in_specs=in_specs,
        out_specs=pl.BlockSpec((seq, LANE), lambda b, h: (b, h)),
        out_shape=jax.ShapeDtypeStruct((t, A_HEADS * LANE), BF16),
        scratch_shapes=[pltpu.VMEM((seq, LANE), F32)] * 3,
        compiler_params=_params("parallel", "arbitrary"),
        name="dilated_attn",
    )(*args)


KEY_SPLITS = 4


def _partial_softmax(s, v):
    m = jnp.max(s, axis=-1, keepdims=True)
    p = jnp.exp2(s - m)
    return m, jnp.sum(p, axis=-1, keepdims=True), _dot(p.astype(BF16), v)


def _merge_partials(parts):
    m = parts[0][0]
    for pm, _, _ in parts[1:]:
        m = jnp.maximum(m, pm)
    weights = [jnp.exp2(pm - m) for pm, _, _ in parts]
    denom = sum(w * pl_ for w, (_, pl_, _) in zip(weights, parts))
    acc = sum(w * pa for w, (_, _, pa) in zip(weights, parts))
    return acc / denom


T5_BAND = T5_MAX_DIST


def _diff_band(tq):
    lo = -((T5_BAND + LANE - 1) // LANE) - 1
    hi = (T5_BAND + tq - 1 + LANE - 1) // LANE
    return lo, hi


def _diff_kernel(q_ref, k_ref, v_ref, tz_ref, lq1_ref, lk1_ref, lq2_ref, lk2_ref, sub_ref, o_ref, *,
                 tq, seq, lambda_init):
    qi = pl.program_id(2)
    lo, hi = _diff_band(tq)
    q_chunk0 = qi * (tq // LANE)
    span = seq // KEY_SPLITS

    def attend(mi):
        parts = []
        for part in range(KEY_SPLITS):
            k0 = part * span
            bias = jnp.concatenate(
                [tz_ref[0, jnp.clip(cc - q_chunk0, lo, hi) - lo]
                 for cc in range(k0 // LANE, (k0 + span) // LANE)], axis=1)
            s = _dot_nt(q_ref[mi], k_ref[mi, k0:k0 + span, :]) + bias
            vc = jnp.concatenate([v_ref[0, k0:k0 + span, :], v_ref[1, k0:k0 + span, :]], axis=1)
            parts.append(_partial_softmax(s, vc))
        return _merge_partials(parts)

    lam = (jnp.exp(jnp.sum(lq1_ref[...] * lk1_ref[...], axis=-1, keepdims=True))
           - jnp.exp(jnp.sum(lq2_ref[...] * lk2_ref[...], axis=-1, keepdims=True)) + lambda_init)
    o = attend(0) - lam * attend(1)
    o_ref[...] = (_rms(o, sub_ref[...]) * (1.0 - lambda_init)).astype(o_ref.dtype)


def _diff_toeplitz(t5_cols, tq):
    lo, hi = _diff_band(tq)
    n_off = hi - lo + 1
    x = np.arange(tq + n_off * LANE - 1) - (tq - 1) + LANE * lo
    fv = LOG2E * t5_cols[_t5_bucket(jnp.asarray(x, jnp.int32))].T.astype(F32)
    tz = _toeplitz(fv, tq, n_off * LANE)
    return jnp.transpose(tz.reshape(-1, tq, n_off, LANE), (0, 2, 1, 3))


def diff_attention(proj, t5_cols, lq1, lk1, lq2, lk2, subln, *, chunk0, batch, seq, lambda_init):
    nc, t, _ = proj.shape
    tq = 256
    assert seq % tq == 0 and seq % (KEY_SPLITS * LANE) == 0 and chunk0 % 2 == 0
    nq = seq // tq
    tz = _diff_toeplitz(t5_cols, tq)
    n_off = tz.shape[1]
    qb, kb, vb = chunk0 // 2, chunk0 // 2 + B_HEADS, chunk0 // 2 + 2 * B_HEADS
    vec = lambda a: a.reshape(1, -1).astype(F32)
    vspec = lambda n: pl.BlockSpec((1, n), lambda b, h, qi: (0, 0))
    return pl.pallas_call(
        functools.partial(_diff_kernel, tq=tq, seq=seq, lambda_init=lambda_init),
        grid=(batch, B_HEADS, nq),
        in_specs=[
            pl.BlockSpec((2, tq, LANE), lambda b, h, qi: (qb + h, b * nq + qi, 0)),
            pl.BlockSpec((2, seq, LANE), lambda b, h, qi: (kb + h, b, 0)),
            pl.BlockSpec((2, seq, LANE), lambda b, h, qi: (vb + h, b, 0)),
            pl.BlockSpec((1, n_off, tq, LANE), lambda b, h, qi: (h, 0, 0, 0)),
            vspec(B_QK_DIM), vspec(B_QK_DIM), vspec(B_QK_DIM), vspec(B_QK_DIM), vspec(B_V_DIM),
        ],
        out_specs=pl.BlockSpec((tq, B_V_DIM), lambda b, h, qi: (b * nq + qi, h)),
        out_shape=jax.ShapeDtypeStruct((t, B_HEADS * B_V_DIM), BF16),
        compiler_params=_params("parallel", "parallel", "arbitrary"),
        name="diff_attn",
    )(proj, proj, proj, tz, vec(lq1), vec(lk1), vec(lq2), vec(lk2), vec(subln))


def _gqa_kernel(q_ref, k_ref, v_ref, o_ref, *, tq, seq, group):
    span = seq // KEY_SPLITS
    bounds = [0] + [span // 2 + i * span for i in range(KEY_SPLITS)] + [seq]
    q = q_ref[...].reshape(group * tq, HEAD_DIM)
    parts = []
    for k0, k1 in zip(bounds[:-1], bounds[1:]):
        parts.append(_partial_softmax(_dot_nt(q, k_ref[0, k0:k1, :]), v_ref[0, k0:k1, :]))
    o = _merge_partials(parts)
    for g in range(group):
        o_ref[:, g * HEAD_DIM:(g + 1) * HEAD_DIM] = o[g * tq:(g + 1) * tq].astype(o_ref.dtype)


def gqa_attention(proj, *, batch, seq):
    nc, t, _ = proj.shape
    tq = 256
    group = C_HEADS // C_KV_HEADS
    nq = seq // tq
    return pl.pallas_call(
        functools.partial(_gqa_kernel, tq=tq, seq=seq, group=group),
        grid=(batch, C_KV_HEADS, nq),
        in_specs=[
            pl.BlockSpec((group, tq, LANE), lambda b, n, qi: (n, b * nq + qi, 0)),
            pl.BlockSpec((1, seq, LANE), lambda b, n, qi: (C_HEADS + n, b, 0)),
            pl.BlockSpec((1, seq, LANE), lambda b, n, qi: (C_HEADS + C_KV_HEADS + n, b, 0)),
        ],
        out_specs=pl.BlockSpec((tq, group * HEAD_DIM), lambda b, n, qi: (b * nq + qi, n)),
        out_shape=jax.ShapeDtypeStruct((t, C_HEADS * HEAD_DIM), BF16),
        compiler_params=_params("parallel", "parallel", "arbitrary"),
        name="gqa_attn",
    )(proj, proj, proj)


NA_QROWS = 4
NA_KROWS = NA_QROWS + NA_ROWS
NA_BLOCKS_PER_STEP = 4


def _na_kernel(q_ref, k_ref, v_ref, *rest, rows, nsub):
    bm_refs, o_ref = rest[:nsub], rest[nsub]
    step = pl.program_id(2)
    tq = NA_QROWS * GRID_W
    nkeys = NA_KROWS * GRID_W
    for c in range(nsub):
        blk = step * nsub + c
        wrow = jnp.clip(blk * NA_QROWS - NA_ROWS // 2, 0, rows - NA_KROWS)
        kstart = pl.multiple_of(wrow * GRID_W, NA_QROWS * GRID_W)
        kw = k_ref[0, pl.ds(kstart, nkeys), :]
        vw = v_ref[0, pl.ds(kstart, nkeys), :]
        s = _dot_nt(q_ref[0, c * tq:(c + 1) * tq, :], kw) + bm_refs[c][0, 0]
        m = jnp.max(s, axis=-1, keepdims=True)
        p = jnp.exp2(s - m)
        denom = jnp.sum(p, axis=-1, keepdims=True)
        o_ref[c * tq:(c + 1) * tq, :] = (_dot(p.astype(BF16), vw) / denom).astype(o_ref.dtype)


def _na_bias_mask(rpb, rows):
    kr = min(NA_ROWS, rows)
    nblk = rows // NA_QROWS
    nheads = rpb.shape[0]
    pad = GRID_W - NA_COLS
    col_blocks = _toeplitz(jnp.pad(LOG2E * rpb.astype(F32), ((0, 0), (0, 0), (pad, pad))), GRID_W, GRID_W)
    qj = np.arange(GRID_W)[:, None]
    kj = np.arange(GRID_W)[None, :]
    cs = np.clip(qj - NA_COLS // 2, 0, GRID_W - NA_COLS)
    col_ok = jnp.asarray((kj >= cs) & (kj < cs + NA_COLS))
    col_blocks = jnp.where(col_ok[None, None], col_blocks, NEG_INF)
    masked = jnp.full((nheads, GRID_W, GRID_W), NEG_INF, F32)
    tabs = []
    for blk in (0, 1, nblk - 1):
        wrow = int(np.clip(blk * NA_QROWS - NA_ROWS // 2, 0, rows - NA_KROWS))
        row_strips = []
        for a in range(NA_QROWS):
            qi = blk * NA_QROWS + a
            rs = int(np.clip(qi - kr // 2, 0, rows - kr))
            strip = []
            for c in range(NA_KROWS):
                ki = wrow + c
                strip.append(col_blocks[:, ki - qi + NA_ROWS - 1] if rs <= ki < rs + kr else masked)
            row_strips.append(jnp.concatenate(strip, axis=-1))
        tabs.append(jnp.concatenate(row_strips, axis=-2))
    return jnp.stack(tabs)


def na_attention(proj, rpb, *, chunk0, batch, seq):
    nc, t, _ = proj.shape
    rows = seq // GRID_W
    nsub = NA_BLOCKS_PER_STEP
    assert rows % (NA_QROWS * nsub) == 0 and rows >= NA_KROWS and rows // NA_QROWS >= 3
    nblk = rows // NA_QROWS
    nstep = nblk // nsub
    tq = NA_QROWS * GRID_W
    nkeys = NA_KROWS * GRID_W
    bm = _na_bias_mask(rpb, rows)

    def bm_spec(c):
        def index(b, h, step):
            blk = step * nsub + c
            return (jnp.where(blk == 0, 0, jnp.where(blk == nblk - 1, 2, 1)), h, 0, 0)
        return pl.BlockSpec((1, 1, tq, nkeys), index)

    return pl.pallas_call(
        functools.partial(_na_kernel, rows=rows, nsub=nsub),
        grid=(batch, D_HEADS, nstep),
        in_specs=[
            pl.BlockSpec((1, nsub * tq, LANE), lambda b, h, step: (chunk0 + h, b * nstep + step, 0)),
            pl.BlockSpec((1, seq, LANE), lambda b, h, step: (chunk0 + D_HEADS + h, b, 0)),
            pl.BlockSpec((1, seq, LANE), lambda b, h, step: (chunk0 + 2 * D_HEADS + h, b, 0)),
        ] + [bm_spec(c) for c in range(nsub)],
        out_specs=pl.BlockSpec((nsub * tq, LANE), lambda b, h, step: (b * nstep + step, h)),
        out_shape=jax.ShapeDtypeStruct((t, D_HEADS * HEAD_DIM), BF16),
        compiler_params=_params("parallel", "parallel", "arbitrary"),
        name="na_attn",
    )(proj, proj, proj, *([bm] * nsub))


def _out_proj_kernel(a_ref, b_ref, w_ref, h_ref, out_ref, *, na):
    out_ref[...] = h_ref[...] + _dot(a_ref[...], w_ref[:na, :]) + _dot(b_ref[...], w_ref[na:, :])


def out_proj(a, b, w, h, *, tm, tn):
    t, d = h.shape
    na, nb = a.shape[1], b.shape[1]
    return pl.pallas_call(
        functools.partial(_out_proj_kernel, na=na),
        grid=(t // tm, d // tn),
        in_specs=[pl.BlockSpec((tm, na), lambda i, j: (i, 0)),
                  pl.BlockSpec((tm, nb), lambda i, j: (i, 0)),
                  pl.BlockSpec((na + nb, tn), lambda i, j: (0, j)),
                  pl.BlockSpec((tm, tn), lambda i, j: (i, j))],
        out_specs=pl.BlockSpec((tm, tn), lambda i, j: (i, j)),
        out_shape=jax.ShapeDtypeStruct((t, d), F32),
        compiler_params=_params("parallel", "arbitrary"),
        name="out_proj",
    )(a, b, w, h)


FFN_HALO = 16


def _gelu_tanh(x):
    return 0.5 * x * (1.0 + jnp.tanh(math.sqrt(2.0 / math.pi) * (x + 0.044715 * (x * x * x))))


def _ffn_kernel(x_ref, xp_ref, xnx_ref, g_ref, wg_ref, wu_ref, cw_ref, cb_ref, wd_ref, gf_ref, out_ref,
                xn_ref, *, tm, tiles_per_seq, final_norm):
    i = pl.program_id(0)
    f = pl.program_id(1)
    nf = pl.num_programs(1)
    hl = FFN_HALO

    @pl.when(f == 0)
    def _():
        gain = g_ref[...]
        keep_prev = jnp.where(i % tiles_per_seq != 0, 1.0, 0.0)
        keep_next = jnp.where(i % tiles_per_seq != tiles_per_seq - 1, 1.0, 0.0)
        xn_ref[:hl, :] = (_rms(xp_ref[...], gain) * keep_prev).astype(BF16)
        xn_ref[hl:hl + tm, :] = _rms(x_ref[...], gain).astype(BF16)
        xn_ref[hl + tm:, :] = (_rms(xnx_ref[...], gain) * keep_next).astype(BF16)
        out_ref[...] = x_ref[...]

    ge = _dot(xn_ref[...], wg_ref[...])
    cw = cw_ref[...]
    g = (cw[0:1, :] * ge[hl - 1:hl - 1 + tm] + cw[1:2, :] * ge[hl:hl + tm]
         + cw[2:3, :] * ge[hl + 1:hl + 1 + tm] + cb_ref[...])
    u = _dot(xn_ref[hl:hl + tm, :], wu_ref[...])
    act = (_gelu_tanh(g) * u).astype(BF16)
    out_ref[...] += _dot(act, wd_ref[...])

    if final_norm:
        @pl.when(f == nf - 1)
        def _():
            out_ref[...] = _rms(out_ref[...], gf_ref[...])


def conv_ffn_block(h, gain, w_up, conv_w, conv_b, w_down, final_gain, *, seq, tm, tf, final_norm):
    t, d = h.shape
    ff = w_down.shape[0]
    assert t % tm == 0 and seq % tm == 0 and ff % tf == 0 and tm % FFN_HALO == 0
    nfb = ff // tf
    hb = tm // FFN_HALO
    last_halo = t // FFN_HALO - 1
    return pl.pallas_call(
        functools.partial(_ffn_kernel, tm=tm, tiles_per_seq=seq // tm, final_norm=final_norm),
        grid=(t // tm, nfb),
        in_specs=[
            pl.BlockSpec((tm, d), lambda i, f: (i, 0)),
            pl.BlockSpec((FFN_HALO, d), lambda i, f: (jnp.maximum(i * hb - 1, 0), 0)),
            pl.BlockSpec((FFN_HALO, d), lambda i, f: (jnp.minimum((i + 1) * hb, last_halo), 0)),
            pl.BlockSpec((1, d), lambda i, f: (0, 0)),
            pl.BlockSpec((d, tf), lambda i, f: (0, f)),
            pl.BlockSpec((d, tf), lambda i, f: (0, nfb + f)),
            pl.BlockSpec((3, tf), lambda i, f: (0, f)),
            pl.BlockSpec((1, tf), lambda i, f: (0, f)),
            pl.BlockSpec((tf, d), lambda i, f: (f, 0)),
            pl.BlockSpec((1, d), lambda i, f: (0, 0)),
        ],
        out_specs=pl.BlockSpec((tm, d), lambda i, f: (i, 0)),
        out_shape=jax.ShapeDtypeStruct((t, d), F32),
        scratch_shapes=[pltpu.VMEM((tm + 2 * FFN_HALO, d), BF16)],
        compiler_params=_params("parallel", "arbitrary"),
        name="conv_ffn",
    )(h, h, h, gain.reshape(1, d), w_up, w_up, conv_w, conv_b.reshape(1, ff), w_down,
      final_gain.reshape(1, d))


def _rope_tables(seq):
    tpos = jnp.arange(seq, dtype=jnp.int32)
    row = (tpos // GRID_W).astype(F32)
    col = (tpos % GRID_W).astype(F32)
    inv_freq = ROPE_THETA ** (-(jnp.arange(0, ROPE_AXIS_DIM, 2, dtype=F32) / ROPE_AXIS_DIM))
    ang = jnp.concatenate([row[:, None] * inv_freq[None], col[:, None] * inv_freq[None]], axis=-1)
    cos, sin = jnp.cos(ang), jnp.sin(ang)
    cos_dup = jnp.repeat(cos, 2, axis=-1)
    sin_signed = jnp.stack([-sin, sin], axis=-1).reshape(seq, HEAD_DIM)
    return cos_dup, sin_signed


def kernel(x, ln_mix, ln_ffn, ln_final, t5_table, ev_w_in, ev_w_out, diff_lq1, diff_lk1, diff_lq2,
           diff_lk2, diff_subln, od_w_in, od_w_out, gqa_q_norm, gqa_k_norm, na_rpb, ffn_w_up,
           ffn_conv_w, ffn_conv_b, ffn_w_down):
    batch, seq, d = x.shape
    depth = ln_mix.shape[0]
    t = batch * seq
    h = x.reshape(t, d)

    for layer in range(depth):
        if layer % 2 == 0:
            e = layer // 2
            lambda_init = 0.8 - 0.6 * math.exp(-0.3 * layer)
            nat, *dil_projs = even_norm_proj(h, ln_mix[layer], ev_w_in[e].astype(BF16),
                                             batch=batch, seq=seq, tm=1024, tn=1024)
            oa = dilated_attention(nat, dil_projs, t5_table, batch=batch, seq=seq)
            ob = diff_attention(nat, t5_table[:, A_GROUPS * A_HEADS:], diff_lq1[e], diff_lk1[e],
                                diff_lq2[e], diff_lk2[e], diff_subln[e], chunk0=3 * A_HEADS,
                                batch=batch, seq=seq, lambda_init=lambda_init)
            h = out_proj(oa, ob, ev_w_out[e].astype(BF16), h, tm=512, tn=1024)
        else:
            o = layer // 2
            n_rope = C_HEADS + C_KV_HEADS
            head_gain = jnp.concatenate([jnp.tile(gqa_q_norm[o][None], (C_HEADS, 1)),
                                         jnp.tile(gqa_k_norm[o][None], (C_KV_HEADS, 1))], axis=0)
            cos_dup, sin_signed = _rope_tables(seq)
            na_chunk0 = n_rope + C_KV_HEADS
            query_chunks = list(range(C_HEADS)) + list(range(na_chunk0, na_chunk0 + D_HEADS))
            proj = odd_norm_proj(h, ln_mix[layer], od_w_in[o].astype(BF16), head_gain.astype(F32),
                                 cos_dup, sin_signed, query_chunks=query_chunks, tm=1024, tn=1536)
            oc = gqa_attention(proj, batch=batch, seq=seq)
            od = na_attention(proj, na_rpb[o], chunk0=na_chunk0, batch=batch, seq=seq)
            h = out_proj(oc, od, od_w_out[o].astype(BF16), h, tm=512, tn=1024)
        h = conv_ffn_block(h, ln_ffn[layer], ffn_w_up[layer].astype(BF16), ffn_conv_w[layer],
                           ffn_conv_b[layer], ffn_w_down[layer].astype(BF16), ln_final,
                           seq=seq, tm=512, tf=512, final_norm=(layer == depth - 1))
    return h.reshape(batch, seq, d)
```

```python
import functools
import math

import numpy as np
import jax
import jax.numpy as jnp
from jax import lax
from jax.experimental import pallas as pl
from jax.experimental.pallas import tpu as pltpu

HEAD_DIM = 128
GRID_W = 64
RMS_EPS = 1e-6
NEG_INF = -1e30
T5_BUCKETS = 32
T5_MAX_DIST = 1024
DILATED_PATTERNS = ((128, 1), (512, 4), (2048, 16))
A_GROUPS = len(DILATED_PATTERNS)
A_HEADS = 8
B_HEADS = 4
B_QK_DIM = 128
B_V_DIM = 2 * B_QK_DIM
C_HEADS = 8
C_KV_HEADS = 2
ROPE_THETA = 10000.0
ROPE_AXIS_DIM = HEAD_DIM // 2
D_HEADS = 8
NA_ROWS = 8
NA_COLS = 16

LANE = 128
V7X_VMEM_LIMIT = 56 * 1024 * 1024

BF16 = jnp.bfloat16
F32 = jnp.float32

LOG2E = math.log2(math.e)
QUERY_SCALE = HEAD_DIM ** -0.5 * LOG2E


def _params(*sem):
    return pltpu.CompilerParams(dimension_semantics=sem, vmem_limit_bytes=V7X_VMEM_LIMIT)


def _rms(xf, gain):
    ms = jnp.mean(xf * xf, axis=-1, keepdims=True)
    return xf * lax.rsqrt(ms + RMS_EPS) * gain


def _dot_nt(a, b):
    return lax.dot_general(a, b, (((1,), (1,)), ((), ())), preferred_element_type=F32)


def _dot(a, b):
    return jnp.dot(a, b, preferred_element_type=F32)


def _column_tiles(w, tn):
    k, n = w.shape
    return jnp.transpose(w.astype(BF16).reshape(k, n // tn, tn), (1, 0, 2))


def _lanes(x, width):
    return x if width == LANE else jnp.concatenate([x] * (width // LANE), axis=1)


def _t5_bucket(rel):
    nb = T5_BUCKETS // 2
    max_exact = nb // 2
    ret = jnp.where(rel > 0, nb, 0)
    n = jnp.abs(rel)
    n_f = jnp.maximum(n, 1).astype(F32)
    large = max_exact + (jnp.log(n_f / max_exact) / math.log(T5_MAX_DIST / max_exact)
                         * (nb - max_exact)).astype(jnp.int32)
    large = jnp.minimum(large, nb - 1)
    return ret + jnp.where(n < max_exact, n, large)


def _toeplitz(fv, rows, cols):
    n = rows + cols - 1
    assert fv.shape[-1] == n
    v = jnp.concatenate([fv[..., rows - 1:], fv[..., :rows - 1]], axis=-1)
    flat = jnp.tile(v, rows)[..., :rows * (n - 1)]
    return flat.reshape(fv.shape[:-1] + (rows, n - 1))[..., :cols]


def _store_chunks(acc, o_ref, n_chunks):
    for c in range(n_chunks):
        o_ref[c] = acc[:, c * LANE:(c + 1) * LANE].astype(o_ref.dtype)


def _even_proj_kernel(x_ref, g_ref, w_ref, nat_ref, *rest, plan, n_chunks, tm, qkv_period):
    dil_refs, (xn_ref, slab_ref) = rest[:-2], rest[-2:]
    j = pl.program_id(1)

    @pl.when(j == 0)
    def _():
        xn_ref[...] = _rms(x_ref[...], g_ref[...]).astype(BF16)

    factor = jnp.where(j % qkv_period == 0, QUERY_SCALE, 1.0)

    for lo, hi, dilation, out_idx in plan:
        @pl.when((j >= lo) & (j < hi))
        def _(dilation=dilation, out_idx=out_idx):
            acc = _dot(xn_ref[...], w_ref[0]) * factor
            if dilation == 1:
                _store_chunks(acc, nat_ref, n_chunks)
                return
            o_ref = dil_refs[out_idx]
            per = tm // dilation
            for c in range(n_chunks):
                slab_ref[c] = acc[:, c * LANE:(c + 1) * LANE]
            for c in range(n_chunks):
                for r in range(dilation):
                    o_ref[c, 0, r] = slab_ref[c, pl.ds(r, per, stride=dilation), :].astype(o_ref.dtype)


def even_norm_proj(x, gain, w, *, batch, seq, tm, tn):
    t, d = x.shape
    n = w.shape[1]
    w = _column_tiles(w, tn)
    n_chunks = tn // LANE
    group_cols = 3 * A_HEADS * HEAD_DIM
    assert group_cols % tn == 0 and n % tn == 0 and seq % tm == 0 and t == batch * seq
    assert tn == A_HEADS * HEAD_DIM == B_HEADS * 2 * B_QK_DIM and B_QK_DIM == HEAD_DIM
    gt = group_cols // tn
    nj = n // tn
    tiles_per_seq = seq // tm
    dils = [dil for _, dil in DILATED_PATTERNS]
    assert dils[0] == 1 and all(tm % (16 * dil) == 0 for dil in dils)
    plan = [(0, gt, 1, None)]
    for gi in range(1, A_GROUPS):
        plan.append((gi * gt, (gi + 1) * gt, dils[gi], gi - 1))
    plan.append((A_GROUPS * gt, nj, 1, None))
    nat_tiles = gt + (nj - A_GROUPS * gt)

    def nat_map(i, j):
        return (jnp.where(j < gt, j, jnp.where(j < A_GROUPS * gt, gt - 1, j - (A_GROUPS - 1) * gt)), i, 0)

    def dil_map(gi):
        return lambda i, j: (jnp.clip(j - gi * gt, 0, gt - 1), i // tiles_per_seq, 0, i % tiles_per_seq, 0)

    out_specs = [pl.BlockSpec((n_chunks, tm, LANE), nat_map)]
    out_shape = [jax.ShapeDtypeStruct((nat_tiles * n_chunks, t, LANE), BF16)]
    for gi in range(1, A_GROUPS):
        dil = dils[gi]
        out_specs.append(pl.BlockSpec((n_chunks, 1, dil, tm // dil, LANE), dil_map(gi)))
        out_shape.append(jax.ShapeDtypeStruct((gt * n_chunks, batch, dil, seq // dil, LANE), BF16))
    return pl.pallas_call(
        functools.partial(_even_proj_kernel, plan=tuple(plan), n_chunks=n_chunks, tm=tm, qkv_period=gt),
        grid=(t // tm, nj),
        in_specs=[
            pl.BlockSpec((tm, d), lambda i, j: (i, 0)),
            pl.BlockSpec((1, d), lambda i, j: (0, 0)),
            pl.BlockSpec((1, d, tn), lambda i, j: (j, 0, 0)),
        ],
        out_specs=out_specs,
        out_shape=out_shape,
        scratch_shapes=[pltpu.VMEM((tm, d), BF16), pltpu.VMEM((n_chunks, tm, LANE), F32)],
        compiler_params=_params("parallel", "arbitrary"),
        name="even_norm_proj",
    )(x, gain.reshape(1, d), w)


def _rope_chunk(y, hg, cosd, sins, even_lane):
    y = _rms(y, hg)
    partner = jnp.where(even_lane, pltpu.roll(y, LANE - 1, axis=1), pltpu.roll(y, 1, axis=1))
    return y * cosd + partner * sins


def _odd_proj_kernel(x_ref, g_ref, w_ref, hg_ref, cos_ref, sin_ref, o_ref, xn_ref, *,
                     n_chunks, n_rope, n_tiles, query_chunks):
    j = pl.program_id(1)

    @pl.when(j == 0)
    def _():
        xn_ref[...] = _rms(x_ref[...], g_ref[...]).astype(BF16)

    acc = _dot(xn_ref[...], w_ref[0])

    for jt in range(n_tiles):
        @pl.when(j == jt)
        def _(jt=jt):
            if jt * n_chunks < n_rope:
                cosd = cos_ref[...]
                sins = sin_ref[...]
                even_lane = (lax.broadcasted_iota(jnp.int32, cosd.shape, 1) % 2) == 0
            for c in range(n_chunks):
                gc = jt * n_chunks + c
                y = acc[:, c * LANE:(c + 1) * LANE]
                if gc < n_rope:
                    y = _rope_chunk(y, hg_ref[gc:gc + 1, :], cosd, sins, even_lane)
                if gc in query_chunks:
                    y = y * QUERY_SCALE
                o_ref[c] = y.astype(o_ref.dtype)


def odd_norm_proj(x, gain, w, head_gain, cos_dup, sin_signed, *, query_chunks, tm, tn):
    t, d = x.shape
    n = w.shape[1]
    assert t % tm == 0 and n % tn == 0 and tn % LANE == 0
    w = _column_tiles(w, tn)
    n_chunks = tn // LANE
    n_rope = head_gain.shape[0]
    s = cos_dup.shape[0]
    assert s % tm == 0
    sb = s // tm
    return pl.pallas_call(
        functools.partial(_odd_proj_kernel, n_chunks=n_chunks, n_rope=n_rope, n_tiles=n // tn,
                          query_chunks=frozenset(query_chunks)),
        grid=(t // tm, n // tn),
        in_specs=[
            pl.BlockSpec((tm, d), lambda i, j: (i, 0)),
            pl.BlockSpec((1, d), lambda i, j: (0, 0)),
            pl.BlockSpec((1, d, tn), lambda i, j: (j, 0, 0)),
            pl.BlockSpec((n_rope, LANE), lambda i, j: (0, 0)),
            pl.BlockSpec((tm, LANE), lambda i, j: (i % sb, 0)),
            pl.BlockSpec((tm, LANE), lambda i, j: (i % sb, 0)),
        ],
        out_specs=pl.BlockSpec((n_chunks, tm, LANE), lambda i, j: (j, i, 0)),
        out_shape=jax.ShapeDtypeStruct((n // LANE, t, LANE), BF16),
        scratch_shapes=[pltpu.VMEM((tm, d), BF16)],
        compiler_params=_params("parallel", "arbitrary"),
        name="odd_norm_proj",
    )(x, gain.reshape(1, d), w, head_gain, cos_dup, sin_signed)


A_TQ = 256
A_BLOCKS_PER_ITER = 8


def _a_geometry(seq, window, dilation):
    sub = seq // dilation
    half = window // (2 * dilation)
    tq = min(A_TQ, sub)
    win = min(tq + 2 * half, sub)
    assert seq % dilation == 0 and sub % tq == 0 and half % 16 == 0 and win % LANE == 0
    assert win == sub or win == tq + 2 * half
    return sub, half, tq, win


def _dilated_kernel(*refs, geoms):
    ng = len(geoms)
    qkv = refs[:3 * ng]
    bms = refs[3 * ng:4 * ng]
    o_ref = refs[4 * ng]
    m_s, l_s, acc_s = refs[4 * ng + 1:]

    order = sorted(range(ng), key=lambda g: -geoms[g][0])
    for pos, g in enumerate(order):
        dilation, sub, half, tq, win = geoms[g]
        q_ref, k_ref, v_ref = qkv[3 * g:3 * g + 3]
        bm_ref = bms[g]
        nq = sub // tq
        first = pos == 0
        nblocks = dilation * nq
        batch = math.gcd(A_BLOCKS_PER_ITER, nblocks)

        def one_block(it, dilation=dilation, sub=sub, half=half, tq=tq, win=win, nq=nq,
                      q_ref=q_ref, k_ref=k_ref, v_ref=v_ref, bm_ref=bm_ref, first=first):
            r = it // nq
            qi = it % nq
            q0 = pl.multiple_of(qi * tq, tq)
            kstart = pl.multiple_of(jnp.clip(q0 - half, 0, sub - win), half)
            if dilation == 1:
                q = q_ref[0, pl.ds(q0, tq), :]
                kw = k_ref[0, pl.ds(kstart, win), :]
                vw = v_ref[0, pl.ds(kstart, win), :]
                rows = pl.ds(q0, tq)
            else:
                q = q_ref[0, 0, r, pl.ds(q0, tq), :]
                kw = k_ref[0, 0, r, pl.ds(kstart, win), :]
                vw = v_ref[0, 0, r, pl.ds(kstart, win), :]
                rows = pl.ds(q0 * dilation + r, tq, stride=dilation)
            variant = 0 if nq == 1 else jnp.where(qi == 0, 0, jnp.where(qi == nq - 1, 2, 1))
            s = _dot_nt(q, kw) + bm_ref[variant, 0]
            mb = jnp.max(s, axis=-1, keepdims=True)
            if first:
                p = jnp.exp2(s - mb)
                return (rows, jnp.broadcast_to(mb, (tq, LANE)),
                        jnp.broadcast_to(jnp.sum(p, axis=-1, keepdims=True), (tq, LANE)),
                        _dot(p.astype(BF16), vw))
            m_old = m_s[rows, :]
            m_new = jnp.maximum(m_old, mb)
            alpha = jnp.exp2(m_old - m_new)
            p = jnp.exp2(s - _lanes(m_new, win))
            return (rows, m_new, alpha * l_s[rows, :] + jnp.sum(p, axis=-1, keepdims=True),
                    alpha * acc_s[rows, :] + _dot(p.astype(BF16), vw))

        def blocks(step, carry, batch=batch, one_block=one_block):
            results = [one_block(step * batch + i) for i in range(batch)]
            for rows, m_new, l_new, acc_new in results:
                m_s[rows, :] = m_new
                l_s[rows, :] = l_new
                acc_s[rows, :] = acc_new
            return carry

        lax.fori_loop(0, nblocks // batch, blocks, 0)

    o_ref[...] = (acc_s[...] / l_s[...]).astype(o_ref.dtype)


def _dilated_bias_mask(t5_cols, dilation, half, tq, win, sub):
    nq = sub // tq
    offs = [0] if nq == 1 else [0, -half, tq - win]
    tabs = []
    for off in offs:
        x = np.arange(tq + win - 1) - (tq - 1) + off
        fv = jnp.where(jnp.asarray(np.abs(x) <= half)[None],
                       LOG2E * t5_cols[_t5_bucket(jnp.asarray(dilation * x, jnp.int32))].T.astype(F32), NEG_INF)
        tabs.append(_toeplitz(fv, tq, win))
    return jnp.stack(tabs)


def dilated_attention(nat, dil_projs, t5_table, *, batch, seq):
    t = batch * seq
    geoms, args, in_specs, bms = [], [], [], []
    for gi, (window, dilation) in enumerate(DILATED_PATTERNS):
        sub, half, tq, win = _a_geometry(seq, window, dilation)
        geoms.append((dilation, sub, half, tq, win))
        for part in range(3):
            if dilation == 1:
                args.append(nat)
                in_specs.append(pl.BlockSpec(
                    (1, seq, LANE), lambda b, h, part=part: (part * A_HEADS + h, b, 0)))
            else:
                args.append(dil_projs[gi - 1])
                in_specs.append(pl.BlockSpec(
                    (1, 1, dilation, sub, LANE),
                    lambda b, h, part=part: (part * A_HEADS + h, b, 0, 0, 0)))
        bms.append(_dilated_bias_mask(t5_table[:, gi * A_HEADS:(gi + 1) * A_HEADS],
                                      dilation, half, tq, win, sub))
    for bm in bms:
        args.append(bm)
        in_specs.append(pl.BlockSpec((bm.shape[0], 1) + bm.shape[2:], lambda b, h: (0, h, 0, 0)))
    return pl.pallas_call(
        functools.partial(_dilated_kernel, geoms=tuple(geoms)),
        grid=(batch, A_HEADS),
        in_specs=in_specs,
        out_specs=pl.BlockSpec((seq, LANE), lambda b, h: (b, h)),
        out_shape=jax.ShapeDtypeStruct((t, A_HEADS * LANE), BF16),
        scratch_shapes=[pltpu.VMEM((seq, LANE), F32)] * 3,
        compiler_params=_params("parallel", "arbitrary"),
        name="dilated_attn",
    )(*args)


KEY_SPLITS = 4


def _partial_softmax(s, v):
    m = jnp.max(s, axis=-1, keepdims=True)
    p = jnp.exp2(s - m)
    return m, jnp.sum(p, axis=-1, keepdims=True), _dot(p.astype(BF16), v)


def _merge_partials(parts):
    m = parts[0][0]
    for pm, _, _ in parts[1:]:
        m = jnp.maximum(m, pm)
    weights = [jnp.exp2(pm - m) for pm, _, _ in parts]
    denom = sum(w * pl_ for w, (_, pl_, _) in zip(weights, parts))
    acc = sum(w * pa for w, (_, _, pa) in zip(weights, parts))
    return acc / denom


T5_BAND = T5_MAX_DIST


def _diff_band(tq):
    lo = -((T5_BAND + LANE - 1) // LANE) - 1
    hi = (T5_BAND + tq - 1 + LANE - 1) // LANE
    return lo, hi


def _diff_kernel(q_ref, k_ref, v_ref, tz_ref, lq1_ref, lk1_ref, lq2_ref, lk2_ref, sub_ref, o_ref, *,
                 tq, seq, lambda_init):
    qi = pl.program_id(2)
    lo, hi = _diff_band(tq)
    q_chunk0 = qi * (tq // LANE)
    span = seq // KEY_SPLITS

    def attend(mi):
        parts = []
        for part in range(KEY_SPLITS):
            k0 = part * span
            bias = jnp.concatenate(
                [tz_ref[0, jnp.clip(cc - q_chunk0, lo, hi) - lo]
                 for cc in range(k0 // LANE, (k0 + span) // LANE)], axis=1)
            s = _dot_nt(q_ref[mi], k_ref[mi, k0:k0 + span, :]) + bias
            vc = jnp.concatenate([v_ref[0, k0:k0 + span, :], v_ref[1, k0:k0 + span, :]], axis=1)
            parts.append(_partial_softmax(s, vc))
        return _merge_partials(parts)

    lam = (jnp.exp(jnp.sum(lq1_ref[...] * lk1_ref[...], axis=-1, keepdims=True))
           - jnp.exp(jnp.sum(lq2_ref[...] * lk2_ref[...], axis=-1, keepdims=True)) + lambda_init)
    o = attend(0) - lam * attend(1)
    o_ref[...] = (_rms(o, sub_ref[...]) * (1.0 - lambda_init)).astype(o_ref.dtype)


def _diff_toeplitz(t5_cols, tq):
    lo, hi = _diff_band(tq)
    n_off = hi - lo + 1
    x = np.arange(tq + n_off * LANE - 1) - (tq - 1) + LANE * lo
    fv = LOG2E * t5_cols[_t5_bucket(jnp.asarray(x, jnp.int32))].T.astype(F32)
    tz = _toeplitz(fv, tq, n_off * LANE)
    return jnp.transpose(tz.reshape(-1, tq, n_off, LANE), (0, 2, 1, 3))


def diff_attention(proj, t5_cols, lq1, lk1, lq2, lk2, subln, *, chunk0, batch, seq, lambda_init):
    nc, t, _ = proj.shape
    tq = 256
    assert seq % tq == 0 and seq % (KEY_SPLITS * LANE) == 0 and chunk0 % 2 == 0
    nq = seq // tq
    tz = _diff_toeplitz(t5_cols, tq)
    n_off = tz.shape[1]
    qb, kb, vb = chunk0 // 2, chunk0 // 2 + B_HEADS, chunk0 // 2 + 2 * B_HEADS
    vec = lambda a: a.reshape(1, -1).astype(F32)
    vspec = lambda n: pl.BlockSpec((1, n), lambda b, h, qi: (0, 0))
    return pl.pallas_call(
        functools.partial(_diff_kernel, tq=tq, seq=seq, lambda_init=lambda_init),
        grid=(batch, B_HEADS, nq),
        in_specs=[
            pl.BlockSpec((2, tq, LANE), lambda b, h, qi: (qb + h, b * nq + qi, 0)),
            pl.BlockSpec((2, seq, LANE), lambda b, h, qi: (kb + h, b, 0)),
            pl.BlockSpec((2, seq, LANE), lambda b, h, qi: (vb + h, b, 0)),
            pl.BlockSpec((1, n_off, tq, LANE), lambda b, h, qi: (h, 0, 0, 0)),
            vspec(B_QK_DIM), vspec(B_QK_DIM), vspec(B_QK_DIM), vspec(B_QK_DIM), vspec(B_V_DIM),
        ],
        out_specs=pl.BlockSpec((tq, B_V_DIM), lambda b, h, qi: (b * nq + qi, h)),
        out_shape=jax.ShapeDtypeStruct((t, B_HEADS * B_V_DIM), BF16),
        compiler_params=_params("parallel", "parallel", "arbitrary"),
        name="diff_attn",
    )(proj, proj, proj, tz, vec(lq1), vec(lk1), vec(lq2), vec(lk2), vec(subln))


def _gqa_kernel(q_ref, k_ref, v_ref, o_ref, *, tq, seq, group):
    span = seq // KEY_SPLITS
    bounds = [0] + [span // 2 + i * span for i in range(KEY_SPLITS)] + [seq]
    q = q_ref[...].reshape(group * tq, HEAD_DIM)
    parts = []
    for k0, k1 in zip(bounds[:-1], bounds[1:]):
        parts.append(_partial_softmax(_dot_nt(q, k_ref[0, k0:k1, :]), v_ref[0, k0:k1, :]))
    o = _merge_partials(parts)
    for g in range(group):
        o_ref[:, g * HEAD_DIM:(g + 1) * HEAD_DIM] = o[g * tq:(g + 1) * tq].astype(o_ref.dtype)


def gqa_attention(proj, *, batch, seq):
    nc, t, _ = proj.shape
    tq = 256
    group = C_HEADS // C_KV_HEADS
    nq = seq // tq
    return pl.pallas_call(
        functools.partial(_gqa_kernel, tq=tq, seq=seq, group=group),
        grid=(batch, C_KV_HEADS, nq),
        in_specs=[
            pl.BlockSpec((group, tq, LANE), lambda b, n, qi: (n, b * nq + qi, 0)),
            pl.BlockSpec((1, seq, LANE), lambda b, n, qi: (C_HEADS + n, b, 0)),
            pl.BlockSpec((1, seq, LANE), lambda b, n, qi: (C_HEADS + C_KV_HEADS + n, b, 0)),
        ],
        out_specs=pl.BlockSpec((tq, group * HEAD_DIM), lambda b, n, qi: (b * nq + qi, n)),
        out_shape=jax.ShapeDtypeStruct((t, C_HEADS * HEAD_DIM), BF16),
        compiler_params=_params("parallel", "parallel", "arbitrary"),
        name="gqa_attn",
    )(proj, proj, proj)


NA_QROWS = 4
NA_KROWS = NA_QROWS + NA_ROWS
NA_BLOCKS_PER_STEP = 4


def _na_kernel(q_ref, k_ref, v_ref, *rest, rows, nsub):
    bm_refs, o_ref = rest[:nsub], rest[nsub]
    step = pl.program_id(2)
    tq = NA_QROWS * GRID_W
    nkeys = NA_KROWS * GRID_W
    for c in range(nsub):
        blk = step * nsub + c
        wrow = jnp.clip(blk * NA_QROWS - NA_ROWS // 2, 0, rows - NA_KROWS)
        kstart = pl.multiple_of(wrow * GRID_W, NA_QROWS * GRID_W)
        kw = k_ref[0, pl.ds(kstart, nkeys), :]
        vw = v_ref[0, pl.ds(kstart, nkeys), :]
        s = _dot_nt(q_ref[0, c * tq:(c + 1) * tq, :], kw) + bm_refs[c][0, 0]
        m = jnp.max(s, axis=-1, keepdims=True)
        p = jnp.exp2(s - m)
        denom = jnp.sum(p, axis=-1, keepdims=True)
        o_ref[c * tq:(c + 1) * tq, :] = (_dot(p.astype(BF16), vw) / denom).astype(o_ref.dtype)


def _na_bias_mask(rpb, rows):
    kr = min(NA_ROWS, rows)
    nblk = rows // NA_QROWS
    nheads = rpb.shape[0]
    pad = GRID_W - NA_COLS
    col_blocks = _toeplitz(jnp.pad(LOG2E * rpb.astype(F32), ((0, 0), (0, 0), (pad, pad))), GRID_W, GRID_W)
    qj = np.arange(GRID_W)[:, None]
    kj = np.arange(GRID_W)[None, :]
    cs = np.clip(qj - NA_COLS // 2, 0, GRID_W - NA_COLS)
    col_ok = jnp.asarray((kj >= cs) & (kj < cs + NA_COLS))
    col_blocks = jnp.where(col_ok[None, None], col_blocks, NEG_INF)
    masked = jnp.full((nheads, GRID_W, GRID_W), NEG_INF, F32)
    tabs = []
    for blk in (0, 1, nblk - 1):
        wrow = int(np.clip(blk * NA_QROWS - NA_ROWS // 2, 0, rows - NA_KROWS))
        row_strips = []
        for a in range(NA_QROWS):
            qi = blk * NA_QROWS + a
            rs = int(np.clip(qi - kr // 2, 0, rows - kr))
            strip = []
            for c in range(NA_KROWS):
                ki = wrow + c
                strip.append(col_blocks[:, ki - qi + NA_ROWS - 1] if rs <= ki < rs + kr else masked)
            row_strips.append(jnp.concatenate(strip, axis=-1))
        tabs.append(jnp.concatenate(row_strips, axis=-2))
    return jnp.stack(tabs)


def na_attention(proj, rpb, *, chunk0, batch, seq):
    nc, t, _ = proj.shape
    rows = seq // GRID_W
    nsub = NA_BLOCKS_PER_STEP
    assert rows % (NA_QROWS * nsub) == 0 and rows >= NA_KROWS and rows // NA_QROWS >= 3
    nblk = rows // NA_QROWS
    nstep = nblk // nsub
    tq = NA_QROWS * GRID_W
    nkeys = NA_KROWS * GRID_W
    bm = _na_bias_mask(rpb, rows)

    def bm_spec(c):
        def index(b, h, step):
            blk = step * nsub + c
            return (jnp.where(blk == 0, 0, jnp.where(blk == nblk - 1, 2, 1)), h, 0, 0)
        return pl.BlockSpec((1, 1, tq, nkeys), index)

    return pl.pallas_call(
        functools.partial(_na_kernel, rows=rows, nsub=nsub),
        grid=(batch, D_HEADS, nstep),
        in_specs=[
            pl.BlockSpec((1, nsub * tq, LANE), lambda b, h, step: (chunk0 + h, b * nstep + step, 0)),
            pl.BlockSpec((1, seq, LANE), lambda b, h, step: (chunk0 + D_HEADS + h, b, 0)),
            pl.BlockSpec((1, seq, LANE), lambda b, h, step: (chunk0 + 2 * D_HEADS + h, b, 0)),
        ] + [bm_spec(c) for c in range(nsub)],
        out_specs=pl.BlockSpec((nsub * tq, LANE), lambda b, h, step: (b * nstep + step, h)),
        out_shape=jax.ShapeDtypeStruct((t, D_HEADS * HEAD_DIM), BF16),
        compiler_params=_params("parallel", "parallel", "arbitrary"),
        name="na_attn",
    )(proj, proj, proj, *([bm] * nsub))


def _out_proj_kernel(a_ref, b_ref, w_ref, h_ref, out_ref, *, na):
    out_ref[...] = h_ref[...] + _dot(a_ref[...], w_ref[:na, :]) + _dot(b_ref[...], w_ref[na:, :])


def out_proj(a, b, w, h, *, tm, tn):
    t, d = h.shape
    na, nb = a.shape[1], b.shape[1]
    w = w.astype(BF16)
    return pl.pallas_call(
        functools.partial(_out_proj_kernel, na=na),
        grid=(t // tm, d // tn),
        in_specs=[pl.BlockSpec((tm, na), lambda i, j: (i, 0)),
                  pl.BlockSpec((tm, nb), lambda i, j: (i, 0)),
                  pl.BlockSpec((na + nb, tn), lambda i, j: (0, j)),
                  pl.BlockSpec((tm, tn), lambda i, j: (i, j))],
        out_specs=pl.BlockSpec((tm, tn), lambda i, j: (i, j)),
        out_shape=jax.ShapeDtypeStruct((t, d), F32),
        compiler_params=_params("parallel", "arbitrary"),
        name="out_proj",
    )(a, b, w, h)


FFN_HALO = 16


def _gelu_tanh(x):
    return 0.5 * x * (1.0 + jnp.tanh(math.sqrt(2.0 / math.pi) * (x + 0.044715 * (x * x * x))))


def _ffn_kernel(x_ref, xp_ref, xnx_ref, g_ref, wg_ref, wu_ref, cw_ref, cb_ref, wd_ref, gf_ref, out_ref,
                xn_ref, *, tm, tiles_per_seq, final_norm):
    i = pl.program_id(0)
    f = pl.program_id(1)
    nf = pl.num_programs(1)
    hl = FFN_HALO

    @pl.when(f == 0)
    def _():
        gain = g_ref[...]
        keep_prev = jnp.where(i % tiles_per_seq != 0, 1.0, 0.0)
        keep_next = jnp.where(i % tiles_per_seq != tiles_per_seq - 1, 1.0, 0.0)
        xn_ref[:hl, :] = (_rms(xp_ref[...], gain) * keep_prev).astype(BF16)
        xn_ref[hl:hl + tm, :] = _rms(x_ref[...], gain).astype(BF16)
        xn_ref[hl + tm:, :] = (_rms(xnx_ref[...], gain) * keep_next).astype(BF16)
        out_ref[...] = x_ref[...]

    ge = _dot(xn_ref[...], wg_ref[0])
    cw = cw_ref[...]
    g = (cw[0:1, :] * ge[hl - 1:hl - 1 + tm] + cw[1:2, :] * ge[hl:hl + tm]
         + cw[2:3, :] * ge[hl + 1:hl + 1 + tm] + cb_ref[...])
    u = _dot(xn_ref[hl:hl + tm, :], wu_ref[0])
    act = (_gelu_tanh(g) * u).astype(BF16)
    out_ref[...] += _dot(act, wd_ref[...])

    if final_norm:
        @pl.when(f == nf - 1)
        def _():
            out_ref[...] = _rms(out_ref[...], gf_ref[...])


def conv_ffn_block(h, gain, w_up, conv_w, conv_b, w_down, final_gain, *, seq, tm, tf, final_norm):
    t, d = h.shape
    ff = w_down.shape[0]
    assert t % tm == 0 and seq % tm == 0 and ff % tf == 0 and tm % FFN_HALO == 0
    nfb = ff // tf
    w_up = _column_tiles(w_up, tf)
    w_down = w_down.astype(BF16)
    hb = tm // FFN_HALO
    last_halo = t // FFN_HALO - 1
    return pl.pallas_call(
        functools.partial(_ffn_kernel, tm=tm, tiles_per_seq=seq // tm, final_norm=final_norm),
        grid=(t // tm, nfb),
        in_specs=[
            pl.BlockSpec((tm, d), lambda i, f: (i, 0), pipeline_mode=pl.Buffered(1)),
            pl.BlockSpec((FFN_HALO, d), lambda i, f: (jnp.maximum(i * hb - 1, 0), 0)),
            pl.BlockSpec((FFN_HALO, d), lambda i, f: (jnp.minimum((i + 1) * hb, last_halo), 0)),
            pl.BlockSpec((1, d), lambda i, f: (0, 0)),
            pl.BlockSpec((1, d, tf), lambda i, f: (f, 0, 0)),
            pl.BlockSpec((1, d, tf), lambda i, f: (nfb + f, 0, 0)),
            pl.BlockSpec((3, tf), lambda i, f: (0, f)),
            pl.BlockSpec((1, tf), lambda i, f: (0, f)),
            pl.BlockSpec((tf, d), lambda i, f: (f, 0)),
            pl.BlockSpec((1, d), lambda i, f: (0, 0)),
        ],
        out_specs=pl.BlockSpec((tm, d), lambda i, f: (i, 0)),
        out_shape=jax.ShapeDtypeStruct((t, d), F32),
        scratch_shapes=[pltpu.VMEM((tm + 2 * FFN_HALO, d), BF16)],
        compiler_params=_params("parallel", "arbitrary"),
        name="conv_ffn",
    )(h, h, h, gain.reshape(1, d), w_up, w_up, conv_w, conv_b.reshape(1, ff), w_down,
      final_gain.reshape(1, d))


def _rope_tables(seq):
    tpos = jnp.arange(seq, dtype=jnp.int32)
    row = (tpos // GRID_W).astype(F32)
    col = (tpos % GRID_W).astype(F32)
    inv_freq = ROPE_THETA ** (-(jnp.arange(0, ROPE_AXIS_DIM, 2, dtype=F32) / ROPE_AXIS_DIM))
    ang = jnp.concatenate([row[:, None] * inv_freq[None], col[:, None] * inv_freq[None]], axis=-1)
    cos, sin = jnp.cos(ang), jnp.sin(ang)
    cos_dup = jnp.repeat(cos, 2, axis=-1)
    sin_signed = jnp.stack([-sin, sin], axis=-1).reshape(seq, HEAD_DIM)
    return cos_dup, sin_signed


def kernel(x, ln_mix, ln_ffn, ln_final, t5_table, ev_w_in, ev_w_out, diff_lq1, diff_lk1, diff_lq2,
           diff_lk2, diff_subln, od_w_in, od_w_out, gqa_q_norm, gqa_k_norm, na_rpb, ffn_w_up,
           ffn_conv_w, ffn_conv_b, ffn_w_down):
    batch, seq, d = x.shape
    depth = ln_mix.shape[0]
    t = batch * seq
    h = x.reshape(t, d)
    proj_tm, even_tn, odd_tn = 1024, A_HEADS * HEAD_DIM, 12 * LANE
    out_tiles = dict(tm=512, tn=d)
    ffn_tiles = dict(tm=1024, tf=512)

    for layer in range(depth):
        if layer % 2 == 0:
            e = layer // 2
            lambda_init = 0.8 - 0.6 * math.exp(-0.3 * layer)
            nat, *dil_projs = even_norm_proj(h, ln_mix[layer], ev_w_in[e],
                                             batch=batch, seq=seq, tm=proj_tm, tn=even_tn)
            oa = dilated_attention(nat, dil_projs, t5_table, batch=batch, seq=seq)
            ob = diff_attention(nat, t5_table[:, A_GROUPS * A_HEADS:], diff_lq1[e], diff_lk1[e],
                                diff_lq2[e], diff_lk2[e], diff_subln[e], chunk0=3 * A_HEADS,
                                batch=batch, seq=seq, lambda_init=lambda_init)
            h = out_proj(oa, ob, ev_w_out[e], h, **out_tiles)
        else:
            o = layer // 2
            n_rope = C_HEADS + C_KV_HEADS
            head_gain = jnp.concatenate([jnp.tile(gqa_q_norm[o][None], (C_HEADS, 1)),
                                         jnp.tile(gqa_k_norm[o][None], (C_KV_HEADS, 1))], axis=0)
            cos_dup, sin_signed = _rope_tables(seq)
            na_chunk0 = n_rope + C_KV_HEADS
            query_chunks = list(range(C_HEADS)) + list(range(na_chunk0, na_chunk0 + D_HEADS))
            proj = odd_norm_proj(h, ln_mix[layer], od_w_in[o], head_gain.astype(F32),
                                 cos_dup, sin_signed, query_chunks=query_chunks, tm=proj_tm, tn=odd_tn)
            oc = gqa_attention(proj, batch=batch, seq=seq)
            od = na_attention(proj, na_rpb[o], chunk0=na_chunk0, batch=batch, seq=seq)
            h = out_proj(oc, od, od_w_out[o], h, **out_tiles)
        h = conv_ffn_block(h, ln_ffn[layer], ffn_w_up[layer], ffn_conv_w[layer],
                           ffn_conv_b[layer], ffn_w_down[layer], ln_final,
                           seq=seq, final_norm=(layer == depth - 1), **ffn_tiles)
    return h.reshape(batch, seq, d)
```

```python
import functools
import math

import numpy as np
import jax
import jax.numpy as jnp
from jax import lax
from jax.experimental import pallas as pl
from jax.experimental.pallas import tpu as pltpu

HEAD_DIM = 128
GRID_W = 64
RMS_EPS = 1e-6
NEG_INF = -1e30
T5_BUCKETS = 32
T5_MAX_DIST = 1024
DILATED_PATTERNS = ((128, 1), (512, 4), (2048, 16))
A_GROUPS = len(DILATED_PATTERNS)
A_HEADS = 8
B_HEADS = 4
B_QK_DIM = 128
B_V_DIM = 2 * B_QK_DIM
C_HEADS = 8
C_KV_HEADS = 2
ROPE_THETA = 10000.0
ROPE_AXIS_DIM = HEAD_DIM // 2
D_HEADS = 8
NA_ROWS = 8
NA_COLS = 16

LANE = 128
V7X_VMEM_LIMIT = 56 * 1024 * 1024

BF16 = jnp.bfloat16
F32 = jnp.float32

LOG2E = math.log2(math.e)
QUERY_SCALE = HEAD_DIM ** -0.5 * LOG2E


def _params(*sem):
    return pltpu.CompilerParams(dimension_semantics=sem, vmem_limit_bytes=V7X_VMEM_LIMIT)


def _rms(xf, gain):
    ms = jnp.mean(xf * xf, axis=-1, keepdims=True)
    return xf * lax.rsqrt(ms + RMS_EPS) * gain


def _dot_nt(a, b):
    return lax.dot_general(a, b, (((1,), (1,)), ((), ())), preferred_element_type=F32)


def _dot(a, b):
    return jnp.dot(a, b, preferred_element_type=F32)


def _circulant(row, rows):
    return pltpu.roll(jnp.broadcast_to(row, (rows, row.shape[1])), 0, 1, stride=1, stride_axis=0)


def _circulant_positions(cols, rows, width):
    assert width >= rows + cols - 1
    p = np.arange(width)
    x = np.where(p < cols, p, p - width)
    return x, (p < cols) | (p > width - rows)


def _lanes(x, width):
    return x if width == LANE else jnp.concatenate([x] * (width // LANE), axis=1)


def _t5_bucket(rel):
    nb = T5_BUCKETS // 2
    max_exact = nb // 2
    ret = jnp.where(rel > 0, nb, 0)
    n = jnp.abs(rel)
    n_f = jnp.maximum(n, 1).astype(F32)
    large = max_exact + (jnp.log(n_f / max_exact) / math.log(T5_MAX_DIST / max_exact)
                         * (nb - max_exact)).astype(jnp.int32)
    large = jnp.minimum(large, nb - 1)
    return ret + jnp.where(n < max_exact, n, large)


def _store_chunks(acc, o_ref, n_chunks):
    for c in range(n_chunks):
        o_ref[c] = acc[:, c * LANE:(c + 1) * LANE].astype(o_ref.dtype)


def _even_proj_kernel(x_ref, g_ref, w_ref, nat_ref, *rest, plan, n_chunks, tm, qkv_period):
    dil_refs, (xn_ref, slab_ref) = rest[:-2], rest[-2:]
    j = pl.program_id(1)

    @pl.when(j == 0)
    def _():
        xn_ref[...] = _rms(x_ref[...], g_ref[...]).astype(BF16)

    factor = jnp.where(j % qkv_period == 0, QUERY_SCALE, 1.0)

    for lo, hi, dilation, out_idx in plan:
        @pl.when((j >= lo) & (j < hi))
        def _(dilation=dilation, out_idx=out_idx):
            acc = _dot(xn_ref[...], w_ref[...]) * factor
            if dilation == 1:
                _store_chunks(acc, nat_ref, n_chunks)
                return
            o_ref = dil_refs[out_idx]
            per = tm // dilation
            for c in range(n_chunks):
                slab_ref[c] = acc[:, c * LANE:(c + 1) * LANE]
            for c in range(n_chunks):
                for r in range(dilation):
                    o_ref[c, 0, r] = slab_ref[c, pl.ds(r, per, stride=dilation), :].astype(o_ref.dtype)


def even_norm_proj(x, gain, w, *, batch, seq, tm, tn):
    t, d = x.shape
    n = w.shape[1]
    w = w.astype(BF16)
    n_chunks = tn // LANE
    group_cols = 3 * A_HEADS * HEAD_DIM
    assert group_cols % tn == 0 and n % tn == 0 and seq % tm == 0 and t == batch * seq
    assert tn == A_HEADS * HEAD_DIM == B_HEADS * 2 * B_QK_DIM and B_QK_DIM == HEAD_DIM
    gt = group_cols // tn
    nj = n // tn
    tiles_per_seq = seq // tm
    dils = [dil for _, dil in DILATED_PATTERNS]
    assert dils[0] == 1 and all(tm % (16 * dil) == 0 for dil in dils)
    plan = [(0, gt, 1, None)]
    for gi in range(1, A_GROUPS):
        plan.append((gi * gt, (gi + 1) * gt, dils[gi], gi - 1))
    plan.append((A_GROUPS * gt, nj, 1, None))
    nat_tiles = gt + (nj - A_GROUPS * gt)

    def nat_map(i, j):
        return (jnp.where(j < gt, j, jnp.where(j < A_GROUPS * gt, gt - 1, j - (A_GROUPS - 1) * gt)), i, 0)

    def dil_map(gi):
        return lambda i, j: (jnp.clip(j - gi * gt, 0, gt - 1), i // tiles_per_seq, 0, i % tiles_per_seq, 0)

    out_specs = [pl.BlockSpec((n_chunks, tm, LANE), nat_map)]
    out_shape = [jax.ShapeDtypeStruct((nat_tiles * n_chunks, t, LANE), BF16)]
    for gi in range(1, A_GROUPS):
        dil = dils[gi]
        out_specs.append(pl.BlockSpec((n_chunks, 1, dil, tm // dil, LANE), dil_map(gi)))
        out_shape.append(jax.ShapeDtypeStruct((gt * n_chunks, batch, dil, seq // dil, LANE), BF16))
    return pl.pallas_call(
        functools.partial(_even_proj_kernel, plan=tuple(plan), n_chunks=n_chunks, tm=tm, qkv_period=gt),
        grid=(t // tm, nj),
        in_specs=[
            pl.BlockSpec((tm, d), lambda i, j: (i, 0)),
            pl.BlockSpec((1, d), lambda i, j: (0, 0)),
            pl.BlockSpec((d, tn), lambda i, j: (0, j)),
        ],
        out_specs=out_specs,
        out_shape=out_shape,
        scratch_shapes=[pltpu.VMEM((tm, d), BF16), pltpu.VMEM((n_chunks, tm, LANE), F32)],
        compiler_params=_params("parallel", "arbitrary"),
        name="even_norm_proj",
    )(x, gain.reshape(1, d), w)


def _rope_chunk(y, hg, cosd, sins, even_lane):
    y = _rms(y, hg)
    partner = jnp.where(even_lane, pltpu.roll(y, LANE - 1, axis=1), pltpu.roll(y, 1, axis=1))
    return y * cosd + partner * sins


def _odd_proj_kernel(x_ref, g_ref, w_ref, hg_ref, cos_ref, sin_ref, o_ref, xn_ref, *,
                     n_chunks, n_rope, n_tiles, query_chunks):
    j = pl.program_id(1)

    @pl.when(j == 0)
    def _():
        xn_ref[...] = _rms(x_ref[...], g_ref[...]).astype(BF16)

    acc = _dot(xn_ref[...], w_ref[...])

    for jt in range(n_tiles):
        @pl.when(j == jt)
        def _(jt=jt):
            if jt * n_chunks < n_rope:
                cosd = cos_ref[...]
                sins = sin_ref[...]
                even_lane = (lax.broadcasted_iota(jnp.int32, cosd.shape, 1) % 2) == 0
            for c in range(n_chunks):
                gc = jt * n_chunks + c
                y = acc[:, c * LANE:(c + 1) * LANE]
                if gc < n_rope:
                    y = _rope_chunk(y, hg_ref[gc:gc + 1, :], cosd, sins, even_lane)
                if gc in query_chunks:
                    y = y * QUERY_SCALE
                o_ref[c] = y.astype(o_ref.dtype)


def odd_norm_proj(x, gain, w, head_gain, cos_dup, sin_signed, *, query_chunks, tm, tn):
    t, d = x.shape
    n = w.shape[1]
    assert t % tm == 0 and n % tn == 0 and tn % LANE == 0
    w = w.astype(BF16)
    n_chunks = tn // LANE
    n_rope = head_gain.shape[0]
    s = cos_dup.shape[0]
    assert s % tm == 0
    sb = s // tm
    return pl.pallas_call(
        functools.partial(_odd_proj_kernel, n_chunks=n_chunks, n_rope=n_rope, n_tiles=n // tn,
                          query_chunks=frozenset(query_chunks)),
        grid=(t // tm, n // tn),
        in_specs=[
            pl.BlockSpec((tm, d), lambda i, j: (i, 0)),
            pl.BlockSpec((1, d), lambda i, j: (0, 0)),
            pl.BlockSpec((d, tn), lambda i, j: (0, j)),
            pl.BlockSpec((n_rope, LANE), lambda i, j: (0, 0)),
            pl.BlockSpec((tm, LANE), lambda i, j: (i % sb, 0)),
            pl.BlockSpec((tm, LANE), lambda i, j: (i % sb, 0)),
        ],
        out_specs=pl.BlockSpec((n_chunks, tm, LANE), lambda i, j: (j, i, 0)),
        out_shape=jax.ShapeDtypeStruct((n // LANE, t, LANE), BF16),
        scratch_shapes=[pltpu.VMEM((tm, d), BF16)],
        compiler_params=_params("parallel", "arbitrary"),
        name="odd_norm_proj",
    )(x, gain.reshape(1, d), w, head_gain, cos_dup, sin_signed)


A_TQ = 256
A_BLOCKS_PER_ITER = 8


def _a_geometry(seq, window, dilation):
    sub = seq // dilation
    half = window // (2 * dilation)
    tq = min(A_TQ, sub)
    win = min(tq + 2 * half, sub)
    assert seq % dilation == 0 and sub % tq == 0 and half % 16 == 0 and win % LANE == 0
    assert win == sub or win == tq + 2 * half
    return sub, half, tq, win


def _dilated_kernel(*refs, geoms):
    ng = len(geoms)
    qkv = refs[:3 * ng]
    bias_rows = refs[3 * ng:4 * ng]
    o_ref = refs[4 * ng]
    m_s, l_s, acc_s = refs[4 * ng + 1:4 * ng + 4]
    bms = refs[4 * ng + 4:]

    for g in range(ng):
        _, _, _, tq, win = geoms[g]
        for variant in range(bms[g].shape[0]):
            bms[g][variant] = _circulant(bias_rows[g][variant, 0], tq)[:, :win]

    order = sorted(range(ng), key=lambda g: -geoms[g][0])
    for pos, g in enumerate(order):
        dilation, sub, half, tq, win = geoms[g]
        q_ref, k_ref, v_ref = qkv[3 * g:3 * g + 3]
        bm_ref = bms[g]
        nq = sub // tq
        first = pos == 0
        nblocks = dilation * nq
        batch = math.gcd(A_BLOCKS_PER_ITER, nblocks)

        def one_block(it, dilation=dilation, sub=sub, half=half, tq=tq, win=win, nq=nq,
                      q_ref=q_ref, k_ref=k_ref, v_ref=v_ref, bm_ref=bm_ref, first=first):
            r = it // nq
            qi = it % nq
            q0 = pl.multiple_of(qi * tq, tq)
            kstart = pl.multiple_of(jnp.clip(q0 - half, 0, sub - win), half)
            if dilation == 1:
                q = q_ref[0, pl.ds(q0, tq), :]
                kw = k_ref[0, pl.ds(kstart, win), :]
                vw = v_ref[0, pl.ds(kstart, win), :]
                rows = pl.ds(q0, tq)
            else:
                q = q_ref[0, 0, r, pl.ds(q0, tq), :]
                kw = k_ref[0, 0, r, pl.ds(kstart, win), :]
                vw = v_ref[0, 0, r, pl.ds(kstart, win), :]
                rows = pl.ds(q0 * dilation + r, tq, stride=dilation)
            variant = 0 if nq == 1 else jnp.where(qi == 0, 0, jnp.where(qi == nq - 1, 2, 1))
            s = _dot_nt(q, kw) + bm_ref[variant]
            mb = jnp.max(s, axis=-1, keepdims=True)
            if first:
                p = jnp.exp2(s - mb)
                return (rows, jnp.broadcast_to(mb, (tq, LANE)),
                        jnp.broadcast_to(jnp.sum(p, axis=-1, keepdims=True), (tq, LANE)),
                        _dot(p.astype(BF16), vw))
            m_old = m_s[rows, :]
            m_new = jnp.maximum(m_old, mb)
            alpha = jnp.exp2(m_old - m_new)
            p = jnp.exp2(s - _lanes(m_new, win))
            return (rows, m_new, alpha * l_s[rows, :] + jnp.sum(p, axis=-1, keepdims=True),
                    alpha * acc_s[rows, :] + _dot(p.astype(BF16), vw))

        def blocks(step, carry, batch=batch, one_block=one_block):
            results = [one_block(step * batch + i) for i in range(batch)]
            for rows, m_new, l_new, acc_new in results:
                m_s[rows, :] = m_new
                l_s[rows, :] = l_new
                acc_s[rows, :] = acc_new
            return carry

        lax.fori_loop(0, nblocks // batch, blocks, 0)

    o_ref[...] = (acc_s[...] / l_s[...]).astype(o_ref.dtype)


def _dilated_bias_rows(t5_cols, dilation, half, tq, win, sub):
    nq = sub // tq
    offs = [0] if nq == 1 else [0, -half, tq - win]
    width = -(-(tq + win - 1) // LANE) * LANE
    x, used = _circulant_positions(win, tq, width)
    rows = []
    for off in offs:
        rel = x + off
        ok = jnp.asarray(used & (np.abs(rel) <= half))[None]
        bias = LOG2E * t5_cols[_t5_bucket(jnp.asarray(dilation * rel, jnp.int32))].T.astype(F32)
        rows.append(jnp.where(ok, bias, NEG_INF))
    return jnp.stack(rows)[:, :, None, :]


def dilated_attention(nat, dil_projs, t5_table, *, batch, seq):
    t = batch * seq
    geoms, args, in_specs, bias_rows, tables = [], [], [], [], []
    for gi, (window, dilation) in enumerate(DILATED_PATTERNS):
        sub, half, tq, win = _a_geometry(seq, window, dilation)
        geoms.append((dilation, sub, half, tq, win))
        for part in range(3):
            if dilation == 1:
                args.append(nat)
                in_specs.append(pl.BlockSpec(
                    (1, seq, LANE), lambda b, h, part=part: (part * A_HEADS + h, b, 0)))
            else:
                args.append(dil_projs[gi - 1])
                in_specs.append(pl.BlockSpec(
                    (1, 1, dilation, sub, LANE),
                    lambda b, h, part=part: (part * A_HEADS + h, b, 0, 0, 0)))
        bias_rows.append(_dilated_bias_rows(t5_table[:, gi * A_HEADS:(gi + 1) * A_HEADS],
                                            dilation, half, tq, win, sub))
        tables.append(pltpu.VMEM((bias_rows[-1].shape[0], tq, win), F32))
    for rows in bias_rows:
        args.append(rows)
        in_specs.append(pl.BlockSpec((rows.shape[0], 1) + rows.shape[2:], lambda b, h: (0, h, 0, 0)))
    return pl.pallas_call(
        functools.partial(_dilated_kernel, geoms=tuple(geoms)),
        grid=(batch, A_HEADS),
        in_specs=in_specs,
        out_specs=pl.BlockSpec((seq, LANE), lambda b, h: (b, h)),
        out_shape=jax.ShapeDtypeStruct((t, A_HEADS * LANE), BF16),
        scratch_shapes=[pltpu.VMEM((seq, LANE), F32)] * 3 + tables,
        compiler_params=_params("parallel", "arbitrary"),
        name="dilated_attn",
    )(*args)


KEY_SPLITS = 4


def _partial_softmax(s, v):
    m = jnp.max(s, axis=-1, keepdims=True)
    p = jnp.exp2(s - m)
    return m, jnp.sum(p, axis=-1, keepdims=True), _dot(p.astype(BF16), v)


def _merge_partials(parts):
    m = parts[0][0]
    for pm, _, _ in parts[1:]:
        m = jnp.maximum(m, pm)
    weights = [jnp.exp2(pm - m) for pm, _, _ in parts]
    denom = sum(w * pl_ for w, (_, pl_, _) in zip(weights, parts))
    acc = sum(w * pa for w, (_, _, pa) in zip(weights, parts))
    return acc / denom


T5_BAND = T5_MAX_DIST


def _diff_band(tq):
    lo = -((T5_BAND + LANE - 1) // LANE) - 1
    hi = (T5_BAND + tq - 1 + LANE - 1) // LANE
    return lo, hi


def _diff_kernel(q_ref, k_ref, v_ref, row_ref, lq1_ref, lk1_ref, lq2_ref, lk2_ref, sub_ref, o_ref,
                 tz_ref, *, tq, seq, lambda_init):
    qi = pl.program_id(2)
    lo, hi = _diff_band(tq)
    q_chunk0 = qi * (tq // LANE)
    span = seq // KEY_SPLITS

    @pl.when(qi == 0)
    def _():
        band = _circulant(row_ref[0], tq)
        for j in range(hi - lo + 1):
            tz_ref[j] = band[:, j * LANE:(j + 1) * LANE]

    def attend(mi):
        parts = []
        for part in range(KEY_SPLITS):
            k0 = part * span
            bias = jnp.concatenate(
                [tz_ref[jnp.clip(cc - q_chunk0, lo, hi) - lo]
                 for cc in range(k0 // LANE, (k0 + span) // LANE)], axis=1)
            s = _dot_nt(q_ref[mi], k_ref[mi, k0:k0 + span, :]) + bias
            vc = jnp.concatenate([v_ref[0, k0:k0 + span, :], v_ref[1, k0:k0 + span, :]], axis=1)
            parts.append(_partial_softmax(s, vc))
        return _merge_partials(parts)

    lam = (jnp.exp(jnp.sum(lq1_ref[...] * lk1_ref[...], axis=-1, keepdims=True))
           - jnp.exp(jnp.sum(lq2_ref[...] * lk2_ref[...], axis=-1, keepdims=True)) + lambda_init)
    o = attend(0) - lam * attend(1)
    o_ref[...] = (_rms(o, sub_ref[...]) * (1.0 - lambda_init)).astype(o_ref.dtype)


def _diff_bias_rows(t5_cols, tq):
    lo, hi = _diff_band(tq)
    cols = (hi - lo + 1) * LANE
    width = -(-(tq + cols - 1) // LANE) * LANE
    x, _ = _circulant_positions(cols, tq, width)
    rel = jnp.asarray(x + LANE * lo, jnp.int32)
    return (LOG2E * t5_cols[_t5_bucket(rel)].T.astype(F32))[:, None, :]


def diff_attention(proj, t5_cols, lq1, lk1, lq2, lk2, subln, *, chunk0, batch, seq, lambda_init):
    nc, t, _ = proj.shape
    tq = 256
    assert seq % tq == 0 and seq % (KEY_SPLITS * LANE) == 0 and chunk0 % 2 == 0
    nq = seq // tq
    bias_rows = _diff_bias_rows(t5_cols, tq)
    lo, hi = _diff_band(tq)
    qb, kb, vb = chunk0 // 2, chunk0 // 2 + B_HEADS, chunk0 // 2 + 2 * B_HEADS
    vec = lambda a: a.reshape(1, -1).astype(F32)
    vspec = lambda n: pl.BlockSpec((1, n), lambda b, h, qi: (0, 0))
    return pl.pallas_call(
        functools.partial(_diff_kernel, tq=tq, seq=seq, lambda_init=lambda_init),
        grid=(batch, B_HEADS, nq),
        in_specs=[
            pl.BlockSpec((2, tq, LANE), lambda b, h, qi: (qb + h, b * nq + qi, 0)),
            pl.BlockSpec((2, seq, LANE), lambda b, h, qi: (kb + h, b, 0)),
            pl.BlockSpec((2, seq, LANE), lambda b, h, qi: (vb + h, b, 0)),
            pl.BlockSpec((1, 1, bias_rows.shape[2]), lambda b, h, qi: (h, 0, 0)),
            vspec(B_QK_DIM), vspec(B_QK_DIM), vspec(B_QK_DIM), vspec(B_QK_DIM), vspec(B_V_DIM),
        ],
        out_specs=pl.BlockSpec((tq, B_V_DIM), lambda b, h, qi: (b * nq + qi, h)),
        out_shape=jax.ShapeDtypeStruct((t, B_HEADS * B_V_DIM), BF16),
        scratch_shapes=[pltpu.VMEM((hi - lo + 1, tq, LANE), F32)],
        compiler_params=_params("parallel", "parallel", "arbitrary"),
        name="diff_attn",
    )(proj, proj, proj, bias_rows, vec(lq1), vec(lk1), vec(lq2), vec(lk2), vec(subln))


def _gqa_kernel(q_ref, k_ref, v_ref, o_ref, *, tq, seq, group):
    span = seq // KEY_SPLITS
    bounds = [0] + [span // 2 + i * span for i in range(KEY_SPLITS)] + [seq]
    q = q_ref[...].reshape(group * tq, HEAD_DIM)
    parts = []
    for k0, k1 in zip(bounds[:-1], bounds[1:]):
        parts.append(_partial_softmax(_dot_nt(q, k_ref[0, k0:k1, :]), v_ref[0, k0:k1, :]))
    o = _merge_partials(parts)
    for g in range(group):
        o_ref[:, g * HEAD_DIM:(g + 1) * HEAD_DIM] = o[g * tq:(g + 1) * tq].astype(o_ref.dtype)


def gqa_attention(proj, *, batch, seq):
    nc, t, _ = proj.shape
    tq = 256
    group = C_HEADS // C_KV_HEADS
    nq = seq // tq
    return pl.pallas_call(
        functools.partial(_gqa_kernel, tq=tq, seq=seq, group=group),
        grid=(batch, C_KV_HEADS, nq),
        in_specs=[
            pl.BlockSpec((group, tq, LANE), lambda b, n, qi: (n, b * nq + qi, 0)),
            pl.BlockSpec((1, seq, LANE), lambda b, n, qi: (C_HEADS + n, b, 0)),
            pl.BlockSpec((1, seq, LANE), lambda b, n, qi: (C_HEADS + C_KV_HEADS + n, b, 0)),
        ],
        out_specs=pl.BlockSpec((tq, group * HEAD_DIM), lambda b, n, qi: (b * nq + qi, n)),
        out_shape=jax.ShapeDtypeStruct((t, C_HEADS * HEAD_DIM), BF16),
        compiler_params=_params("parallel", "parallel", "arbitrary"),
        name="gqa_attn",
    )(proj, proj, proj)


NA_QROWS = 4
NA_KROWS = NA_QROWS + NA_ROWS
NA_BLOCKS_PER_STEP = 4


def _na_kernel(q_ref, k_ref, v_ref, rows_ref, o_ref, bm_ref, *, rows, nsub):
    step = pl.program_id(2)
    tq = NA_QROWS * GRID_W
    nkeys = NA_KROWS * GRID_W
    nblk = rows // NA_QROWS
    kr = min(NA_ROWS, rows)

    @pl.when(step == 0)
    def _():
        lane = lax.broadcasted_iota(jnp.int32, (GRID_W, LANE), 1)
        qj = lax.broadcasted_iota(jnp.int32, (GRID_W, LANE), 0)
        kj = lane % GRID_W
        cs = jnp.clip(qj - NA_COLS // 2, 0, GRID_W - NA_COLS)
        col_ok = (kj >= cs) & (kj < cs + NA_COLS)
        left = lane < GRID_W
        neg = jnp.full((GRID_W, LANE), NEG_INF, F32)
        bands = {}

        def band(dr, right):
            if (dr, right) not in bands:
                wide = jnp.broadcast_to(rows_ref[0, dr], (GRID_W, LANE))
                rolled = pltpu.roll(wide, GRID_W if right else 0, 1, stride=1, stride_axis=0)
                bands[dr, right] = jnp.where(col_ok, rolled, NEG_INF)
            return bands[dr, right]

        for variant, blk in enumerate((0, 1, nblk - 1)):
            wrow = min(max(blk * NA_QROWS - NA_ROWS // 2, 0), rows - NA_KROWS)
            for a in range(NA_QROWS):
                qi = blk * NA_QROWS + a
                rs = min(max(qi - kr // 2, 0), rows - kr)
                for pair in range(NA_KROWS // 2):
                    halves = []
                    for right in (False, True):
                        ki = wrow + 2 * pair + int(right)
                        halves.append(band(ki - qi + NA_ROWS - 1, right) if rs <= ki < rs + kr else neg)
                    bm_ref[variant, a * GRID_W:(a + 1) * GRID_W, pair * LANE:(pair + 1) * LANE] = (
                        jnp.where(left, halves[0], halves[1]))

    for c in range(nsub):
        blk = step * nsub + c
        wrow = jnp.clip(blk * NA_QROWS - NA_ROWS // 2, 0, rows - NA_KROWS)
        kstart = pl.multiple_of(wrow * GRID_W, NA_QROWS * GRID_W)
        kw = k_ref[0, pl.ds(kstart, nkeys), :]
        vw = v_ref[0, pl.ds(kstart, nkeys), :]
        variant = jnp.where(blk == 0, 0, jnp.where(blk == nblk - 1, 2, 1))
        s = _dot_nt(q_ref[0, c * tq:(c + 1) * tq, :], kw) + bm_ref[variant]
        m = jnp.max(s, axis=-1, keepdims=True)
        p = jnp.exp2(s - m)
        denom = jnp.sum(p, axis=-1, keepdims=True)
        o_ref[c * tq:(c + 1) * tq, :] = (_dot(p.astype(BF16), vw) / denom).astype(o_ref.dtype)


def _na_bias_rows(rpb):
    r = LOG2E * rpb.astype(F32)
    gap = jnp.zeros(r.shape[:-1] + (LANE - r.shape[-1],), F32)
    return jnp.concatenate([r[..., NA_COLS - 1:], gap, r[..., :NA_COLS - 1]], axis=-1)[:, :, None, :]


def na_attention(proj, rpb, *, chunk0, batch, seq):
    nc, t, _ = proj.shape
    rows = seq // GRID_W
    nsub = NA_BLOCKS_PER_STEP
    assert rows % (NA_QROWS * nsub) == 0 and rows >= NA_KROWS and rows // NA_QROWS >= 3
    assert NA_KROWS % 2 == 0 and 2 * GRID_W == LANE and 2 * NA_COLS - 1 <= LANE
    nblk = rows // NA_QROWS
    nstep = nblk // nsub
    tq = NA_QROWS * GRID_W
    nkeys = NA_KROWS * GRID_W
    bias_rows = _na_bias_rows(rpb)
    return pl.pallas_call(
        functools.partial(_na_kernel, rows=rows, nsub=nsub),
        grid=(batch, D_HEADS, nstep),
        in_specs=[
            pl.BlockSpec((1, nsub * tq, LANE), lambda b, h, step: (chunk0 + h, b * nstep + step, 0)),
            pl.BlockSpec((1, seq, LANE), lambda b, h, step: (chunk0 + D_HEADS + h, b, 0)),
            pl.BlockSpec((1, seq, LANE), lambda b, h, step: (chunk0 + 2 * D_HEADS + h, b, 0)),
            pl.BlockSpec((1,) + bias_rows.shape[1:], lambda b, h, step: (h, 0, 0, 0)),
        ],
        out_specs=pl.BlockSpec((nsub * tq, LANE), lambda b, h, step: (b * nstep + step, h)),
        out_shape=jax.ShapeDtypeStruct((t, D_HEADS * HEAD_DIM), BF16),
        scratch_shapes=[pltpu.VMEM((3, tq, nkeys), F32)],
        compiler_params=_params("parallel", "parallel", "arbitrary"),
        name="na_attn",
    )(proj, proj, proj, bias_rows)


def _out_proj_kernel(a_ref, b_ref, w_ref, h_ref, out_ref, *, na):
    out_ref[...] = h_ref[...] + _dot(a_ref[...], w_ref[:na, :]) + _dot(b_ref[...], w_ref[na:, :])


def out_proj(a, b, w, h, *, tm, tn):
    t, d = h.shape
    na, nb = a.shape[1], b.shape[1]
    w = w.astype(BF16)
    return pl.pallas_call(
        functools.partial(_out_proj_kernel, na=na),
        grid=(t // tm, d // tn),
        in_specs=[pl.BlockSpec((tm, na), lambda i, j: (i, 0)),
                  pl.BlockSpec((tm, nb), lambda i, j: (i, 0)),
                  pl.BlockSpec((na + nb, tn), lambda i, j: (0, j)),
                  pl.BlockSpec((tm, tn), lambda i, j: (i, j))],
        out_specs=pl.BlockSpec((tm, tn), lambda i, j: (i, j)),
        out_shape=jax.ShapeDtypeStruct((t, d), F32),
        compiler_params=_params("parallel", "arbitrary"),
        name="out_proj",
    )(a, b, w, h)


FFN_HALO = 16


def _gelu_tanh(x):
    return 0.5 * x * (1.0 + jnp.tanh(math.sqrt(2.0 / math.pi) * (x + 0.044715 * (x * x * x))))


def _ffn_kernel(x_ref, xp_ref, xnx_ref, g_ref, wg_ref, wu_ref, cw_ref, cb_ref, wd_ref, gf_ref, out_ref,
                xn_ref, *, tm, tiles_per_seq, final_norm):
    i = pl.program_id(0)
    f = pl.program_id(1)
    nf = pl.num_programs(1)
    hl = FFN_HALO

    @pl.when(f == 0)
    def _():
        gain = g_ref[...]
        keep_prev = jnp.where(i % tiles_per_seq != 0, 1.0, 0.0)
        keep_next = jnp.where(i % tiles_per_seq != tiles_per_seq - 1, 1.0, 0.0)
        xn_ref[:hl, :] = (_rms(xp_ref[...], gain) * keep_prev).astype(BF16)
        xn_ref[hl:hl + tm, :] = _rms(x_ref[...], gain).astype(BF16)
        xn_ref[hl + tm:, :] = (_rms(xnx_ref[...], gain) * keep_next).astype(BF16)
        out_ref[...] = x_ref[...]

    ge = _dot(xn_ref[...], wg_ref[...])
    cw = cw_ref[...]
    g = (cw[0:1, :] * ge[hl - 1:hl - 1 + tm] + cw[1:2, :] * ge[hl:hl + tm]
         + cw[2:3, :] * ge[hl + 1:hl + 1 + tm] + cb_ref[...])
    u = _dot(xn_ref[hl:hl + tm, :], wu_ref[...])
    act = (_gelu_tanh(g) * u).astype(BF16)
    out_ref[...] += _dot(act, wd_ref[...])

    if final_norm:
        @pl.when(f == nf - 1)
        def _():
            out_ref[...] = _rms(out_ref[...], gf_ref[...])


def conv_ffn_block(h, gain, w_up, conv_w, conv_b, w_down, final_gain, *, seq, tm, tf, final_norm):
    t, d = h.shape
    ff = w_down.shape[0]
    assert t % tm == 0 and seq % tm == 0 and ff % tf == 0 and tm % FFN_HALO == 0
    nfb = ff // tf
    w_up = w_up.astype(BF16)
    w_down = w_down.astype(BF16)
    hb = tm // FFN_HALO
    last_halo = t // FFN_HALO - 1
    return pl.pallas_call(
        functools.partial(_ffn_kernel, tm=tm, tiles_per_seq=seq // tm, final_norm=final_norm),
        grid=(t // tm, nfb),
        in_specs=[
            pl.BlockSpec((tm, d), lambda i, f: (i, 0), pipeline_mode=pl.Buffered(1)),
            pl.BlockSpec((FFN_HALO, d), lambda i, f: (jnp.maximum(i * hb - 1, 0), 0)),
            pl.BlockSpec((FFN_HALO, d), lambda i, f: (jnp.minimum((i + 1) * hb, last_halo), 0)),
            pl.BlockSpec((1, d), lambda i, f: (0, 0)),
            pl.BlockSpec((d, tf), lambda i, f: (0, f)),
            pl.BlockSpec((d, tf), lambda i, f: (0, nfb + f)),
            pl.BlockSpec((3, tf), lambda i, f: (0, f)),
            pl.BlockSpec((1, tf), lambda i, f: (0, f)),
            pl.BlockSpec((tf, d), lambda i, f: (f, 0)),
            pl.BlockSpec((1, d), lambda i, f: (0, 0)),
        ],
        out_specs=pl.BlockSpec((tm, d), lambda i, f: (i, 0)),
        out_shape=jax.ShapeDtypeStruct((t, d), F32),
        scratch_shapes=[pltpu.VMEM((tm + 2 * FFN_HALO, d), BF16)],
        compiler_params=_params("parallel", "arbitrary"),
        name="conv_ffn",
    )(h, h, h, gain.reshape(1, d), w_up, w_up, conv_w, conv_b.reshape(1, ff), w_down,
      final_gain.reshape(1, d))


def _rope_tables(seq):
    tpos = jnp.arange(seq, dtype=jnp.int32)
    row = (tpos // GRID_W).astype(F32)
    col = (tpos % GRID_W).astype(F32)
    inv_freq = ROPE_THETA ** (-(jnp.arange(0, ROPE_AXIS_DIM, 2, dtype=F32) / ROPE_AXIS_DIM))
    ang = jnp.concatenate([row[:, None] * inv_freq[None], col[:, None] * inv_freq[None]], axis=-1)
    cos, sin = jnp.cos(ang), jnp.sin(ang)
    cos_dup = jnp.repeat(cos, 2, axis=-1)
    sin_signed = jnp.stack([-sin, sin], axis=-1).reshape(seq, HEAD_DIM)
    return cos_dup, sin_signed


def kernel(x, ln_mix, ln_ffn, ln_final, t5_table, ev_w_in, ev_w_out, diff_lq1, diff_lk1, diff_lq2,
           diff_lk2, diff_subln, od_w_in, od_w_out, gqa_q_norm, gqa_k_norm, na_rpb, ffn_w_up,
           ffn_conv_w, ffn_conv_b, ffn_w_down):
    batch, seq, d = x.shape
    depth = ln_mix.shape[0]
    t = batch * seq
    h = x.reshape(t, d)
    proj_tm, even_tn, odd_tn = 1024, A_HEADS * HEAD_DIM, 12 * LANE
    out_tiles = dict(tm=512, tn=d)
    ffn_tiles = dict(tm=1024, tf=512)

    for layer in range(depth):
        if layer % 2 == 0:
            e = layer // 2
            lambda_init = 0.8 - 0.6 * math.exp(-0.3 * layer)
            nat, *dil_projs = even_norm_proj(h, ln_mix[layer], ev_w_in[e],
                                             batch=batch, seq=seq, tm=proj_tm, tn=even_tn)
            oa = dilated_attention(nat, dil_projs, t5_table, batch=batch, seq=seq)
            ob = diff_attention(nat, t5_table[:, A_GROUPS * A_HEADS:], diff_lq1[e], diff_lk1[e],
                                diff_lq2[e], diff_lk2[e], diff_subln[e], chunk0=3 * A_HEADS,
                                batch=batch, seq=seq, lambda_init=lambda_init)
            h = out_proj(oa, ob, ev_w_out[e], h, **out_tiles)
        else:
            o = layer // 2
            n_rope = C_HEADS + C_KV_HEADS
            head_gain = jnp.concatenate([jnp.tile(gqa_q_norm[o][None], (C_HEADS, 1)),
                                         jnp.tile(gqa_k_norm[o][None], (C_KV_HEADS, 1))], axis=0)
            cos_dup, sin_signed = _rope_tables(seq)
            na_chunk0 = n_rope + C_KV_HEADS
            query_chunks = list(range(C_HEADS)) + list(range(na_chunk0, na_chunk0 + D_HEADS))
            proj = odd_norm_proj(h, ln_mix[layer], od_w_in[o], head_gain.astype(F32),
                                 cos_dup, sin_signed, query_chunks=query_chunks, tm=proj_tm, tn=odd_tn)
            oc = gqa_attention(proj, batch=batch, seq=seq)
            od = na_attention(proj, na_rpb[o], chunk0=na_chunk0, batch=batch, seq=seq)
            h = out_proj(oc, od, od_w_out[o], h, **out_tiles)
        h = conv_ffn_block(h, ln_ffn[layer], ffn_w_up[layer], ffn_conv_w[layer],
                           ffn_conv_b[layer], ffn_w_down[layer], ln_final,
                           seq=seq, final_norm=(layer == depth - 1), **ffn_tiles)
    return h.reshape(batch, seq, d)
```

```python
import functools
import math

import numpy as np
import jax
import jax.numpy as jnp
from jax import lax
from jax.experimental import pallas as pl
from jax.experimental.pallas import tpu as pltpu

HEAD_DIM = 128
GRID_W = 64
RMS_EPS = 1e-6
NEG_INF = -1e30
T5_BUCKETS = 32
T5_MAX_DIST = 1024
DILATED_PATTERNS = ((128, 1), (512, 4), (2048, 16))
A_GROUPS = len(DILATED_PATTERNS)
A_HEADS = 8
B_HEADS = 4
B_QK_DIM = 128
B_V_DIM = 2 * B_QK_DIM
C_HEADS = 8
C_KV_HEADS = 2
ROPE_THETA = 10000.0
ROPE_AXIS_DIM = HEAD_DIM // 2
D_HEADS = 8
NA_ROWS = 8
NA_COLS = 16

LANE = 128
V7X_VMEM_LIMIT = 56 * 1024 * 1024

BF16 = jnp.bfloat16
F32 = jnp.float32

LOG2E = math.log2(math.e)
QUERY_SCALE = HEAD_DIM ** -0.5 * LOG2E


def _params(*sem):
    return pltpu.CompilerParams(dimension_semantics=sem, vmem_limit_bytes=V7X_VMEM_LIMIT)


def _rms(xf, gain):
    ms = jnp.mean(xf * xf, axis=-1, keepdims=True)
    return xf * lax.rsqrt(ms + RMS_EPS) * gain


def _dot_nt(a, b):
    return lax.dot_general(a, b, (((1,), (1,)), ((), ())), preferred_element_type=F32)


def _dot(a, b):
    return jnp.dot(a, b, preferred_element_type=F32)


def _circulant(row, rows):
    return pltpu.roll(jnp.broadcast_to(row, (rows, row.shape[1])), 0, 1, stride=1, stride_axis=0)


def _circulant_positions(cols, rows, width):
    assert width >= rows + cols - 1
    p = np.arange(width)
    x = np.where(p < cols, p, p - width)
    return x, (p < cols) | (p > width - rows)


def _lanes(x, width):
    return x if width == LANE else jnp.concatenate([x] * (width // LANE), axis=1)


def _t5_bucket(rel):
    nb = T5_BUCKETS // 2
    max_exact = nb // 2
    ret = jnp.where(rel > 0, nb, 0)
    n = jnp.abs(rel)
    n_f = jnp.maximum(n, 1).astype(F32)
    large = max_exact + (jnp.log(n_f / max_exact) / math.log(T5_MAX_DIST / max_exact)
                         * (nb - max_exact)).astype(jnp.int32)
    large = jnp.minimum(large, nb - 1)
    return ret + jnp.where(n < max_exact, n, large)


def _store_chunks(acc, o_ref, n_chunks):
    for c in range(n_chunks):
        o_ref[c] = acc[:, c * LANE:(c + 1) * LANE].astype(o_ref.dtype)


def _even_proj_kernel(x_ref, g_ref, w_ref, nat_ref, *rest, plan, n_chunks, tm, qkv_period):
    dil_refs, (xn_ref, slab_ref) = rest[:-2], rest[-2:]
    j = pl.program_id(1)

    @pl.when(j == 0)
    def _():
        xn_ref[...] = _rms(x_ref[...], g_ref[...]).astype(BF16)

    factor = jnp.where(j % qkv_period == 0, QUERY_SCALE, 1.0)

    for lo, hi, dilation, out_idx in plan:
        @pl.when((j >= lo) & (j < hi))
        def _(dilation=dilation, out_idx=out_idx):
            acc = _dot(xn_ref[...], w_ref[...]) * factor
            if dilation == 1:
                _store_chunks(acc, nat_ref, n_chunks)
                return
            o_ref = dil_refs[out_idx]
            per = tm // dilation
            for c in range(n_chunks):
                slab_ref[c] = acc[:, c * LANE:(c + 1) * LANE]
            for c in range(n_chunks):
                for r in range(dilation):
                    o_ref[c, 0, r] = slab_ref[c, pl.ds(r, per, stride=dilation), :].astype(o_ref.dtype)


def even_norm_proj(x, gain, w, *, batch, seq, tm, tn):
    t, d = x.shape
    n = w.shape[1]
    w = w.astype(BF16)
    n_chunks = tn // LANE
    group_cols = 3 * A_HEADS * HEAD_DIM
    assert group_cols % tn == 0 and n % tn == 0 and seq % tm == 0 and t == batch * seq
    assert tn == A_HEADS * HEAD_DIM == B_HEADS * 2 * B_QK_DIM and B_QK_DIM == HEAD_DIM
    gt = group_cols // tn
    nj = n // tn
    tiles_per_seq = seq // tm
    dils = [dil for _, dil in DILATED_PATTERNS]
    assert dils[0] == 1 and all(tm % (16 * dil) == 0 for dil in dils)
    plan = [(0, gt, 1, None)]
    for gi in range(1, A_GROUPS):
        plan.append((gi * gt, (gi + 1) * gt, dils[gi], gi - 1))
    plan.append((A_GROUPS * gt, nj, 1, None))
    nat_tiles = gt + (nj - A_GROUPS * gt)

    def nat_map(i, j):
        return (jnp.where(j < gt, j, jnp.where(j < A_GROUPS * gt, gt - 1, j - (A_GROUPS - 1) * gt)), i, 0)

    def dil_map(gi):
        return lambda i, j: (jnp.clip(j - gi * gt, 0, gt - 1), i // tiles_per_seq, 0, i % tiles_per_seq, 0)

    out_specs = [pl.BlockSpec((n_chunks, tm, LANE), nat_map)]
    out_shape = [jax.ShapeDtypeStruct((nat_tiles * n_chunks, t, LANE), BF16)]
    for gi in range(1, A_GROUPS):
        dil = dils[gi]
        out_specs.append(pl.BlockSpec((n_chunks, 1, dil, tm // dil, LANE), dil_map(gi)))
        out_shape.append(jax.ShapeDtypeStruct((gt * n_chunks, batch, dil, seq // dil, LANE), BF16))
    return pl.pallas_call(
        functools.partial(_even_proj_kernel, plan=tuple(plan), n_chunks=n_chunks, tm=tm, qkv_period=gt),
        grid=(t // tm, nj),
        in_specs=[
            pl.BlockSpec((tm, d), lambda i, j: (i, 0)),
            pl.BlockSpec((1, d), lambda i, j: (0, 0)),
            pl.BlockSpec((d, tn), lambda i, j: (0, j)),
        ],
        out_specs=out_specs,
        out_shape=out_shape,
        scratch_shapes=[pltpu.VMEM((tm, d), BF16), pltpu.VMEM((n_chunks, tm, LANE), F32)],
        compiler_params=_params("parallel", "arbitrary"),
        name="even_norm_proj",
    )(x, gain.reshape(1, d), w)


def _rope_chunk(y, hg, cosd, sins, even_lane):
    y = _rms(y, hg)
    partner = jnp.where(even_lane, pltpu.roll(y, LANE - 1, axis=1), pltpu.roll(y, 1, axis=1))
    return y * cosd + partner * sins


def _odd_proj_kernel(x_ref, g_ref, w_ref, hg_ref, cos_ref, sin_ref, o_ref, xn_ref, *,
                     n_chunks, n_rope, n_tiles, query_chunks):
    j = pl.program_id(1)

    @pl.when(j == 0)
    def _():
        xn_ref[...] = _rms(x_ref[...], g_ref[...]).astype(BF16)

    for jt in range(n_tiles):
        @pl.when(j == jt)
        def _(jt=jt):
            acc = _dot(xn_ref[...], w_ref[...])
            if jt * n_chunks < n_rope:
                cosd = cos_ref[...]
                sins = sin_ref[...]
                even_lane = (lax.broadcasted_iota(jnp.int32, cosd.shape, 1) % 2) == 0
            for c in range(n_chunks):
                gc = jt * n_chunks + c
                y = acc[:, c * LANE:(c + 1) * LANE]
                if gc < n_rope:
                    y = _rope_chunk(y, hg_ref[gc:gc + 1, :], cosd, sins, even_lane)
                if gc in query_chunks:
                    y = y * QUERY_SCALE
                o_ref[c] = y.astype(o_ref.dtype)


def odd_norm_proj(x, gain, w, head_gain, cos_dup, sin_signed, *, query_chunks, tm, tn):
    t, d = x.shape
    n = w.shape[1]
    assert t % tm == 0 and n % tn == 0 and tn % LANE == 0
    w = w.astype(BF16)
    n_chunks = tn // LANE
    n_rope = head_gain.shape[0]
    s = cos_dup.shape[0]
    assert s % tm == 0
    sb = s // tm
    return pl.pallas_call(
        functools.partial(_odd_proj_kernel, n_chunks=n_chunks, n_rope=n_rope, n_tiles=n // tn,
                          query_chunks=frozenset(query_chunks)),
        grid=(t // tm, n // tn),
        in_specs=[
            pl.BlockSpec((tm, d), lambda i, j: (i, 0)),
            pl.BlockSpec((1, d), lambda i, j: (0, 0)),
            pl.BlockSpec((d, tn), lambda i, j: (0, j)),
            pl.BlockSpec((n_rope, LANE), lambda i, j: (0, 0)),
            pl.BlockSpec((tm, LANE), lambda i, j: (i % sb, 0)),
            pl.BlockSpec((tm, LANE), lambda i, j: (i % sb, 0)),
        ],
        out_specs=pl.BlockSpec((n_chunks, tm, LANE), lambda i, j: (j, i, 0)),
        out_shape=jax.ShapeDtypeStruct((n // LANE, t, LANE), BF16),
        scratch_shapes=[pltpu.VMEM((tm, d), BF16)],
        compiler_params=_params("parallel", "arbitrary"),
        name="odd_norm_proj",
    )(x, gain.reshape(1, d), w, head_gain, cos_dup, sin_signed)


A_TQ = 256
A_BLOCKS_PER_ITER = 16


def _a_geometry(seq, window, dilation):
    sub = seq // dilation
    half = window // (2 * dilation)
    tq = min(A_TQ, sub)
    win = min(tq + 2 * half, sub)
    assert seq % dilation == 0 and sub % tq == 0 and half % 16 == 0 and win % LANE == 0
    assert win == sub or win == tq + 2 * half
    return sub, half, tq, win


def _dilated_kernel(*refs, geoms):
    ng = len(geoms)
    qkv = refs[:3 * ng]
    bias_rows = refs[3 * ng:4 * ng]
    o_ref = refs[4 * ng]
    m_s, l_s, acc_s = refs[4 * ng + 1:4 * ng + 4]
    bms = refs[4 * ng + 4:]

    for g in range(ng):
        _, _, _, tq, win = geoms[g]
        for variant in range(bms[g].shape[0]):
            bms[g][variant] = _circulant(bias_rows[g][variant, 0], tq)[:, :win]

    order = sorted(range(ng), key=lambda g: -geoms[g][0])
    for pos, g in enumerate(order):
        dilation, sub, half, tq, win = geoms[g]
        q_ref, k_ref, v_ref = qkv[3 * g:3 * g + 3]
        bm_ref = bms[g]
        nq = sub // tq
        first = pos == 0
        nblocks = dilation * nq
        batch = math.gcd(A_BLOCKS_PER_ITER, nblocks)

        def one_block(it, dilation=dilation, sub=sub, half=half, tq=tq, win=win, nq=nq,
                      q_ref=q_ref, k_ref=k_ref, v_ref=v_ref, bm_ref=bm_ref, first=first):
            r = it // nq
            qi = it % nq
            q0 = pl.multiple_of(qi * tq, tq)
            kstart = pl.multiple_of(jnp.clip(q0 - half, 0, sub - win), half)
            if dilation == 1:
                q = q_ref[0, pl.ds(q0, tq), :]
                kw = k_ref[0, pl.ds(kstart, win), :]
                vw = v_ref[0, pl.ds(kstart, win), :]
                rows = pl.ds(q0, tq)
            else:
                q = q_ref[0, 0, r, pl.ds(q0, tq), :]
                kw = k_ref[0, 0, r, pl.ds(kstart, win), :]
                vw = v_ref[0, 0, r, pl.ds(kstart, win), :]
                rows = pl.ds(q0 * dilation + r, tq, stride=dilation)
            variant = 0 if nq == 1 else jnp.where(qi == 0, 0, jnp.where(qi == nq - 1, 2, 1))
            s = _dot_nt(q, kw) + bm_ref[variant]
            mb = jnp.max(s, axis=-1, keepdims=True)
            if first:
                p = jnp.exp2(s - mb)
                return (rows, jnp.broadcast_to(mb, (tq, LANE)),
                        jnp.broadcast_to(jnp.sum(p, axis=-1, keepdims=True), (tq, LANE)),
                        _dot(p.astype(BF16), vw))
            m_old = m_s[rows, :]
            m_new = jnp.maximum(m_old, mb)
            alpha = jnp.exp2(m_old - m_new)
            p = jnp.exp2(s - _lanes(m_new, win))
            return (rows, m_new, alpha * l_s[rows, :] + jnp.sum(p, axis=-1, keepdims=True),
                    alpha * acc_s[rows, :] + _dot(p.astype(BF16), vw))

        def blocks(step, carry, batch=batch, one_block=one_block):
            results = [one_block(step * batch + i) for i in range(batch)]
            for rows, m_new, l_new, acc_new in results:
                m_s[rows, :] = m_new
                l_s[rows, :] = l_new
                acc_s[rows, :] = acc_new
            return carry

        lax.fori_loop(0, nblocks // batch, blocks, 0)

    o_ref[...] = (acc_s[...] / l_s[...]).astype(o_ref.dtype)


def _dilated_bias_rows(t5_cols, dilation, half, tq, win, sub):
    nq = sub // tq
    offs = [0] if nq == 1 else [0, -half, tq - win]
    width = -(-(tq + win - 1) // LANE) * LANE
    x, used = _circulant_positions(win, tq, width)
    rows = []
    for off in offs:
        rel = x + off
        ok = jnp.asarray(used & (np.abs(rel) <= half))[None]
        bias = LOG2E * t5_cols[_t5_bucket(jnp.asarray(dilation * rel, jnp.int32))].T.astype(F32)
        rows.append(jnp.where(ok, bias, NEG_INF))
    return jnp.stack(rows)[:, :, None, :]


def dilated_attention(nat, dil_projs, t5_table, *, batch, seq):
    t = batch * seq
    geoms, args, in_specs, bias_rows, tables = [], [], [], [], []
    for gi, (window, dilation) in enumerate(DILATED_PATTERNS):
        sub, half, tq, win = _a_geometry(seq, window, dilation)
        geoms.append((dilation, sub, half, tq, win))
        for part in range(3):
            if dilation == 1:
                args.append(nat)
                in_specs.append(pl.BlockSpec(
                    (1, seq, LANE), lambda b, h, part=part: (part * A_HEADS + h, b, 0)))
            else:
                args.append(dil_projs[gi - 1])
                in_specs.append(pl.BlockSpec(
                    (1, 1, dilation, sub, LANE),
                    lambda b, h, part=part: (part * A_HEADS + h, b, 0, 0, 0)))
        bias_rows.append(_dilated_bias_rows(t5_table[:, gi * A_HEADS:(gi + 1) * A_HEADS],
                                            dilation, half, tq, win, sub))
        tables.append(pltpu.VMEM((bias_rows[-1].shape[0], tq, win), F32))
    for rows in bias_rows:
        args.append(rows)
        in_specs.append(pl.BlockSpec((rows.shape[0], 1) + rows.shape[2:], lambda b, h: (0, h, 0, 0)))
    return pl.pallas_call(
        functools.partial(_dilated_kernel, geoms=tuple(geoms)),
        grid=(batch, A_HEADS),
        in_specs=in_specs,
        out_specs=pl.BlockSpec((seq, LANE), lambda b, h: (b, h)),
        out_shape=jax.ShapeDtypeStruct((t, A_HEADS * LANE), BF16),
        scratch_shapes=[pltpu.VMEM((seq, LANE), F32)] * 3 + tables,
        compiler_params=_params("parallel", "arbitrary"),
        name="dilated_attn",
    )(*args)


KEY_SPLITS = 4


def _partial_softmax(s, v):
    m = jnp.max(s, axis=-1, keepdims=True)
    p = jnp.exp2(s - m)
    return m, jnp.sum(p, axis=-1, keepdims=True), _dot(p.astype(BF16), v)


def _merge_partials(parts):
    m = parts[0][0]
    for pm, _, _ in parts[1:]:
        m = jnp.maximum(m, pm)
    weights = [jnp.exp2(pm - m) for pm, _, _ in parts]
    denom = sum(w * pl_ for w, (_, pl_, _) in zip(weights, parts))
    acc = sum(w * pa for w, (_, _, pa) in zip(weights, parts))
    return acc / denom


T5_BAND = T5_MAX_DIST


def _diff_band(tq):
    lo = -((T5_BAND + LANE - 1) // LANE) - 1
    hi = (T5_BAND + tq - 1 + LANE - 1) // LANE
    return lo, hi


def _diff_kernel(q_ref, k_ref, v_ref, row_ref, lq1_ref, lk1_ref, lq2_ref, lk2_ref, sub_ref, o_ref,
                 tz_ref, *, tq, seq, lambda_init):
    qi = pl.program_id(2)
    lo, hi = _diff_band(tq)
    q_chunk0 = qi * (tq // LANE)
    span = seq // KEY_SPLITS

    @pl.when(qi == 0)
    def _():
        band = _circulant(row_ref[0], tq)
        for j in range(hi - lo + 1):
            tz_ref[j] = band[:, j * LANE:(j + 1) * LANE]

    def attend(mi):
        parts = []
        for part in range(KEY_SPLITS):
            k0 = part * span
            bias = jnp.concatenate(
                [tz_ref[jnp.clip(cc - q_chunk0, lo, hi) - lo]
                 for cc in range(k0 // LANE, (k0 + span) // LANE)], axis=1)
            s = _dot_nt(q_ref[mi], k_ref[mi, k0:k0 + span, :]) + bias
            vc = jnp.concatenate([v_ref[0, k0:k0 + span, :], v_ref[1, k0:k0 + span, :]], axis=1)
            parts.append(_partial_softmax(s, vc))
        return _merge_partials(parts)

    lam = (jnp.exp(jnp.sum(lq1_ref[...] * lk1_ref[...], axis=-1, keepdims=True))
           - jnp.exp(jnp.sum(lq2_ref[...] * lk2_ref[...], axis=-1, keepdims=True)) + lambda_init)
    o = attend(0) - lam * attend(1)
    o_ref[...] = (_rms(o, sub_ref[...]) * (1.0 - lambda_init)).astype(o_ref.dtype)


def _diff_bias_rows(t5_cols, tq):
    lo, hi = _diff_band(tq)
    cols = (hi - lo + 1) * LANE
    width = -(-(tq + cols - 1) // LANE) * LANE
    x, _ = _circulant_positions(cols, tq, width)
    rel = jnp.asarray(x + LANE * lo, jnp.int32)
    return (LOG2E * t5_cols[_t5_bucket(rel)].T.astype(F32))[:, None, :]


def diff_attention(proj, t5_cols, lq1, lk1, lq2, lk2, subln, *, chunk0, batch, seq, lambda_init):
    nc, t, _ = proj.shape
    tq = 256
    assert seq % tq == 0 and seq % (KEY_SPLITS * LANE) == 0 and chunk0 % 2 == 0
    nq = seq // tq
    bias_rows = _diff_bias_rows(t5_cols, tq)
    lo, hi = _diff_band(tq)
    qb, kb, vb = chunk0 // 2, chunk0 // 2 + B_HEADS, chunk0 // 2 + 2 * B_HEADS
    vec = lambda a: a.reshape(1, -1).astype(F32)
    vspec = lambda n: pl.BlockSpec((1, n), lambda b, h, qi: (0, 0))
    return pl.pallas_call(
        functools.partial(_diff_kernel, tq=tq, seq=seq, lambda_init=lambda_init),
        grid=(batch, B_HEADS, nq),
        in_specs=[
            pl.BlockSpec((2, tq, LANE), lambda b, h, qi: (qb + h, b * nq + qi, 0)),
            pl.BlockSpec((2, seq, LANE), lambda b, h, qi: (kb + h, b, 0)),
            pl.BlockSpec((2, seq, LANE), lambda b, h, qi: (vb + h, b, 0)),
            pl.BlockSpec((1, 1, bias_rows.shape[2]), lambda b, h, qi: (h, 0, 0)),
            vspec(B_QK_DIM), vspec(B_QK_DIM), vspec(B_QK_DIM), vspec(B_QK_DIM), vspec(B_V_DIM),
        ],
        out_specs=pl.BlockSpec((tq, B_V_DIM), lambda b, h, qi: (b * nq + qi, h)),
        out_shape=jax.ShapeDtypeStruct((t, B_HEADS * B_V_DIM), BF16),
        scratch_shapes=[pltpu.VMEM((hi - lo + 1, tq, LANE), F32)],
        compiler_params=_params("parallel", "parallel", "arbitrary"),
        name="diff_attn",
    )(proj, proj, proj, bias_rows, vec(lq1), vec(lk1), vec(lq2), vec(lk2), vec(subln))


def _gqa_kernel(q_ref, k_ref, v_ref, o_ref, *, tq, seq, group):
    span = seq // KEY_SPLITS
    bounds = [0] + [span // 2 + i * span for i in range(KEY_SPLITS)] + [seq]
    q = q_ref[...].reshape(group * tq, HEAD_DIM)
    parts = []
    for k0, k1 in zip(bounds[:-1], bounds[1:]):
        parts.append(_partial_softmax(_dot_nt(q, k_ref[0, k0:k1, :]), v_ref[0, k0:k1, :]))
    o = _merge_partials(parts)
    for g in range(group):
        o_ref[:, g * HEAD_DIM:(g + 1) * HEAD_DIM] = o[g * tq:(g + 1) * tq].astype(o_ref.dtype)


def gqa_attention(proj, *, batch, seq):
    nc, t, _ = proj.shape
    tq = 256
    group = C_HEADS // C_KV_HEADS
    nq = seq // tq
    return pl.pallas_call(
        functools.partial(_gqa_kernel, tq=tq, seq=seq, group=group),
        grid=(batch, C_KV_HEADS, nq),
        in_specs=[
            pl.BlockSpec((group, tq, LANE), lambda b, n, qi: (n, b * nq + qi, 0)),
            pl.BlockSpec((1, seq, LANE), lambda b, n, qi: (C_HEADS + n, b, 0)),
            pl.BlockSpec((1, seq, LANE), lambda b, n, qi: (C_HEADS + C_KV_HEADS + n, b, 0)),
        ],
        out_specs=pl.BlockSpec((tq, group * HEAD_DIM), lambda b, n, qi: (b * nq + qi, n)),
        out_shape=jax.ShapeDtypeStruct((t, C_HEADS * HEAD_DIM), BF16),
        compiler_params=_params("parallel", "parallel", "arbitrary"),
        name="gqa_attn",
    )(proj, proj, proj)


NA_QROWS = 4
NA_KROWS = NA_QROWS + NA_ROWS
NA_BLOCKS_PER_STEP = 4


def _na_kernel(q_ref, k_ref, v_ref, rows_ref, o_ref, bm_ref, *, rows, nsub):
    step = pl.program_id(2)
    tq = NA_QROWS * GRID_W
    nkeys = NA_KROWS * GRID_W
    nblk = rows // NA_QROWS
    kr = min(NA_ROWS, rows)

    @pl.when(step == 0)
    def _():
        lane = lax.broadcasted_iota(jnp.int32, (GRID_W, LANE), 1)
        qj = lax.broadcasted_iota(jnp.int32, (GRID_W, LANE), 0)
        kj = lane % GRID_W
        cs = jnp.clip(qj - NA_COLS // 2, 0, GRID_W - NA_COLS)
        col_ok = (kj >= cs) & (kj < cs + NA_COLS)
        left = lane < GRID_W
        neg = jnp.full((GRID_W, LANE), NEG_INF, F32)
        bands = {}

        def band(dr, right):
            if (dr, right) not in bands:
                wide = jnp.broadcast_to(rows_ref[0, dr], (GRID_W, LANE))
                rolled = pltpu.roll(wide, GRID_W if right else 0, 1, stride=1, stride_axis=0)
                bands[dr, right] = jnp.where(col_ok, rolled, NEG_INF)
            return bands[dr, right]

        for variant, blk in enumerate((0, 1, nblk - 1)):
            wrow = min(max(blk * NA_QROWS - NA_ROWS // 2, 0), rows - NA_KROWS)
            for a in range(NA_QROWS):
                qi = blk * NA_QROWS + a
                rs = min(max(qi - kr // 2, 0), rows - kr)
                for pair in range(NA_KROWS // 2):
                    halves = []
                    for right in (False, True):
                        ki = wrow + 2 * pair + int(right)
                        halves.append(band(ki - qi + NA_ROWS - 1, right) if rs <= ki < rs + kr else neg)
                    bm_ref[variant, a * GRID_W:(a + 1) * GRID_W, pair * LANE:(pair + 1) * LANE] = (
                        jnp.where(left, halves[0], halves[1]))

    for c in range(nsub):
        blk = step * nsub + c
        wrow = jnp.clip(blk * NA_QROWS - NA_ROWS // 2, 0, rows - NA_KROWS)
        kstart = pl.multiple_of(wrow * GRID_W, NA_QROWS * GRID_W)
        kw = k_ref[0, pl.ds(kstart, nkeys), :]
        vw = v_ref[0, pl.ds(kstart, nkeys), :]
        variant = jnp.where(blk == 0, 0, jnp.where(blk == nblk - 1, 2, 1))
        s = _dot_nt(q_ref[0, c * tq:(c + 1) * tq, :], kw) + bm_ref[variant]
        m = jnp.max(s, axis=-1, keepdims=True)
        p = jnp.exp2(s - m)
        denom = jnp.sum(p, axis=-1, keepdims=True)
        o_ref[c * tq:(c + 1) * tq, :] = (_dot(p.astype(BF16), vw) / denom).astype(o_ref.dtype)


def _na_bias_rows(rpb):
    r = LOG2E * rpb.astype(F32)
    gap = jnp.zeros(r.shape[:-1] + (LANE - r.shape[-1],), F32)
    return jnp.concatenate([r[..., NA_COLS - 1:], gap, r[..., :NA_COLS - 1]], axis=-1)[:, :, None, :]


def na_attention(proj, rpb, *, chunk0, batch, seq):
    nc, t, _ = proj.shape
    rows = seq // GRID_W
    nsub = NA_BLOCKS_PER_STEP
    assert rows % (NA_QROWS * nsub) == 0 and rows >= NA_KROWS and rows // NA_QROWS >= 3
    assert NA_KROWS % 2 == 0 and 2 * GRID_W == LANE and 2 * NA_COLS - 1 <= LANE
    nblk = rows // NA_QROWS
    nstep = nblk // nsub
    tq = NA_QROWS * GRID_W
    nkeys = NA_KROWS * GRID_W
    bias_rows = _na_bias_rows(rpb)
    return pl.pallas_call(
        functools.partial(_na_kernel, rows=rows, nsub=nsub),
        grid=(batch, D_HEADS, nstep),
        in_specs=[
            pl.BlockSpec((1, nsub * tq, LANE), lambda b, h, step: (chunk0 + h, b * nstep + step, 0)),
            pl.BlockSpec((1, seq, LANE), lambda b, h, step: (chunk0 + D_HEADS + h, b, 0)),
            pl.BlockSpec((1, seq, LANE), lambda b, h, step: (chunk0 + 2 * D_HEADS + h, b, 0)),
            pl.BlockSpec((1,) + bias_rows.shape[1:], lambda b, h, step: (h, 0, 0, 0)),
        ],
        out_specs=pl.BlockSpec((nsub * tq, LANE), lambda b, h, step: (b * nstep + step, h)),
        out_shape=jax.ShapeDtypeStruct((t, D_HEADS * HEAD_DIM), BF16),
        scratch_shapes=[pltpu.VMEM((3, tq, nkeys), F32)],
        compiler_params=_params("parallel", "parallel", "arbitrary"),
        name="na_attn",
    )(proj, proj, proj, bias_rows)


def _out_proj_kernel(a_ref, b_ref, w_ref, h_ref, out_ref, *, na):
    out_ref[...] = h_ref[...] + _dot(a_ref[...], w_ref[:na, :]) + _dot(b_ref[...], w_ref[na:, :])


def out_proj(a, b, w, h, *, tm, tn):
    t, d = h.shape
    na, nb = a.shape[1], b.shape[1]
    w = w.astype(BF16)
    return pl.pallas_call(
        functools.partial(_out_proj_kernel, na=na),
        grid=(t // tm, d // tn),
        in_specs=[pl.BlockSpec((tm, na), lambda i, j: (i, 0)),
                  pl.BlockSpec((tm, nb), lambda i, j: (i, 0)),
                  pl.BlockSpec((na + nb, tn), lambda i, j: (0, j)),
                  pl.BlockSpec((tm, tn), lambda i, j: (i, j))],
        out_specs=pl.BlockSpec((tm, tn), lambda i, j: (i, j)),
        out_shape=jax.ShapeDtypeStruct((t, d), F32),
        compiler_params=_params("parallel", "arbitrary"),
        name="out_proj",
    )(a, b, w, h)


FFN_HALO = 16


def _gelu_tanh(x):
    return 0.5 * x * (1.0 + jnp.tanh(math.sqrt(2.0 / math.pi) * (x + 0.044715 * (x * x * x))))


def _ffn_kernel(x_ref, xp_ref, xnx_ref, g_ref, wg_ref, wu_ref, cw_ref, cb_ref, wd_ref, gf_ref, out_ref,
                xn_ref, *, tm, tiles_per_seq, final_norm):
    i = pl.program_id(0)
    f = pl.program_id(1)
    nf = pl.num_programs(1)
    hl = FFN_HALO

    @pl.when(f == 0)
    def _():
        gain = g_ref[...]
        keep_prev = jnp.where(i % tiles_per_seq != 0, 1.0, 0.0)
        keep_next = jnp.where(i % tiles_per_seq != tiles_per_seq - 1, 1.0, 0.0)
        xn_ref[:hl, :] = (_rms(xp_ref[...], gain) * keep_prev).astype(BF16)
        xn_ref[hl:hl + tm, :] = _rms(x_ref[...], gain).astype(BF16)
        xn_ref[hl + tm:, :] = (_rms(xnx_ref[...], gain) * keep_next).astype(BF16)
        out_ref[...] = x_ref[...]

    ge = _dot(xn_ref[...], wg_ref[...])
    cw = cw_ref[...]
    g = (cw[0:1, :] * ge[hl - 1:hl - 1 + tm] + cw[1:2, :] * ge[hl:hl + tm]
         + cw[2:3, :] * ge[hl + 1:hl + 1 + tm] + cb_ref[...])
    u = _dot(xn_ref[hl:hl + tm, :], wu_ref[...])
    act = (_gelu_tanh(g) * u).astype(BF16)
    out_ref[...] += _dot(act, wd_ref[...])

    if final_norm:
        @pl.when(f == nf - 1)
        def _():
            out_ref[...] = _rms(out_ref[...], gf_ref[...])


def conv_ffn_block(h, gain, w_up, conv_w, conv_b, w_down, final_gain, *, seq, tm, tf, final_norm):
    t, d = h.shape
    ff = w_down.shape[0]
    assert t % tm == 0 and seq % tm == 0 and ff % tf == 0 and tm % FFN_HALO == 0
    nfb = ff // tf
    w_up = w_up.astype(BF16)
    w_down = w_down.astype(BF16)
    hb = tm // FFN_HALO
    last_halo = t // FFN_HALO - 1
    return pl.pallas_call(
        functools.partial(_ffn_kernel, tm=tm, tiles_per_seq=seq // tm, final_norm=final_norm),
        grid=(t // tm, nfb),
        in_specs=[
            pl.BlockSpec((tm, d), lambda i, f: (i, 0), pipeline_mode=pl.Buffered(1)),
            pl.BlockSpec((FFN_HALO, d), lambda i, f: (jnp.maximum(i * hb - 1, 0), 0)),
            pl.BlockSpec((FFN_HALO, d), lambda i, f: (jnp.minimum((i + 1) * hb, last_halo), 0)),
            pl.BlockSpec((1, d), lambda i, f: (0, 0)),
            pl.BlockSpec((d, tf), lambda i, f: (0, f)),
            pl.BlockSpec((d, tf), lambda i, f: (0, nfb + f)),
            pl.BlockSpec((3, tf), lambda i, f: (0, f)),
            pl.BlockSpec((1, tf), lambda i, f: (0, f)),
            pl.BlockSpec((tf, d), lambda i, f: (f, 0)),
            pl.BlockSpec((1, d), lambda i, f: (0, 0)),
        ],
        out_specs=pl.BlockSpec((tm, d), lambda i, f: (i, 0)),
        out_shape=jax.ShapeDtypeStruct((t, d), F32),
        scratch_shapes=[pltpu.VMEM((tm + 2 * FFN_HALO, d), BF16)],
        compiler_params=_params("parallel", "arbitrary"),
        name="conv_ffn",
    )(h, h, h, gain.reshape(1, d), w_up, w_up, conv_w, conv_b.reshape(1, ff), w_down,
      final_gain.reshape(1, d))


def _rope_tables(seq):
    tpos = jnp.arange(seq, dtype=jnp.int32)
    row = (tpos // GRID_W).astype(F32)
    col = (tpos % GRID_W).astype(F32)
    inv_freq = ROPE_THETA ** (-(jnp.arange(0, ROPE_AXIS_DIM, 2, dtype=F32) / ROPE_AXIS_DIM))
    ang = jnp.concatenate([row[:, None] * inv_freq[None], col[:, None] * inv_freq[None]], axis=-1)
    cos, sin = jnp.cos(ang), jnp.sin(ang)
    cos_dup = jnp.repeat(cos, 2, axis=-1)
    sin_signed = jnp.stack([-sin, sin], axis=-1).reshape(seq, HEAD_DIM)
    return cos_dup, sin_signed


def kernel(x, ln_mix, ln_ffn, ln_final, t5_table, ev_w_in, ev_w_out, diff_lq1, diff_lk1, diff_lq2,
           diff_lk2, diff_subln, od_w_in, od_w_out, gqa_q_norm, gqa_k_norm, na_rpb, ffn_w_up,
           ffn_conv_w, ffn_conv_b, ffn_w_down):
    batch, seq, d = x.shape
    depth = ln_mix.shape[0]
    t = batch * seq
    h = x.reshape(t, d)
    proj_tm, even_tn, odd_tn = 1024, A_HEADS * HEAD_DIM, 12 * LANE
    out_tiles = dict(tm=512, tn=d)
    ffn_tiles = dict(tm=1024, tf=512)

    for layer in range(depth):
        if layer % 2 == 0:
            e = layer // 2
            lambda_init = 0.8 - 0.6 * math.exp(-0.3 * layer)
            nat, *dil_projs = even_norm_proj(h, ln_mix[layer], ev_w_in[e],
                                             batch=batch, seq=seq, tm=proj_tm, tn=even_tn)
            oa = dilated_attention(nat, dil_projs, t5_table, batch=batch, seq=seq)
            ob = diff_attention(nat, t5_table[:, A_GROUPS * A_HEADS:], diff_lq1[e], diff_lk1[e],
                                diff_lq2[e], diff_lk2[e], diff_subln[e], chunk0=3 * A_HEADS,
                                batch=batch, seq=seq, lambda_init=lambda_init)
            h = out_proj(oa, ob, ev_w_out[e], h, **out_tiles)
        else:
            o = layer // 2
            n_rope = C_HEADS + C_KV_HEADS
            head_gain = jnp.concatenate([jnp.tile(gqa_q_norm[o][None], (C_HEADS, 1)),
                                         jnp.tile(gqa_k_norm[o][None], (C_KV_HEADS, 1))], axis=0)
            cos_dup, sin_signed = _rope_tables(seq)
            na_chunk0 = n_rope + C_KV_HEADS
            query_chunks = list(range(C_HEADS)) + list(range(na_chunk0, na_chunk0 + D_HEADS))
            proj = odd_norm_proj(h, ln_mix[layer], od_w_in[o], head_gain.astype(F32),
                                 cos_dup, sin_signed, query_chunks=query_chunks, tm=proj_tm, tn=odd_tn)
            oc = gqa_attention(proj, batch=batch, seq=seq)
            od = na_attention(proj, na_rpb[o], chunk0=na_chunk0, batch=batch, seq=seq)
            h = out_proj(oc, od, od_w_out[o], h, **out_tiles)
        h = conv_ffn_block(h, ln_ffn[layer], ffn_w_up[layer], ffn_conv_w[layer],
                           ffn_conv_b[layer], ffn_w_down[layer], ln_final,
                           seq=seq, final_norm=(layer == depth - 1), **ffn_tiles)
    return h.reshape(batch, seq, d)
```

```python
import functools
import math

import numpy as np
import jax
import jax.numpy as jnp
from jax import lax
from jax.experimental import pallas as pl
from jax.experimental.pallas import tpu as pltpu

HEAD_DIM = 128
GRID_W = 64
RMS_EPS = 1e-6
NEG_INF = -1e30
T5_BUCKETS = 32
T5_MAX_DIST = 1024
DILATED_PATTERNS = ((128, 1), (512, 4), (2048, 16))
A_GROUPS = len(DILATED_PATTERNS)
A_HEADS = 8
B_HEADS = 4
B_QK_DIM = 128
B_V_DIM = 2 * B_QK_DIM
C_HEADS = 8
C_KV_HEADS = 2
ROPE_THETA = 10000.0
ROPE_AXIS_DIM = HEAD_DIM // 2
D_HEADS = 8
NA_ROWS = 8
NA_COLS = 16

LANE = 128
V7X_VMEM_LIMIT = 56 * 1024 * 1024

BF16 = jnp.bfloat16
F32 = jnp.float32

LOG2E = math.log2(math.e)
QUERY_SCALE = HEAD_DIM ** -0.5 * LOG2E


def _params(*sem):
    return pltpu.CompilerParams(dimension_semantics=sem, vmem_limit_bytes=V7X_VMEM_LIMIT)


def _rms(xf, gain):
    ms = jnp.mean(xf * xf, axis=-1, keepdims=True)
    return xf * lax.rsqrt(ms + RMS_EPS) * gain


def _dot_nt(a, b):
    return lax.dot_general(a, b, (((1,), (1,)), ((), ())), preferred_element_type=F32)


def _dot(a, b):
    return jnp.dot(a, b, preferred_element_type=F32)


def _circulant(row, rows):
    return pltpu.roll(jnp.broadcast_to(row, (rows, row.shape[1])), 0, 1, stride=1, stride_axis=0)


def _circulant_positions(cols, rows, width):
    assert width >= rows + cols - 1
    p = np.arange(width)
    x = np.where(p < cols, p, p - width)
    return x, (p < cols) | (p > width - rows)


def _lanes(x, width):
    return x if width == LANE else jnp.concatenate([x] * (width // LANE), axis=1)


def _t5_bucket(rel):
    nb = T5_BUCKETS // 2
    max_exact = nb // 2
    ret = jnp.where(rel > 0, nb, 0)
    n = jnp.abs(rel)
    n_f = jnp.maximum(n, 1).astype(F32)
    large = max_exact + (jnp.log(n_f / max_exact) / math.log(T5_MAX_DIST / max_exact)
                         * (nb - max_exact)).astype(jnp.int32)
    large = jnp.minimum(large, nb - 1)
    return ret + jnp.where(n < max_exact, n, large)


def _store_chunks(acc, o_ref, n_chunks):
    for c in range(n_chunks):
        o_ref[c] = acc[:, c * LANE:(c + 1) * LANE].astype(o_ref.dtype)


def _even_proj_kernel(x_ref, g_ref, w_ref, nat_ref, *rest, plan, n_chunks, tm, qkv_period):
    dil_refs, (xn_ref, slab_ref) = rest[:-2], rest[-2:]
    j = pl.program_id(1)

    @pl.when(j == 0)
    def _():
        xn_ref[...] = _rms(x_ref[...], g_ref[...]).astype(BF16)

    factor = jnp.where(j % qkv_period == 0, QUERY_SCALE, 1.0)

    for lo, hi, dilation, out_idx in plan:
        @pl.when((j >= lo) & (j < hi))
        def _(dilation=dilation, out_idx=out_idx):
            acc = _dot(xn_ref[...], w_ref[...]) * factor
            if dilation == 1:
                _store_chunks(acc, nat_ref, n_chunks)
                return
            o_ref = dil_refs[out_idx]
            per = tm // dilation
            for c in range(n_chunks):
                slab_ref[c] = acc[:, c * LANE:(c + 1) * LANE]
            for c in range(n_chunks):
                for r in range(dilation):
                    o_ref[c, 0, r] = slab_ref[c, pl.ds(r, per, stride=dilation), :].astype(o_ref.dtype)


def even_norm_proj(x, gain, w, *, batch, seq, tm, tn):
    t, d = x.shape
    n = w.shape[1]
    w = w.astype(BF16)
    n_chunks = tn // LANE
    group_cols = 3 * A_HEADS * HEAD_DIM
    assert group_cols % tn == 0 and n % tn == 0 and seq % tm == 0 and t == batch * seq
    assert tn == A_HEADS * HEAD_DIM == B_HEADS * 2 * B_QK_DIM and B_QK_DIM == HEAD_DIM
    gt = group_cols // tn
    nj = n // tn
    tiles_per_seq = seq // tm
    dils = [dil for _, dil in DILATED_PATTERNS]
    assert dils[0] == 1 and all(tm % (16 * dil) == 0 for dil in dils)
    plan = [(0, gt, 1, None)]
    for gi in range(1, A_GROUPS):
        plan.append((gi * gt, (gi + 1) * gt, dils[gi], gi - 1))
    plan.append((A_GROUPS * gt, nj, 1, None))
    nat_tiles = gt + (nj - A_GROUPS * gt)

    def nat_map(i, j):
        return (jnp.where(j < gt, j, jnp.where(j < A_GROUPS * gt, gt - 1, j - (A_GROUPS - 1) * gt)), i, 0)

    def dil_map(gi):
        return lambda i, j: (jnp.clip(j - gi * gt, 0, gt - 1), i // tiles_per_seq, 0, i % tiles_per_seq, 0)

    out_specs = [pl.BlockSpec((n_chunks, tm, LANE), nat_map)]
    out_shape = [jax.ShapeDtypeStruct((nat_tiles * n_chunks, t, LANE), BF16)]
    for gi in range(1, A_GROUPS):
        dil = dils[gi]
        out_specs.append(pl.BlockSpec((n_chunks, 1, dil, tm // dil, LANE), dil_map(gi)))
        out_shape.append(jax.ShapeDtypeStruct((gt * n_chunks, batch, dil, seq // dil, LANE), BF16))
    return pl.pallas_call(
        functools.partial(_even_proj_kernel, plan=tuple(plan), n_chunks=n_chunks, tm=tm, qkv_period=gt),
        grid=(t // tm, nj),
        in_specs=[
            pl.BlockSpec((tm, d), lambda i, j: (i, 0)),
            pl.BlockSpec((1, d), lambda i, j: (0, 0)),
            pl.BlockSpec((d, tn), lambda i, j: (0, j)),
        ],
        out_specs=out_specs,
        out_shape=out_shape,
        scratch_shapes=[pltpu.VMEM((tm, d), BF16), pltpu.VMEM((n_chunks, tm, LANE), F32)],
        compiler_params=_params("parallel", "arbitrary"),
        name="even_norm_proj",
    )(x, gain.reshape(1, d), w)


def _rope_chunk(y, hg, cosd, sins, even_lane):
    y = _rms(y, hg)
    partner = jnp.where(even_lane, pltpu.roll(y, LANE - 1, axis=1), pltpu.roll(y, 1, axis=1))
    return y * cosd + partner * sins


def _odd_proj_kernel(x_ref, g_ref, w_ref, hg_ref, cos_ref, sin_ref, o_ref, xn_ref, *,
                     n_chunks, n_rope, n_tiles, query_chunks):
    j = pl.program_id(1)

    @pl.when(j == 0)
    def _():
        xn_ref[...] = _rms(x_ref[...], g_ref[...]).astype(BF16)

    acc = _dot(xn_ref[...], w_ref[...])

    for jt in range(n_tiles):
        @pl.when(j == jt)
        def _(jt=jt):
            if jt * n_chunks < n_rope:
                cosd = cos_ref[...]
                sins = sin_ref[...]
                even_lane = (lax.broadcasted_iota(jnp.int32, cosd.shape, 1) % 2) == 0
            for c in range(n_chunks):
                gc = jt * n_chunks + c
                y = acc[:, c * LANE:(c + 1) * LANE]
                if gc < n_rope:
                    y = _rope_chunk(y, hg_ref[gc:gc + 1, :], cosd, sins, even_lane)
                if gc in query_chunks:
                    y = y * QUERY_SCALE
                o_ref[c] = y.astype(o_ref.dtype)


def odd_norm_proj(x, gain, w, head_gain, cos_dup, sin_signed, *, query_chunks, tm, tn):
    t, d = x.shape
    n = w.shape[1]
    assert t % tm == 0 and n % tn == 0 and tn % LANE == 0
    w = w.astype(BF16)
    n_chunks = tn // LANE
    n_rope = head_gain.shape[0]
    s = cos_dup.shape[0]
    assert s % tm == 0
    sb = s // tm
    return pl.pallas_call(
        functools.partial(_odd_proj_kernel, n_chunks=n_chunks, n_rope=n_rope, n_tiles=n // tn,
                          query_chunks=frozenset(query_chunks)),
        grid=(t // tm, n // tn),
        in_specs=[
            pl.BlockSpec((tm, d), lambda i, j: (i, 0)),
            pl.BlockSpec((1, d), lambda i, j: (0, 0)),
            pl.BlockSpec((d, tn), lambda i, j: (0, j)),
            pl.BlockSpec((n_rope, LANE), lambda i, j: (0, 0)),
            pl.BlockSpec((tm, LANE), lambda i, j: (i % sb, 0)),
            pl.BlockSpec((tm, LANE), lambda i, j: (i % sb, 0)),
        ],
        out_specs=pl.BlockSpec((n_chunks, tm, LANE), lambda i, j: (j, i, 0)),
        out_shape=jax.ShapeDtypeStruct((n // LANE, t, LANE), BF16),
        scratch_shapes=[pltpu.VMEM((tm, d), BF16)],
        compiler_params=_params("parallel", "arbitrary"),
        name="odd_norm_proj",
    )(x, gain.reshape(1, d), w, head_gain, cos_dup, sin_signed)


A_TQ = 256
A_BLOCKS_PER_ITER = 16


def _a_geometry(seq, window, dilation):
    sub = seq // dilation
    half = window // (2 * dilation)
    tq = min(A_TQ, sub)
    win = min(tq + 2 * half, sub)
    assert seq % dilation == 0 and sub % tq == 0 and half % 16 == 0 and win % LANE == 0
    assert win == sub or win == tq + 2 * half
    return sub, half, tq, win


def _dilated_kernel(*refs, geoms):
    ng = len(geoms)
    qkv = refs[:3 * ng]
    bias_rows = refs[3 * ng:4 * ng]
    o_ref = refs[4 * ng]
    m_s, l_s, acc_s = refs[4 * ng + 1:4 * ng + 4]
    bms = refs[4 * ng + 4:]

    for g in range(ng):
        _, _, _, tq, win = geoms[g]
        for variant in range(bms[g].shape[0]):
            bms[g][variant] = _circulant(bias_rows[g][variant, 0], tq)[:, :win]

    order = sorted(range(ng), key=lambda g: -geoms[g][0])
    for pos, g in enumerate(order):
        dilation, sub, half, tq, win = geoms[g]
        q_ref, k_ref, v_ref = qkv[3 * g:3 * g + 3]
        bm_ref = bms[g]
        nq = sub // tq
        first = pos == 0
        nblocks = dilation * nq
        batch = math.gcd(A_BLOCKS_PER_ITER, nblocks)

        def one_block(it, dilation=dilation, sub=sub, half=half, tq=tq, win=win, nq=nq,
                      q_ref=q_ref, k_ref=k_ref, v_ref=v_ref, bm_ref=bm_ref, first=first):
            r = it // nq
            qi = it % nq
            q0 = pl.multiple_of(qi * tq, tq)
            kstart = pl.multiple_of(jnp.clip(q0 - half, 0, sub - win), half)
            if dilation == 1:
                q = q_ref[0, pl.ds(q0, tq), :]
                kw = k_ref[0, pl.ds(kstart, win), :]
                vw = v_ref[0, pl.ds(kstart, win), :]
                rows = pl.ds(q0, tq)
            else:
                q = q_ref[0, 0, r, pl.ds(q0, tq), :]
                kw = k_ref[0, 0, r, pl.ds(kstart, win), :]
                vw = v_ref[0, 0, r, pl.ds(kstart, win), :]
                rows = pl.ds(q0 * dilation + r, tq, stride=dilation)
            variant = 0 if nq == 1 else jnp.where(qi == 0, 0, jnp.where(qi == nq - 1, 2, 1))
            s = _dot_nt(q, kw) + bm_ref[variant]
            mb = jnp.max(s, axis=-1, keepdims=True)
            if first:
                p = jnp.exp2(s - mb)
                return (rows, jnp.broadcast_to(mb, (tq, LANE)),
                        jnp.broadcast_to(jnp.sum(p, axis=-1, keepdims=True), (tq, LANE)),
                        _dot(p.astype(BF16), vw))
            m_old = m_s[rows, :]
            m_new = jnp.maximum(m_old, mb)
            alpha = jnp.exp2(m_old - m_new)
            p = jnp.exp2(s - _lanes(m_new, win))
            return (rows, m_new, alpha * l_s[rows, :] + jnp.sum(p, axis=-1, keepdims=True),
                    alpha * acc_s[rows, :] + _dot(p.astype(BF16), vw))

        def blocks(step, carry, batch=batch, one_block=one_block):
            results = [one_block(step * batch + i) for i in range(batch)]
            for rows, m_new, l_new, acc_new in results:
                m_s[rows, :] = m_new
                l_s[rows, :] = l_new
                acc_s[rows, :] = acc_new
            return carry

        lax.fori_loop(0, nblocks // batch, blocks, 0)

    o_ref[...] = (acc_s[...] / l_s[...]).astype(o_ref.dtype)


def _dilated_bias_rows(t5_cols, dilation, half, tq, win, sub):
    nq = sub // tq
    offs = [0] if nq == 1 else [0, -half, tq - win]
    width = -(-(tq + win - 1) // LANE) * LANE
    x, used = _circulant_positions(win, tq, width)
    rows = []
    for off in offs:
        rel = x + off
        ok = jnp.asarray(used & (np.abs(rel) <= half))[None]
        bias = LOG2E * t5_cols[_t5_bucket(jnp.asarray(dilation * rel, jnp.int32))].T.astype(F32)
        rows.append(jnp.where(ok, bias, NEG_INF))
    return jnp.stack(rows)[:, :, None, :]


def dilated_attention(nat, dil_projs, t5_table, *, batch, seq):
    t = batch * seq
    geoms, args, in_specs, bias_rows, tables = [], [], [], [], []
    for gi, (window, dilation) in enumerate(DILATED_PATTERNS):
        sub, half, tq, win = _a_geometry(seq, window, dilation)
        geoms.append((dilation, sub, half, tq, win))
        for part in range(3):
            if dilation == 1:
                args.append(nat)
                in_specs.append(pl.BlockSpec(
                    (1, seq, LANE), lambda b, h, part=part: (part * A_HEADS + h, b, 0)))
            else:
                args.append(dil_projs[gi - 1])
                in_specs.append(pl.BlockSpec(
                    (1, 1, dilation, sub, LANE),
                    lambda b, h, part=part: (part * A_HEADS + h, b, 0, 0, 0)))
        bias_rows.append(_dilated_bias_rows(t5_table[:, gi * A_HEADS:(gi + 1) * A_HEADS],
                                            dilation, half, tq, win, sub))
        tables.append(pltpu.VMEM((bias_rows[-1].shape[0], tq, win), F32))
    for rows in bias_rows:
        args.append(rows)
        in_specs.append(pl.BlockSpec((rows.shape[0], 1) + rows.shape[2:], lambda b, h: (0, h, 0, 0)))
    return pl.pallas_call(
        functools.partial(_dilated_kernel, geoms=tuple(geoms)),
        grid=(batch, A_HEADS),
        in_specs=in_specs,
        out_specs=pl.BlockSpec((seq, LANE), lambda b, h: (b, h)),
        out_shape=jax.ShapeDtypeStruct((t, A_HEADS * LANE), BF16),
        scratch_shapes=[pltpu.VMEM((seq, LANE), F32)] * 3 + tables,
        compiler_params=_params("parallel", "arbitrary"),
        name="dilated_attn",
    )(*args)


KEY_SPLITS = 4


def _partial_softmax(s, v):
    m = jnp.max(s, axis=-1, keepdims=True)
    p = jnp.exp2(s - m)
    return m, jnp.sum(p, axis=-1, keepdims=True), _dot(p.astype(BF16), v)


def _merge_partials(parts):
    m = parts[0][0]
    for pm, _, _ in parts[1:]:
        m = jnp.maximum(m, pm)
    weights = [jnp.exp2(pm - m) for pm, _, _ in parts]
    denom = sum(w * pl_ for w, (_, pl_, _) in zip(weights, parts))
    acc = sum(w * pa for w, (_, _, pa) in zip(weights, parts))
    return acc / denom


T5_BAND = T5_MAX_DIST


def _diff_band(tq):
    lo = -((T5_BAND + LANE - 1) // LANE) - 1
    hi = (T5_BAND + tq - 1 + LANE - 1) // LANE
    return lo, hi


def _diff_kernel(q_ref, k_ref, v_ref, row_ref, lq1_ref, lk1_ref, lq2_ref, lk2_ref, sub_ref, o_ref,
                 tz_ref, *, tq, seq, lambda_init):
    qi = pl.program_id(2)
    lo, hi = _diff_band(tq)
    q_chunk0 = qi * (tq // LANE)
    span = seq // KEY_SPLITS

    @pl.when(qi == 0)
    def _():
        band = _circulant(row_ref[0], tq)
        for j in range(hi - lo + 1):
            tz_ref[j] = band[:, j * LANE:(j + 1) * LANE]

    def chain(mi, part):
        k0 = part * span
        bias = jnp.concatenate(
            [tz_ref[jnp.clip(cc - q_chunk0, lo, hi) - lo]
             for cc in range(k0 // LANE, (k0 + span) // LANE)], axis=1)
        s = _dot_nt(q_ref[mi], k_ref[mi, k0:k0 + span, :]) + bias
        vc = jnp.concatenate([v_ref[0, k0:k0 + span, :], v_ref[1, k0:k0 + span, :]], axis=1)
        return _partial_softmax(s, vc)

    chains = {(mi, part): chain(mi, part) for part in range(KEY_SPLITS) for mi in range(2)}

    def attend(mi):
        return _merge_partials([chains[mi, part] for part in range(KEY_SPLITS)])

    lam = (jnp.exp(jnp.sum(lq1_ref[...] * lk1_ref[...], axis=-1, keepdims=True))
           - jnp.exp(jnp.sum(lq2_ref[...] * lk2_ref[...], axis=-1, keepdims=True)) + lambda_init)
    o = attend(0) - lam * attend(1)
    o_ref[...] = (_rms(o, sub_ref[...]) * (1.0 - lambda_init)).astype(o_ref.dtype)


def _diff_bias_rows(t5_cols, tq):
    lo, hi = _diff_band(tq)
    cols = (hi - lo + 1) * LANE
    width = -(-(tq + cols - 1) // LANE) * LANE
    x, _ = _circulant_positions(cols, tq, width)
    rel = jnp.asarray(x + LANE * lo, jnp.int32)
    return (LOG2E * t5_cols[_t5_bucket(rel)].T.astype(F32))[:, None, :]


def diff_attention(proj, t5_cols, lq1, lk1, lq2, lk2, subln, *, chunk0, batch, seq, lambda_init):
    nc, t, _ = proj.shape
    tq = 256
    assert seq % tq == 0 and seq % (KEY_SPLITS * LANE) == 0 and chunk0 % 2 == 0
    nq = seq // tq
    bias_rows = _diff_bias_rows(t5_cols, tq)
    lo, hi = _diff_band(tq)
    qb, kb, vb = chunk0 // 2, chunk0 // 2 + B_HEADS, chunk0 // 2 + 2 * B_HEADS
    vec = lambda a: a.reshape(1, -1).astype(F32)
    vspec = lambda n: pl.BlockSpec((1, n), lambda b, h, qi: (0, 0))
    return pl.pallas_call(
        functools.partial(_diff_kernel, tq=tq, seq=seq, lambda_init=lambda_init),
        grid=(batch, B_HEADS, nq),
        in_specs=[
            pl.BlockSpec((2, tq, LANE), lambda b, h, qi: (qb + h, b * nq + qi, 0)),
            pl.BlockSpec((2, seq, LANE), lambda b, h, qi: (kb + h, b, 0)),
            pl.BlockSpec((2, seq, LANE), lambda b, h, qi: (vb + h, b, 0)),
            pl.BlockSpec((1, 1, bias_rows.shape[2]), lambda b, h, qi: (h, 0, 0)),
            vspec(B_QK_DIM), vspec(B_QK_DIM), vspec(B_QK_DIM), vspec(B_QK_DIM), vspec(B_V_DIM),
        ],
        out_specs=pl.BlockSpec((tq, B_V_DIM), lambda b, h, qi: (b * nq + qi, h)),
        out_shape=jax.ShapeDtypeStruct((t, B_HEADS * B_V_DIM), BF16),
        scratch_shapes=[pltpu.VMEM((hi - lo + 1, tq, LANE), F32)],
        compiler_params=_params("parallel", "parallel", "arbitrary"),
        name="diff_attn",
    )(proj, proj, proj, bias_rows, vec(lq1), vec(lk1), vec(lq2), vec(lk2), vec(subln))


def _gqa_kernel(q_ref, k_ref, v_ref, o_ref, *, tq, seq, group):
    span = seq // KEY_SPLITS
    bounds = [0] + [span // 2 + i * span for i in range(KEY_SPLITS)] + [seq]
    q = q_ref[...].reshape(group * tq, HEAD_DIM)
    parts = []
    for k0, k1 in zip(bounds[:-1], bounds[1:]):
        parts.append(_partial_softmax(_dot_nt(q, k_ref[0, k0:k1, :]), v_ref[0, k0:k1, :]))
    o = _merge_partials(parts)
    for g in range(group):
        o_ref[:, g * HEAD_DIM:(g + 1) * HEAD_DIM] = o[g * tq:(g + 1) * tq].astype(o_ref.dtype)


def gqa_attention(proj, *, batch, seq):
    nc, t, _ = proj.shape
    tq = 256
    group = C_HEADS // C_KV_HEADS
    nq = seq // tq
    return pl.pallas_call(
        functools.partial(_gqa_kernel, tq=tq, seq=seq, group=group),
        grid=(batch, C_KV_HEADS, nq),
        in_specs=[
            pl.BlockSpec((group, tq, LANE), lambda b, n, qi: (n, b * nq + qi, 0)),
            pl.BlockSpec((1, seq, LANE), lambda b, n, qi: (C_HEADS + n, b, 0)),
            pl.BlockSpec((1, seq, LANE), lambda b, n, qi: (C_HEADS + C_KV_HEADS + n, b, 0)),
        ],
        out_specs=pl.BlockSpec((tq, group * HEAD_DIM), lambda b, n, qi: (b * nq + qi, n)),
        out_shape=jax.ShapeDtypeStruct((t, C_HEADS * HEAD_DIM), BF16),
        compiler_params=_params("parallel", "parallel", "arbitrary"),
        name="gqa_attn",
    )(proj, proj, proj)


NA_QROWS = 4
NA_KROWS = NA_QROWS + NA_ROWS
NA_BLOCKS_PER_STEP = 8


def _na_kernel(q_ref, k_ref, v_ref, rows_ref, o_ref, bm_ref, *, rows, nsub):
    step = pl.program_id(2)
    tq = NA_QROWS * GRID_W
    nkeys = NA_KROWS * GRID_W
    nblk = rows // NA_QROWS
    kr = min(NA_ROWS, rows)

    @pl.when(step == 0)
    def _():
        lane = lax.broadcasted_iota(jnp.int32, (GRID_W, LANE), 1)
        qj = lax.broadcasted_iota(jnp.int32, (GRID_W, LANE), 0)
        kj = lane % GRID_W
        cs = jnp.clip(qj - NA_COLS // 2, 0, GRID_W - NA_COLS)
        col_ok = (kj >= cs) & (kj < cs + NA_COLS)
        left = lane < GRID_W
        neg = jnp.full((GRID_W, LANE), NEG_INF, F32)
        bands = {}

        def band(dr, right):
            if (dr, right) not in bands:
                wide = jnp.broadcast_to(rows_ref[0, dr], (GRID_W, LANE))
                rolled = pltpu.roll(wide, GRID_W if right else 0, 1, stride=1, stride_axis=0)
                bands[dr, right] = jnp.where(col_ok, rolled, NEG_INF)
            return bands[dr, right]

        for variant, blk in enumerate((0, 1, nblk - 1)):
            wrow = min(max(blk * NA_QROWS - NA_ROWS // 2, 0), rows - NA_KROWS)
            for a in range(NA_QROWS):
                qi = blk * NA_QROWS + a
                rs = min(max(qi - kr // 2, 0), rows - kr)
                for pair in range(NA_KROWS // 2):
                    halves = []
                    for right in (False, True):
                        ki = wrow + 2 * pair + int(right)
                        halves.append(band(ki - qi + NA_ROWS - 1, right) if rs <= ki < rs + kr else neg)
                    bm_ref[variant, a * GRID_W:(a + 1) * GRID_W, pair * LANE:(pair + 1) * LANE] = (
                        jnp.where(left, halves[0], halves[1]))

    for c in range(nsub):
        blk = step * nsub + c
        wrow = jnp.clip(blk * NA_QROWS - NA_ROWS // 2, 0, rows - NA_KROWS)
        kstart = pl.multiple_of(wrow * GRID_W, NA_QROWS * GRID_W)
        kw = k_ref[0, pl.ds(kstart, nkeys), :]
        vw = v_ref[0, pl.ds(kstart, nkeys), :]
        variant = jnp.where(blk == 0, 0, jnp.where(blk == nblk - 1, 2, 1))
        s = _dot_nt(q_ref[0, c * tq:(c + 1) * tq, :], kw) + bm_ref[variant]
        m = jnp.max(s, axis=-1, keepdims=True)
        p = jnp.exp2(s - m)
        denom = jnp.sum(p, axis=-1, keepdims=True)
        o_ref[c * tq:(c + 1) * tq, :] = (_dot(p.astype(BF16), vw) / denom).astype(o_ref.dtype)


def _na_bias_rows(rpb):
    r = LOG2E * rpb.astype(F32)
    gap = jnp.zeros(r.shape[:-1] + (LANE - r.shape[-1],), F32)
    return jnp.concatenate([r[..., NA_COLS - 1:], gap, r[..., :NA_COLS - 1]], axis=-1)[:, :, None, :]


def na_attention(proj, rpb, *, chunk0, batch, seq):
    nc, t, _ = proj.shape
    rows = seq // GRID_W
    nsub = NA_BLOCKS_PER_STEP
    assert rows % (NA_QROWS * nsub) == 0 and rows >= NA_KROWS and rows // NA_QROWS >= 3
    assert NA_KROWS % 2 == 0 and 2 * GRID_W == LANE and 2 * NA_COLS - 1 <= LANE
    nblk = rows // NA_QROWS
    nstep = nblk // nsub
    tq = NA_QROWS * GRID_W
    nkeys = NA_KROWS * GRID_W
    bias_rows = _na_bias_rows(rpb)
    return pl.pallas_call(
        functools.partial(_na_kernel, rows=rows, nsub=nsub),
        grid=(batch, D_HEADS, nstep),
        in_specs=[
            pl.BlockSpec((1, nsub * tq, LANE), lambda b, h, step: (chunk0 + h, b * nstep + step, 0)),
            pl.BlockSpec((1, seq, LANE), lambda b, h, step: (chunk0 + D_HEADS + h, b, 0)),
            pl.BlockSpec((1, seq, LANE), lambda b, h, step: (chunk0 + 2 * D_HEADS + h, b, 0)),
            pl.BlockSpec((1,) + bias_rows.shape[1:], lambda b, h, step: (h, 0, 0, 0)),
        ],
        out_specs=pl.BlockSpec((nsub * tq, LANE), lambda b, h, step: (b * nstep + step, h)),
        out_shape=jax.ShapeDtypeStruct((t, D_HEADS * HEAD_DIM), BF16),
        scratch_shapes=[pltpu.VMEM((3, tq, nkeys), F32)],
        compiler_params=_params("parallel", "parallel", "arbitrary"),
        name="na_attn",
    )(proj, proj, proj, bias_rows)


def _out_proj_kernel(a_ref, b_ref, w_ref, h_ref, out_ref, *, na):
    out_ref[...] = h_ref[...] + _dot(a_ref[...], w_ref[:na, :]) + _dot(b_ref[...], w_ref[na:, :])


def out_proj(a, b, w, h, *, tm, tn):
    t, d = h.shape
    na, nb = a.shape[1], b.shape[1]
    w = w.astype(BF16)
    return pl.pallas_call(
        functools.partial(_out_proj_kernel, na=na),
        grid=(t // tm, d // tn),
        in_specs=[pl.BlockSpec((tm, na), lambda i, j: (i, 0)),
                  pl.BlockSpec((tm, nb), lambda i, j: (i, 0)),
                  pl.BlockSpec((na + nb, tn), lambda i, j: (0, j)),
                  pl.BlockSpec((tm, tn), lambda i, j: (i, j))],
        out_specs=pl.BlockSpec((tm, tn), lambda i, j: (i, j)),
        out_shape=jax.ShapeDtypeStruct((t, d), F32),
        compiler_params=_params("parallel", "arbitrary"),
        name="out_proj",
    )(a, b, w, h)


FFN_HALO = 16


def _gelu_tanh(x):
    return 0.5 * x * (1.0 + jnp.tanh(math.sqrt(2.0 / math.pi) * (x + 0.044715 * (x * x * x))))


def _ffn_kernel(x_ref, xp_ref, xnx_ref, g_ref, wg_ref, wu_ref, cw_ref, cb_ref, wd_ref, gf_ref, out_ref,
                xn_ref, *, tm, tiles_per_seq, final_norm):
    i = pl.program_id(0)
    f = pl.program_id(1)
    nf = pl.num_programs(1)
    hl = FFN_HALO

    @pl.when(f == 0)
    def _():
        gain = g_ref[...]
        keep_prev = jnp.where(i % tiles_per_seq != 0, 1.0, 0.0)
        keep_next = jnp.where(i % tiles_per_seq != tiles_per_seq - 1, 1.0, 0.0)
        xn_ref[:hl, :] = (_rms(xp_ref[...], gain) * keep_prev).astype(BF16)
        xn_ref[hl:hl + tm, :] = _rms(x_ref[...], gain).astype(BF16)
        xn_ref[hl + tm:, :] = (_rms(xnx_ref[...], gain) * keep_next).astype(BF16)
        out_ref[...] = x_ref[...]

    ge = _dot(xn_ref[...], wg_ref[...])
    cw = cw_ref[...]
    g = (cw[0:1, :] * ge[hl - 1:hl - 1 + tm] + cw[1:2, :] * ge[hl:hl + tm]
         + cw[2:3, :] * ge[hl + 1:hl + 1 + tm] + cb_ref[...])
    u = _dot(xn_ref[hl:hl + tm, :], wu_ref[...])
    act = (_gelu_tanh(g) * u).astype(BF16)
    out_ref[...] += _dot(act, wd_ref[...])

    if final_norm:
        @pl.when(f == nf - 1)
        def _():
            out_ref[...] = _rms(out_ref[...], gf_ref[...])


def conv_ffn_block(h, gain, w_up, conv_w, conv_b, w_down, final_gain, *, seq, tm, tf, final_norm):
    t, d = h.shape
    ff = w_down.shape[0]
    assert t % tm == 0 and seq % tm == 0 and ff % tf == 0 and tm % FFN_HALO == 0
    nfb = ff // tf
    w_up = w_up.astype(BF16)
    w_down = w_down.astype(BF16)
    hb = tm // FFN_HALO
    last_halo = t // FFN_HALO - 1
    return pl.pallas_call(
        functools.partial(_ffn_kernel, tm=tm, tiles_per_seq=seq // tm, final_norm=final_norm),
        grid=(t // tm, nfb),
        in_specs=[
            pl.BlockSpec((tm, d), lambda i, f: (i, 0), pipeline_mode=pl.Buffered(1)),
            pl.BlockSpec((FFN_HALO, d), lambda i, f: (jnp.maximum(i * hb - 1, 0), 0)),
            pl.BlockSpec((FFN_HALO, d), lambda i, f: (jnp.minimum((i + 1) * hb, last_halo), 0)),
            pl.BlockSpec((1, d), lambda i, f: (0, 0)),
            pl.BlockSpec((d, tf), lambda i, f: (0, f)),
            pl.BlockSpec((d, tf), lambda i, f: (0, nfb + f)),
            pl.BlockSpec((3, tf), lambda i, f: (0, f)),
            pl.BlockSpec((1, tf), lambda i, f: (0, f)),
            pl.BlockSpec((tf, d), lambda i, f: (f, 0)),
            pl.BlockSpec((1, d), lambda i, f: (0, 0)),
        ],
        out_specs=pl.BlockSpec((tm, d), lambda i, f: (i, 0)),
        out_shape=jax.ShapeDtypeStruct((t, d), F32),
        scratch_shapes=[pltpu.VMEM((tm + 2 * FFN_HALO, d), BF16)],
        compiler_params=_params("parallel", "arbitrary"),
        name="conv_ffn",
    )(h, h, h, gain.reshape(1, d), w_up, w_up, conv_w, conv_b.reshape(1, ff), w_down,
      final_gain.reshape(1, d))


def _rope_tables(seq):
    tpos = jnp.arange(seq, dtype=jnp.int32)
    row = (tpos // GRID_W).astype(F32)
    col = (tpos % GRID_W).astype(F32)
    inv_freq = ROPE_THETA ** (-(jnp.arange(0, ROPE_AXIS_DIM, 2, dtype=F32) / ROPE_AXIS_DIM))
    ang = jnp.concatenate([row[:, None] * inv_freq[None], col[:, None] * inv_freq[None]], axis=-1)
    cos, sin = jnp.cos(ang), jnp.sin(ang)
    cos_dup = jnp.repeat(cos, 2, axis=-1)
    sin_signed = jnp.stack([-sin, sin], axis=-1).reshape(seq, HEAD_DIM)
    return cos_dup, sin_signed


def kernel(x, ln_mix, ln_ffn, ln_final, t5_table, ev_w_in, ev_w_out, diff_lq1, diff_lk1, diff_lq2,
           diff_lk2, diff_subln, od_w_in, od_w_out, gqa_q_norm, gqa_k_norm, na_rpb, ffn_w_up,
           ffn_conv_w, ffn_conv_b, ffn_w_down):
    batch, seq, d = x.shape
    depth = ln_mix.shape[0]
    t = batch * seq
    h = x.reshape(t, d)
    proj_tm, even_tn, odd_tn = 1024, A_HEADS * HEAD_DIM, 12 * LANE
    out_tiles = dict(tm=512, tn=d)
    ffn_tiles = dict(tm=1024, tf=512)

    for layer in range(depth):
        if layer % 2 == 0:
            e = layer // 2
            lambda_init = 0.8 - 0.6 * math.exp(-0.3 * layer)
            nat, *dil_projs = even_norm_proj(h, ln_mix[layer], ev_w_in[e],
                                             batch=batch, seq=seq, tm=proj_tm, tn=even_tn)
            oa = dilated_attention(nat, dil_projs, t5_table, batch=batch, seq=seq)
            ob = diff_attention(nat, t5_table[:, A_GROUPS * A_HEADS:], diff_lq1[e], diff_lk1[e],
                                diff_lq2[e], diff_lk2[e], diff_subln[e], chunk0=3 * A_HEADS,
                                batch=batch, seq=seq, lambda_init=lambda_init)
            h = out_proj(oa, ob, ev_w_out[e], h, **out_tiles)
        else:
            o = layer // 2
            n_rope = C_HEADS + C_KV_HEADS
            head_gain = jnp.concatenate([jnp.tile(gqa_q_norm[o][None], (C_HEADS, 1)),
                                         jnp.tile(gqa_k_norm[o][None], (C_KV_HEADS, 1))], axis=0)
            cos_dup, sin_signed = _rope_tables(seq)
            na_chunk0 = n_rope + C_KV_HEADS
            query_chunks = list(range(C_HEADS)) + list(range(na_chunk0, na_chunk0 + D_HEADS))
            proj = odd_norm_proj(h, ln_mix[layer], od_w_in[o], head_gain.astype(F32),
                                 cos_dup, sin_signed, query_chunks=query_chunks, tm=proj_tm, tn=odd_tn)
            oc = gqa_attention(proj, batch=batch, seq=seq)
            od = na_attention(proj, na_rpb[o], chunk0=na_chunk0, batch=batch, seq=seq)
            h = out_proj(oc, od, od_w_out[o], h, **out_tiles)
        h = conv_ffn_block(h, ln_ffn[layer], ffn_w_up[layer], ffn_conv_w[layer],
                           ffn_conv_b[layer], ffn_w_down[layer], ln_final,
                           seq=seq, final_norm=(layer == depth - 1), **ffn_tiles)
    return h.reshape(batch, seq, d)
```

```python
import functools
import math

import numpy as np
import jax
import jax.numpy as jnp
from jax import lax
from jax.experimental import pallas as pl
from jax.experimental.pallas import tpu as pltpu

HEAD_DIM = 128
GRID_W = 64
RMS_EPS = 1e-6
NEG_INF = -1e30
T5_BUCKETS = 32
T5_MAX_DIST = 1024
DILATED_PATTERNS = ((128, 1), (512, 4), (2048, 16))
A_GROUPS = len(DILATED_PATTERNS)
A_HEADS = 8
B_HEADS = 4
B_QK_DIM = 128
B_V_DIM = 2 * B_QK_DIM
C_HEADS = 8
C_KV_HEADS = 2
ROPE_THETA = 10000.0
ROPE_AXIS_DIM = HEAD_DIM // 2
D_HEADS = 8
NA_ROWS = 8
NA_COLS = 16

LANE = 128
V7X_VMEM_LIMIT = 56 * 1024 * 1024

BF16 = jnp.bfloat16
F32 = jnp.float32

LOG2E = math.log2(math.e)
QUERY_SCALE = HEAD_DIM ** -0.5 * LOG2E


def _params(*sem):
    return pltpu.CompilerParams(dimension_semantics=sem, vmem_limit_bytes=V7X_VMEM_LIMIT)


def _rms(xf, gain):
    ms = jnp.mean(xf * xf, axis=-1, keepdims=True)
    return xf * lax.rsqrt(ms + RMS_EPS) * gain


def _dot_nt(a, b):
    return lax.dot_general(a, b, (((1,), (1,)), ((), ())), preferred_element_type=F32)


def _dot(a, b):
    return jnp.dot(a, b, preferred_element_type=F32)


def _circulant(row, rows):
    return pltpu.roll(jnp.broadcast_to(row, (rows, row.shape[1])), 0, 1, stride=1, stride_axis=0)


def _circulant_positions(cols, rows, width):
    assert width >= rows + cols - 1
    p = np.arange(width)
    x = np.where(p < cols, p, p - width)
    return x, (p < cols) | (p > width - rows)


def _lanes(x, width):
    return x if width == LANE else jnp.concatenate([x] * (width // LANE), axis=1)


def _t5_bucket(rel):
    nb = T5_BUCKETS // 2
    max_exact = nb // 2
    ret = jnp.where(rel > 0, nb, 0)
    n = jnp.abs(rel)
    n_f = jnp.maximum(n, 1).astype(F32)
    large = max_exact + (jnp.log(n_f / max_exact) / math.log(T5_MAX_DIST / max_exact)
                         * (nb - max_exact)).astype(jnp.int32)
    large = jnp.minimum(large, nb - 1)
    return ret + jnp.where(n < max_exact, n, large)


def _store_chunks(acc, o_ref, n_chunks):
    for c in range(n_chunks):
        o_ref[c] = acc[:, c * LANE:(c + 1) * LANE].astype(o_ref.dtype)


def _even_proj_kernel(x_ref, g_ref, w_ref, nat_ref, *rest, plan, n_chunks, tm, qkv_period):
    dil_refs, (xn_ref, slab_ref) = rest[:-2], rest[-2:]
    j = pl.program_id(1)

    @pl.when(j == 0)
    def _():
        xn_ref[...] = _rms(x_ref[...], g_ref[...]).astype(BF16)

    factor = jnp.where(j % qkv_period == 0, QUERY_SCALE, 1.0)

    for lo, hi, dilation, out_idx in plan:
        @pl.when((j >= lo) & (j < hi))
        def _(dilation=dilation, out_idx=out_idx):
            acc = _dot(xn_ref[...], w_ref[...]) * factor
            if dilation == 1:
                _store_chunks(acc, nat_ref, n_chunks)
                return
            o_ref = dil_refs[out_idx]
            per = tm // dilation
            for c in range(n_chunks):
                slab_ref[c] = acc[:, c * LANE:(c + 1) * LANE]
            for c in range(n_chunks):
                for r in range(dilation):
                    o_ref[c, 0, r] = slab_ref[c, pl.ds(r, per, stride=dilation), :].astype(o_ref.dtype)


def even_norm_proj(x, gain, w, *, batch, seq, tm, tn):
    t, d = x.shape
    n = w.shape[1]
    w = w.astype(BF16)
    n_chunks = tn // LANE
    group_cols = 3 * A_HEADS * HEAD_DIM
    assert group_cols % tn == 0 and n % tn == 0 and seq % tm == 0 and t == batch * seq
    assert tn == A_HEADS * HEAD_DIM == B_HEADS * 2 * B_QK_DIM and B_QK_DIM == HEAD_DIM
    gt = group_cols // tn
    nj = n // tn
    tiles_per_seq = seq // tm
    dils = [dil for _, dil in DILATED_PATTERNS]
    assert dils[0] == 1 and all(tm % (16 * dil) == 0 for dil in dils)
    plan = [(0, gt, 1, None)]
    for gi in range(1, A_GROUPS):
        plan.append((gi * gt, (gi + 1) * gt, dils[gi], gi - 1))
    plan.append((A_GROUPS * gt, nj, 1, None))
    nat_tiles = gt + (nj - A_GROUPS * gt)

    def nat_map(i, j):
        return (jnp.where(j < gt, j, jnp.where(j < A_GROUPS * gt, gt - 1, j - (A_GROUPS - 1) * gt)), i, 0)

    def dil_map(gi):
        return lambda i, j: (jnp.clip(j - gi * gt, 0, gt - 1), i // tiles_per_seq, 0, i % tiles_per_seq, 0)

    out_specs = [pl.BlockSpec((n_chunks, tm, LANE), nat_map)]
    out_shape = [jax.ShapeDtypeStruct((nat_tiles * n_chunks, t, LANE), BF16)]
    for gi in range(1, A_GROUPS):
        dil = dils[gi]
        out_specs.append(pl.BlockSpec((n_chunks, 1, dil, tm // dil, LANE), dil_map(gi)))
        out_shape.append(jax.ShapeDtypeStruct((gt * n_chunks, batch, dil, seq // dil, LANE), BF16))
    return pl.pallas_call(
        functools.partial(_even_proj_kernel, plan=tuple(plan), n_chunks=n_chunks, tm=tm, qkv_period=gt),
        grid=(t // tm, nj),
        in_specs=[
            pl.BlockSpec((tm, d), lambda i, j: (i, 0)),
            pl.BlockSpec((1, d), lambda i, j: (0, 0)),
            pl.BlockSpec((d, tn), lambda i, j: (0, j)),
        ],
        out_specs=out_specs,
        out_shape=out_shape,
        scratch_shapes=[pltpu.VMEM((tm, d), BF16), pltpu.VMEM((n_chunks, tm, LANE), F32)],
        compiler_params=_params("parallel", "arbitrary"),
        name="even_norm_proj",
    )(x, gain.reshape(1, d), w)


def _rope_chunk(y, hg, cosd, sins, even_lane):
    y = _rms(y, hg)
    partner = jnp.where(even_lane, pltpu.roll(y, LANE - 1, axis=1), pltpu.roll(y, 1, axis=1))
    return y * cosd + partner * sins


def _odd_proj_kernel(x_ref, g_ref, w_ref, hg_ref, cos_ref, sin_ref, o_ref, xn_ref, *,
                     n_chunks, n_rope, n_tiles, query_chunks):
    j = pl.program_id(1)

    @pl.when(j == 0)
    def _():
        xn_ref[...] = _rms(x_ref[...], g_ref[...]).astype(BF16)

    acc = _dot(xn_ref[...], w_ref[...])

    for jt in range(n_tiles):
        @pl.when(j == jt)
        def _(jt=jt):
            if jt * n_chunks < n_rope:
                cosd = cos_ref[...]
                sins = sin_ref[...]
                even_lane = (lax.broadcasted_iota(jnp.int32, cosd.shape, 1) % 2) == 0
            for c in range(n_chunks):
                gc = jt * n_chunks + c
                y = acc[:, c * LANE:(c + 1) * LANE]
                if gc < n_rope:
                    y = _rope_chunk(y, hg_ref[gc:gc + 1, :], cosd, sins, even_lane)
                if gc in query_chunks:
                    y = y * QUERY_SCALE
                o_ref[c] = y.astype(o_ref.dtype)


def odd_norm_proj(x, gain, w, head_gain, cos_dup, sin_signed, *, query_chunks, tm, tn):
    t, d = x.shape
    n = w.shape[1]
    assert t % tm == 0 and n % tn == 0 and tn % LANE == 0
    w = w.astype(BF16)
    n_chunks = tn // LANE
    n_rope = head_gain.shape[0]
    s = cos_dup.shape[0]
    assert s % tm == 0
    sb = s // tm
    return pl.pallas_call(
        functools.partial(_odd_proj_kernel, n_chunks=n_chunks, n_rope=n_rope, n_tiles=n // tn,
                          query_chunks=frozenset(query_chunks)),
        grid=(t // tm, n // tn),
        in_specs=[
            pl.BlockSpec((tm, d), lambda i, j: (i, 0)),
            pl.BlockSpec((1, d), lambda i, j: (0, 0)),
            pl.BlockSpec((d, tn), lambda i, j: (0, j)),
            pl.BlockSpec((n_rope, LANE), lambda i, j: (0, 0)),
            pl.BlockSpec((tm, LANE), lambda i, j: (i % sb, 0)),
            pl.BlockSpec((tm, LANE), lambda i, j: (i % sb, 0)),
        ],
        out_specs=pl.BlockSpec((n_chunks, tm, LANE), lambda i, j: (j, i, 0)),
        out_shape=jax.ShapeDtypeStruct((n // LANE, t, LANE), BF16),
        scratch_shapes=[pltpu.VMEM((tm, d), BF16)],
        compiler_params=_params("parallel", "arbitrary"),
        name="odd_norm_proj",
    )(x, gain.reshape(1, d), w, head_gain, cos_dup, sin_signed)


A_TQ = 256
A_BLOCKS_PER_ITER = 16


def _a_geometry(seq, window, dilation):
    sub = seq // dilation
    half = window // (2 * dilation)
    tq = min(A_TQ, sub)
    win = min(tq + 2 * half, sub)
    assert seq % dilation == 0 and sub % tq == 0 and half % 16 == 0 and win % LANE == 0
    assert win == sub or win == tq + 2 * half
    return sub, half, tq, win


def _dilated_kernel(*refs, geoms):
    ng = len(geoms)
    qkv = refs[:3 * ng]
    bias_rows = refs[3 * ng:4 * ng]
    o_ref = refs[4 * ng]
    m_s, l_s, acc_s = refs[4 * ng + 1:4 * ng + 4]
    bms = refs[4 * ng + 4:]

    for g in range(ng):
        _, _, _, tq, win = geoms[g]
        for variant in range(bms[g].shape[0]):
            bms[g][variant] = _circulant(bias_rows[g][variant, 0], tq)[:, :win]

    order = sorted(range(ng), key=lambda g: -geoms[g][0])
    for pos, g in enumerate(order):
        dilation, sub, half, tq, win = geoms[g]
        q_ref, k_ref, v_ref = qkv[3 * g:3 * g + 3]
        bm_ref = bms[g]
        nq = sub // tq
        first = pos == 0
        nblocks = dilation * nq
        batch = math.gcd(A_BLOCKS_PER_ITER, nblocks)

        def one_block(it, dilation=dilation, sub=sub, half=half, tq=tq, win=win, nq=nq,
                      q_ref=q_ref, k_ref=k_ref, v_ref=v_ref, bm_ref=bm_ref, first=first):
            r = it // nq
            qi = it % nq
            q0 = pl.multiple_of(qi * tq, tq)
            kstart = pl.multiple_of(jnp.clip(q0 - half, 0, sub - win), half)
            if dilation == 1:
                q = q_ref[0, pl.ds(q0, tq), :]
                kw = k_ref[0, pl.ds(kstart, win), :]
                vw = v_ref[0, pl.ds(kstart, win), :]
                rows = pl.ds(q0, tq)
            else:
                q = q_ref[0, 0, r, pl.ds(q0, tq), :]
                kw = k_ref[0, 0, r, pl.ds(kstart, win), :]
                vw = v_ref[0, 0, r, pl.ds(kstart, win), :]
                rows = pl.ds(q0 * dilation + r, tq, stride=dilation)
            variant = 0 if nq == 1 else jnp.where(qi == 0, 0, jnp.where(qi == nq - 1, 2, 1))
            s = _dot_nt(q, kw) + bm_ref[variant]
            mb = jnp.max(s, axis=-1, keepdims=True)
            if first:
                p = jnp.exp2(s - mb)
                return (rows, jnp.broadcast_to(mb, (tq, LANE)),
                        jnp.broadcast_to(jnp.sum(p, axis=-1, keepdims=True), (tq, LANE)),
                        _dot(p.astype(BF16), vw))
            m_old = m_s[rows, :]
            m_new = jnp.maximum(m_old, mb)
            alpha = jnp.exp2(m_old - m_new)
            p = jnp.exp2(s - _lanes(m_new, win))
            return (rows, m_new, alpha * l_s[rows, :] + jnp.sum(p, axis=-1, keepdims=True),
                    alpha * acc_s[rows, :] + _dot(p.astype(BF16), vw))

        def blocks(step, carry, batch=batch, one_block=one_block):
            results = [one_block(step * batch + i) for i in range(batch)]
            for rows, m_new, l_new, acc_new in results:
                m_s[rows, :] = m_new
                l_s[rows, :] = l_new
                acc_s[rows, :] = acc_new
            return carry

        lax.fori_loop(0, nblocks // batch, blocks, 0)

    o_ref[...] = (acc_s[...] / l_s[...]).astype(o_ref.dtype)


def _dilated_bias_rows(t5_cols, dilation, half, tq, win, sub):
    nq = sub // tq
    offs = [0] if nq == 1 else [0, -half, tq - win]
    width = -(-(tq + win - 1) // LANE) * LANE
    x, used = _circulant_positions(win, tq, width)
    rows = []
    for off in offs:
        rel = x + off
        ok = jnp.asarray(used & (np.abs(rel) <= half))[None]
        bias = LOG2E * t5_cols[_t5_bucket(jnp.asarray(dilation * rel, jnp.int32))].T.astype(F32)
        rows.append(jnp.where(ok, bias, NEG_INF))
    return jnp.stack(rows)[:, :, None, :]


def dilated_attention(nat, dil_projs, t5_table, *, batch, seq):
    t = batch * seq
    geoms, args, in_specs, bias_rows, tables = [], [], [], [], []
    for gi, (window, dilation) in enumerate(DILATED_PATTERNS):
        sub, half, tq, win = _a_geometry(seq, window, dilation)
        geoms.append((dilation, sub, half, tq, win))
        for part in range(3):
            if dilation == 1:
                args.append(nat)
                in_specs.append(pl.BlockSpec(
                    (1, seq, LANE), lambda b, h, part=part: (part * A_HEADS + h, b, 0)))
            else:
                args.append(dil_projs[gi - 1])
                in_specs.append(pl.BlockSpec(
                    (1, 1, dilation, sub, LANE),
                    lambda b, h, part=part: (part * A_HEADS + h, b, 0, 0, 0)))
        bias_rows.append(_dilated_bias_rows(t5_table[:, gi * A_HEADS:(gi + 1) * A_HEADS],
                                            dilation, half, tq, win, sub))
        tables.append(pltpu.VMEM((bias_rows[-1].shape[0], tq, win), F32))
    for rows in bias_rows:
        args.append(rows)
        in_specs.append(pl.BlockSpec((rows.shape[0], 1) + rows.shape[2:], lambda b, h: (0, h, 0, 0)))
    return pl.pallas_call(
        functools.partial(_dilated_kernel, geoms=tuple(geoms)),
        grid=(batch, A_HEADS),
        in_specs=in_specs,
        out_specs=pl.BlockSpec((seq, LANE), lambda b, h: (b, h)),
        out_shape=jax.ShapeDtypeStruct((t, A_HEADS * LANE), BF16),
        scratch_shapes=[pltpu.VMEM((seq, LANE), F32)] * 3 + tables,
        compiler_params=_params("parallel", "arbitrary"),
        name="dilated_attn",
    )(*args)


KEY_SPLITS = 4
FULL_ATTN_TQ = 256


def _partial_softmax(s, v):
    m = jnp.max(s, axis=-1, keepdims=True)
    p = jnp.exp2(s - m)
    return m, jnp.sum(p, axis=-1, keepdims=True), _dot(p.astype(BF16), v)


def _merge_partials(parts):
    m = parts[0][0]
    for pm, _, _ in parts[1:]:
        m = jnp.maximum(m, pm)
    weights = [jnp.exp2(pm - m) for pm, _, _ in parts]
    denom = sum(w * pl_ for w, (_, pl_, _) in zip(weights, parts))
    acc = sum(w * pa for w, (_, _, pa) in zip(weights, parts))
    return acc / denom


T5_BAND = T5_MAX_DIST


def _diff_band(tq):
    lo = -((T5_BAND + LANE - 1) // LANE) - 1
    hi = (T5_BAND + tq - 1 + LANE - 1) // LANE
    return lo, hi


def _diff_kernel(q_ref, k_ref, v_ref, row_ref, lq1_ref, lk1_ref, lq2_ref, lk2_ref, sub_ref, o_ref,
                 tz_ref, *, tq, seq, lambda_init):
    qi = pl.program_id(2)
    lo, hi = _diff_band(tq)
    q_chunk0 = qi * (tq // LANE)
    span = seq // KEY_SPLITS

    @pl.when(qi == 0)
    def _():
        band = _circulant(row_ref[0], tq)
        for j in range(hi - lo + 1):
            tz_ref[j] = band[:, j * LANE:(j + 1) * LANE]

    def chain(mi, part):
        k0 = part * span
        bias = jnp.concatenate(
            [tz_ref[jnp.clip(cc - q_chunk0, lo, hi) - lo]
             for cc in range(k0 // LANE, (k0 + span) // LANE)], axis=1)
        s = _dot_nt(q_ref[mi], k_ref[mi, k0:k0 + span, :]) + bias
        vc = jnp.concatenate([v_ref[0, k0:k0 + span, :], v_ref[1, k0:k0 + span, :]], axis=1)
        return _partial_softmax(s, vc)

    chains = {(mi, part): chain(mi, part) for part in range(KEY_SPLITS) for mi in range(2)}

    def attend(mi):
        return _merge_partials([chains[mi, part] for part in range(KEY_SPLITS)])

    lam = (jnp.exp(jnp.sum(lq1_ref[...] * lk1_ref[...], axis=-1, keepdims=True))
           - jnp.exp(jnp.sum(lq2_ref[...] * lk2_ref[...], axis=-1, keepdims=True)) + lambda_init)
    o = attend(0) - lam * attend(1)
    o_ref[...] = (_rms(o, sub_ref[...]) * (1.0 - lambda_init)).astype(o_ref.dtype)


def _diff_bias_rows(t5_cols, tq):
    lo, hi = _diff_band(tq)
    cols = (hi - lo + 1) * LANE
    width = -(-(tq + cols - 1) // LANE) * LANE
    x, _ = _circulant_positions(cols, tq, width)
    rel = jnp.asarray(x + LANE * lo, jnp.int32)
    return (LOG2E * t5_cols[_t5_bucket(rel)].T.astype(F32))[:, None, :]


def diff_attention(proj, t5_cols, lq1, lk1, lq2, lk2, subln, *, chunk0, batch, seq, lambda_init):
    t = proj.shape[1]
    tq = FULL_ATTN_TQ
    assert seq % tq == 0 and seq % (KEY_SPLITS * LANE) == 0 and chunk0 % 2 == 0
    nq = seq // tq
    bias_rows = _diff_bias_rows(t5_cols, tq)
    lo, hi = _diff_band(tq)
    qb, kb, vb = chunk0 // 2, chunk0 // 2 + B_HEADS, chunk0 // 2 + 2 * B_HEADS
    vec = lambda a: a.reshape(1, -1).astype(F32)
    vspec = lambda n: pl.BlockSpec((1, n), lambda b, h, qi: (0, 0))
    return pl.pallas_call(
        functools.partial(_diff_kernel, tq=tq, seq=seq, lambda_init=lambda_init),
        grid=(batch, B_HEADS, nq),
        in_specs=[
            pl.BlockSpec((2, tq, LANE), lambda b, h, qi: (qb + h, b * nq + qi, 0)),
            pl.BlockSpec((2, seq, LANE), lambda b, h, qi: (kb + h, b, 0)),
            pl.BlockSpec((2, seq, LANE), lambda b, h, qi: (vb + h, b, 0)),
            pl.BlockSpec((1, 1, bias_rows.shape[2]), lambda b, h, qi: (h, 0, 0)),
            vspec(B_QK_DIM), vspec(B_QK_DIM), vspec(B_QK_DIM), vspec(B_QK_DIM), vspec(B_V_DIM),
        ],
        out_specs=pl.BlockSpec((tq, B_V_DIM), lambda b, h, qi: (b * nq + qi, h)),
        out_shape=jax.ShapeDtypeStruct((t, B_HEADS * B_V_DIM), BF16),
        scratch_shapes=[pltpu.VMEM((hi - lo + 1, tq, LANE), F32)],
        compiler_params=_params("parallel", "parallel", "arbitrary"),
        name="diff_attn",
    )(proj, proj, proj, bias_rows, vec(lq1), vec(lk1), vec(lq2), vec(lk2), vec(subln))


def _gqa_kernel(q_ref, k_ref, v_ref, o_ref, *, tq, seq, group):
    span = seq // KEY_SPLITS
    bounds = [0] + [span // 2 + i * span for i in range(KEY_SPLITS)] + [seq]
    q = q_ref[...].reshape(group * tq, HEAD_DIM)
    parts = []
    for k0, k1 in zip(bounds[:-1], bounds[1:]):
        parts.append(_partial_softmax(_dot_nt(q, k_ref[0, k0:k1, :]), v_ref[0, k0:k1, :]))
    o = _merge_partials(parts)
    for g in range(group):
        o_ref[:, g * HEAD_DIM:(g + 1) * HEAD_DIM] = o[g * tq:(g + 1) * tq].astype(o_ref.dtype)


def gqa_attention(proj, *, batch, seq):
    t = proj.shape[1]
    tq = FULL_ATTN_TQ
    assert seq % tq == 0 and seq % (2 * KEY_SPLITS * LANE) == 0
    group = C_HEADS // C_KV_HEADS
    nq = seq // tq
    return pl.pallas_call(
        functools.partial(_gqa_kernel, tq=tq, seq=seq, group=group),
        grid=(batch, C_KV_HEADS, nq),
        in_specs=[
            pl.BlockSpec((group, tq, LANE), lambda b, n, qi: (n, b * nq + qi, 0)),
            pl.BlockSpec((1, seq, LANE), lambda b, n, qi: (C_HEADS + n, b, 0)),
            pl.BlockSpec((1, seq, LANE), lambda b, n, qi: (C_HEADS + C_KV_HEADS + n, b, 0)),
        ],
        out_specs=pl.BlockSpec((tq, group * HEAD_DIM), lambda b, n, qi: (b * nq + qi, n)),
        out_shape=jax.ShapeDtypeStruct((t, C_HEADS * HEAD_DIM), BF16),
        compiler_params=_params("parallel", "parallel", "arbitrary"),
        name="gqa_attn",
    )(proj, proj, proj)


NA_QROWS = 4
NA_KROWS = NA_QROWS + NA_ROWS
NA_BLOCKS_PER_STEP = 8


def _na_kernel(q_ref, k_ref, v_ref, rows_ref, o_ref, bm_ref, *, rows, nsub):
    step = pl.program_id(2)
    tq = NA_QROWS * GRID_W
    nkeys = NA_KROWS * GRID_W
    nblk = rows // NA_QROWS
    kr = min(NA_ROWS, rows)

    @pl.when(step == 0)
    def _():
        lane = lax.broadcasted_iota(jnp.int32, (GRID_W, LANE), 1)
        qj = lax.broadcasted_iota(jnp.int32, (GRID_W, LANE), 0)
        kj = lane % GRID_W
        cs = jnp.clip(qj - NA_COLS // 2, 0, GRID_W - NA_COLS)
        col_ok = (kj >= cs) & (kj < cs + NA_COLS)
        left = lane < GRID_W
        neg = jnp.full((GRID_W, LANE), NEG_INF, F32)
        bands = {}

        def band(dr, right):
            if (dr, right) not in bands:
                wide = jnp.broadcast_to(rows_ref[0, dr], (GRID_W, LANE))
                rolled = pltpu.roll(wide, GRID_W if right else 0, 1, stride=1, stride_axis=0)
                bands[dr, right] = jnp.where(col_ok, rolled, NEG_INF)
            return bands[dr, right]

        for variant, blk in enumerate((0, 1, nblk - 1)):
            wrow = min(max(blk * NA_QROWS - NA_ROWS // 2, 0), rows - NA_KROWS)
            for a in range(NA_QROWS):
                qi = blk * NA_QROWS + a
                rs = min(max(qi - kr // 2, 0), rows - kr)
                for pair in range(NA_KROWS // 2):
                    halves = []
                    for right in (False, True):
                        ki = wrow + 2 * pair + int(right)
                        halves.append(band(ki - qi + NA_ROWS - 1, right) if rs <= ki < rs + kr else neg)
                    bm_ref[variant, a * GRID_W:(a + 1) * GRID_W, pair * LANE:(pair + 1) * LANE] = (
                        jnp.where(left, halves[0], halves[1]))

    for c in range(nsub):
        blk = step * nsub + c
        wrow = jnp.clip(blk * NA_QROWS - NA_ROWS // 2, 0, rows - NA_KROWS)
        kstart = pl.multiple_of(wrow * GRID_W, NA_QROWS * GRID_W)
        kw = k_ref[0, pl.ds(kstart, nkeys), :]
        vw = v_ref[0, pl.ds(kstart, nkeys), :]
        variant = jnp.where(blk == 0, 0, jnp.where(blk == nblk - 1, 2, 1))
        s = _dot_nt(q_ref[0, c * tq:(c + 1) * tq, :], kw) + bm_ref[variant]
        m = jnp.max(s, axis=-1, keepdims=True)
        p = jnp.exp2(s - m)
        denom = jnp.sum(p, axis=-1, keepdims=True)
        o_ref[c * tq:(c + 1) * tq, :] = (_dot(p.astype(BF16), vw) / denom).astype(o_ref.dtype)


def _na_bias_rows(rpb):
    r = LOG2E * rpb.astype(F32)
    gap = jnp.zeros(r.shape[:-1] + (LANE - r.shape[-1],), F32)
    return jnp.concatenate([r[..., NA_COLS - 1:], gap, r[..., :NA_COLS - 1]], axis=-1)[:, :, None, :]


def na_attention(proj, rpb, *, chunk0, batch, seq):
    t = proj.shape[1]
    rows = seq // GRID_W
    nsub = NA_BLOCKS_PER_STEP
    assert rows % (NA_QROWS * nsub) == 0 and rows >= NA_KROWS and rows // NA_QROWS >= 3
    assert NA_KROWS % 2 == 0 and 2 * GRID_W == LANE and 2 * NA_COLS - 1 <= LANE
    nblk = rows // NA_QROWS
    nstep = nblk // nsub
    tq = NA_QROWS * GRID_W
    nkeys = NA_KROWS * GRID_W
    bias_rows = _na_bias_rows(rpb)
    return pl.pallas_call(
        functools.partial(_na_kernel, rows=rows, nsub=nsub),
        grid=(batch, D_HEADS, nstep),
        in_specs=[
            pl.BlockSpec((1, nsub * tq, LANE), lambda b, h, step: (chunk0 + h, b * nstep + step, 0)),
            pl.BlockSpec((1, seq, LANE), lambda b, h, step: (chunk0 + D_HEADS + h, b, 0)),
            pl.BlockSpec((1, seq, LANE), lambda b, h, step: (chunk0 + 2 * D_HEADS + h, b, 0)),
            pl.BlockSpec((1,) + bias_rows.shape[1:], lambda b, h, step: (h, 0, 0, 0)),
        ],
        out_specs=pl.BlockSpec((nsub * tq, LANE), lambda b, h, step: (b * nstep + step, h)),
        out_shape=jax.ShapeDtypeStruct((t, D_HEADS * HEAD_DIM), BF16),
        scratch_shapes=[pltpu.VMEM((3, tq, nkeys), F32)],
        compiler_params=_params("parallel", "parallel", "arbitrary"),
        name="na_attn",
    )(proj, proj, proj, bias_rows)


def _out_proj_kernel(a_ref, b_ref, w_ref, h_ref, out_ref, *, na):
    out_ref[...] = h_ref[...] + _dot(a_ref[...], w_ref[:na, :]) + _dot(b_ref[...], w_ref[na:, :])


def out_proj(a, b, w, h, *, tm, tn):
    t, d = h.shape
    na, nb = a.shape[1], b.shape[1]
    w = w.astype(BF16)
    return pl.pallas_call(
        functools.partial(_out_proj_kernel, na=na),
        grid=(t // tm, d // tn),
        in_specs=[pl.BlockSpec((tm, na), lambda i, j: (i, 0)),
                  pl.BlockSpec((tm, nb), lambda i, j: (i, 0)),
                  pl.BlockSpec((na + nb, tn), lambda i, j: (0, j)),
                  pl.BlockSpec((tm, tn), lambda i, j: (i, j))],
        out_specs=pl.BlockSpec((tm, tn), lambda i, j: (i, j)),
        out_shape=jax.ShapeDtypeStruct((t, d), F32),
        compiler_params=_params("parallel", "arbitrary"),
        name="out_proj",
    )(a, b, w, h)


FFN_HALO = 16


def _gelu_tanh(x):
    return 0.5 * x * (1.0 + jnp.tanh(math.sqrt(2.0 / math.pi) * (x + 0.044715 * (x * x * x))))


def _ffn_kernel(x_ref, xp_ref, xnx_ref, g_ref, wg_ref, wu_ref, cw_ref, cb_ref, wd_ref, gf_ref, out_ref,
                xn_ref, *, tm, tiles_per_seq, final_norm):
    i = pl.program_id(0)
    f = pl.program_id(1)
    nf = pl.num_programs(1)
    hl = FFN_HALO

    @pl.when(f == 0)
    def _():
        gain = g_ref[...]
        keep_prev = jnp.where(i % tiles_per_seq != 0, 1.0, 0.0)
        keep_next = jnp.where(i % tiles_per_seq != tiles_per_seq - 1, 1.0, 0.0)
        xn_ref[:hl, :] = (_rms(xp_ref[...], gain) * keep_prev).astype(BF16)
        xn_ref[hl:hl + tm, :] = _rms(x_ref[...], gain).astype(BF16)
        xn_ref[hl + tm:, :] = (_rms(xnx_ref[...], gain) * keep_next).astype(BF16)
        out_ref[...] = x_ref[...]

    ge = _dot(xn_ref[...], wg_ref[...])
    cw = cw_ref[...]
    g = (cw[0:1, :] * ge[hl - 1:hl - 1 + tm] + cw[1:2, :] * ge[hl:hl + tm]
         + cw[2:3, :] * ge[hl + 1:hl + 1 + tm] + cb_ref[...])
    u = _dot(xn_ref[hl:hl + tm, :], wu_ref[...])
    act = (_gelu_tanh(g) * u).astype(BF16)
    out_ref[...] += _dot(act, wd_ref[...])

    if final_norm:
        @pl.when(f == nf - 1)
        def _():
            out_ref[...] = _rms(out_ref[...], gf_ref[...])


def conv_ffn_block(h, gain, w_up, conv_w, conv_b, w_down, final_gain, *, seq, tm, tf, final_norm):
    t, d = h.shape
    ff = w_down.shape[0]
    assert t % tm == 0 and seq % tm == 0 and ff % tf == 0 and tm % FFN_HALO == 0
    nfb = ff // tf
    w_up = w_up.astype(BF16)
    w_down = w_down.astype(BF16)
    hb = tm // FFN_HALO
    last_halo = t // FFN_HALO - 1
    return pl.pallas_call(
        functools.partial(_ffn_kernel, tm=tm, tiles_per_seq=seq // tm, final_norm=final_norm),
        grid=(t // tm, nfb),
        in_specs=[
            pl.BlockSpec((tm, d), lambda i, f: (i, 0), pipeline_mode=pl.Buffered(1)),
            pl.BlockSpec((FFN_HALO, d), lambda i, f: (jnp.maximum(i * hb - 1, 0), 0)),
            pl.BlockSpec((FFN_HALO, d), lambda i, f: (jnp.minimum((i + 1) * hb, last_halo), 0)),
            pl.BlockSpec((1, d), lambda i, f: (0, 0)),
            pl.BlockSpec((d, tf), lambda i, f: (0, f)),
            pl.BlockSpec((d, tf), lambda i, f: (0, nfb + f)),
            pl.BlockSpec((3, tf), lambda i, f: (0, f)),
            pl.BlockSpec((1, tf), lambda i, f: (0, f)),
            pl.BlockSpec((tf, d), lambda i, f: (f, 0)),
            pl.BlockSpec((1, d), lambda i, f: (0, 0)),
        ],
        out_specs=pl.BlockSpec((tm, d), lambda i, f: (i, 0)),
        out_shape=jax.ShapeDtypeStruct((t, d), F32),
        scratch_shapes=[pltpu.VMEM((tm + 2 * FFN_HALO, d), BF16)],
        compiler_params=_params("parallel", "arbitrary"),
        name="conv_ffn",
    )(h, h, h, gain.reshape(1, d), w_up, w_up, conv_w, conv_b.reshape(1, ff), w_down,
      final_gain.reshape(1, d))


def _rope_tables(seq):
    tpos = jnp.arange(seq, dtype=jnp.int32)
    row = (tpos // GRID_W).astype(F32)
    col = (tpos % GRID_W).astype(F32)
    inv_freq = ROPE_THETA ** (-(jnp.arange(0, ROPE_AXIS_DIM, 2, dtype=F32) / ROPE_AXIS_DIM))
    ang = jnp.concatenate([row[:, None] * inv_freq[None], col[:, None] * inv_freq[None]], axis=-1)
    cos, sin = jnp.cos(ang), jnp.sin(ang)
    cos_dup = jnp.repeat(cos, 2, axis=-1)
    sin_signed = jnp.stack([-sin, sin], axis=-1).reshape(seq, HEAD_DIM)
    return cos_dup, sin_signed


def kernel(x, ln_mix, ln_ffn, ln_final, t5_table, ev_w_in, ev_w_out, diff_lq1, diff_lk1, diff_lq2,
           diff_lk2, diff_subln, od_w_in, od_w_out, gqa_q_norm, gqa_k_norm, na_rpb, ffn_w_up,
           ffn_conv_w, ffn_conv_b, ffn_w_down):
    batch, seq, d = x.shape
    depth = ln_mix.shape[0]
    t = batch * seq
    h = x.reshape(t, d)
    proj_tm, even_tn, odd_tn = 1024, A_HEADS * HEAD_DIM, 12 * LANE
    out_tiles = dict(tm=512, tn=d)
    ffn_tiles = dict(tm=1024, tf=512)

    for layer in range(depth):
        if layer % 2 == 0:
            e = layer // 2
            lambda_init = 0.8 - 0.6 * math.exp(-0.3 * layer)
            nat, *dil_projs = even_norm_proj(h, ln_mix[layer], ev_w_in[e],
                                             batch=batch, seq=seq, tm=proj_tm, tn=even_tn)
            oa = dilated_attention(nat, dil_projs, t5_table, batch=batch, seq=seq)
            ob = diff_attention(nat, t5_table[:, A_GROUPS * A_HEADS:], diff_lq1[e], diff_lk1[e],
                                diff_lq2[e], diff_lk2[e], diff_subln[e], chunk0=3 * A_HEADS,
                                batch=batch, seq=seq, lambda_init=lambda_init)
            h = out_proj(oa, ob, ev_w_out[e], h, **out_tiles)
        else:
            o = layer // 2
            n_rope = C_HEADS + C_KV_HEADS
            head_gain = jnp.concatenate([jnp.tile(gqa_q_norm[o][None], (C_HEADS, 1)),
                                         jnp.tile(gqa_k_norm[o][None], (C_KV_HEADS, 1))], axis=0)
            cos_dup, sin_signed = _rope_tables(seq)
            na_chunk0 = n_rope + C_KV_HEADS
            query_chunks = list(range(C_HEADS)) + list(range(na_chunk0, na_chunk0 + D_HEADS))
            proj = odd_norm_proj(h, ln_mix[layer], od_w_in[o], head_gain.astype(F32),
                                 cos_dup, sin_signed, query_chunks=query_chunks, tm=proj_tm, tn=odd_tn)
            oc = gqa_attention(proj, batch=batch, seq=seq)
            od = na_attention(proj, na_rpb[o], chunk0=na_chunk0, batch=batch, seq=seq)
            h = out_proj(oc, od, od_w_out[o], h, **out_tiles)
        h = conv_ffn_block(h, ln_ffn[layer], ffn_w_up[layer], ffn_conv_w[layer],
                           ffn_conv_b[layer], ffn_w_down[layer], ln_final,
                           seq=seq, final_norm=(layer == depth - 1), **ffn_tiles)
    return h.reshape(batch, seq, d)
```

```python
import functools
import math

import numpy as np
import jax
import jax.numpy as jnp
from jax import lax
from jax.experimental import pallas as pl
from jax.experimental.pallas import tpu as pltpu

HEAD_DIM = 128
GRID_W = 64
RMS_EPS = 1e-6
NEG_INF = -1e30
T5_BUCKETS = 32
T5_MAX_DIST = 1024
DILATED_PATTERNS = ((128, 1), (512, 4), (2048, 16))
A_GROUPS = len(DILATED_PATTERNS)
A_HEADS = 8
B_HEADS = 4
B_QK_DIM = 128
B_V_DIM = 2 * B_QK_DIM
C_HEADS = 8
C_KV_HEADS = 2
ROPE_THETA = 10000.0
ROPE_AXIS_DIM = HEAD_DIM // 2
D_HEADS = 8
NA_ROWS = 8
NA_COLS = 16

LANE = 128
V7X_VMEM_LIMIT = 56 * 1024 * 1024
V7X_VMEM_LIMIT_FFN = 61 * 1024 * 1024

BF16 = jnp.bfloat16
F32 = jnp.float32

LOG2E = math.log2(math.e)
QUERY_SCALE = HEAD_DIM ** -0.5 * LOG2E


def _params(*sem, vmem_limit=V7X_VMEM_LIMIT):
    return pltpu.CompilerParams(dimension_semantics=sem, vmem_limit_bytes=vmem_limit)


def _rms(xf, gain):
    ms = jnp.mean(xf * xf, axis=-1, keepdims=True)
    return xf * lax.rsqrt(ms + RMS_EPS) * gain


def _dot_nt(a, b):
    return lax.dot_general(a, b, (((1,), (1,)), ((), ())), preferred_element_type=F32)


def _dot(a, b):
    return jnp.dot(a, b, preferred_element_type=F32)


def _circulant(row, rows):
    return pltpu.roll(jnp.broadcast_to(row, (rows, row.shape[1])), 0, 1, stride=1, stride_axis=0)


def _circulant_positions(cols, rows, width):
    assert width >= rows + cols - 1
    p = np.arange(width)
    x = np.where(p < cols, p, p - width)
    return x, (p < cols) | (p > width - rows)


def _lanes(x, width):
    return x if width == LANE else jnp.concatenate([x] * (width // LANE), axis=1)


def _t5_bucket(rel):
    nb = T5_BUCKETS // 2
    max_exact = nb // 2
    ret = jnp.where(rel > 0, nb, 0)
    n = jnp.abs(rel)
    n_f = jnp.maximum(n, 1).astype(F32)
    large = max_exact + (jnp.log(n_f / max_exact) / math.log(T5_MAX_DIST / max_exact)
                         * (nb - max_exact)).astype(jnp.int32)
    large = jnp.minimum(large, nb - 1)
    return ret + jnp.where(n < max_exact, n, large)


def _store_chunks(acc, o_ref, n_chunks):
    for c in range(n_chunks):
        o_ref[c] = acc[:, c * LANE:(c + 1) * LANE].astype(o_ref.dtype)


def _even_proj_kernel(x_ref, g_ref, w_ref, nat_ref, *rest, plan, n_chunks, tm, qkv_period):
    dil_refs, (xn_ref, slab_ref) = rest[:-2], rest[-2:]
    j = pl.program_id(1)

    @pl.when(j == 0)
    def _():
        xn_ref[...] = _rms(x_ref[...], g_ref[...]).astype(BF16)

    factor = jnp.where(j % qkv_period == 0, QUERY_SCALE, 1.0)

    for lo, hi, dilation, out_idx in plan:
        @pl.when((j >= lo) & (j < hi))
        def _(dilation=dilation, out_idx=out_idx):
            acc = _dot(xn_ref[...], w_ref[...]) * factor
            if dilation == 1:
                _store_chunks(acc, nat_ref, n_chunks)
                return
            o_ref = dil_refs[out_idx]
            per = tm // dilation
            for c in range(n_chunks):
                slab_ref[c] = acc[:, c * LANE:(c + 1) * LANE]
            for c in range(n_chunks):
                for r in range(dilation):
                    o_ref[c, 0, r] = slab_ref[c, pl.ds(r, per, stride=dilation), :].astype(o_ref.dtype)


def even_norm_proj(x, gain, w, *, batch, seq, tm, tn):
    t, d = x.shape
    n = w.shape[1]
    w = w.astype(BF16)
    n_chunks = tn // LANE
    group_cols = 3 * A_HEADS * HEAD_DIM
    assert group_cols % tn == 0 and n % tn == 0 and seq % tm == 0 and t == batch * seq
    assert tn == A_HEADS * HEAD_DIM == B_HEADS * 2 * B_QK_DIM and B_QK_DIM == HEAD_DIM
    gt = group_cols // tn
    nj = n // tn
    tiles_per_seq = seq // tm
    dils = [dil for _, dil in DILATED_PATTERNS]
    assert dils[0] == 1 and all(tm % (16 * dil) == 0 for dil in dils)
    plan = [(0, gt, 1, None)]
    for gi in range(1, A_GROUPS):
        plan.append((gi * gt, (gi + 1) * gt, dils[gi], gi - 1))
    plan.append((A_GROUPS * gt, nj, 1, None))
    nat_tiles = gt + (nj - A_GROUPS * gt)

    def nat_map(i, j):
        return (jnp.where(j < gt, j, jnp.where(j < A_GROUPS * gt, gt - 1, j - (A_GROUPS - 1) * gt)), i, 0)

    def dil_map(gi):
        return lambda i, j: (jnp.clip(j - gi * gt, 0, gt - 1), i // tiles_per_seq, 0, i % tiles_per_seq, 0)

    out_specs = [pl.BlockSpec((n_chunks, tm, LANE), nat_map)]
    out_shape = [jax.ShapeDtypeStruct((nat_tiles * n_chunks, t, LANE), BF16)]
    for gi in range(1, A_GROUPS):
        dil = dils[gi]
        out_specs.append(pl.BlockSpec((n_chunks, 1, dil, tm // dil, LANE), dil_map(gi)))
        out_shape.append(jax.ShapeDtypeStruct((gt * n_chunks, batch, dil, seq // dil, LANE), BF16))
    return pl.pallas_call(
        functools.partial(_even_proj_kernel, plan=tuple(plan), n_chunks=n_chunks, tm=tm, qkv_period=gt),
        grid=(t // tm, nj),
        in_specs=[
            pl.BlockSpec((tm, d), lambda i, j: (i, 0)),
            pl.BlockSpec((1, d), lambda i, j: (0, 0)),
            pl.BlockSpec((d, tn), lambda i, j: (0, j)),
        ],
        out_specs=out_specs,
        out_shape=out_shape,
        scratch_shapes=[pltpu.VMEM((tm, d), BF16), pltpu.VMEM((n_chunks, tm, LANE), F32)],
        compiler_params=_params("parallel", "arbitrary"),
        name="even_norm_proj",
    )(x, gain.reshape(1, d), w)


def _rope_chunk(y, hg, cosd, sins, even_lane):
    y = _rms(y, hg)
    partner = jnp.where(even_lane, pltpu.roll(y, LANE - 1, axis=1), pltpu.roll(y, 1, axis=1))
    return y * cosd + partner * sins


def _odd_proj_kernel(x_ref, g_ref, w_ref, hg_ref, cos_ref, sin_ref, o_ref, xn_ref, *,
                     n_chunks, n_rope, n_tiles, query_chunks):
    j = pl.program_id(1)

    @pl.when(j == 0)
    def _():
        xn_ref[...] = _rms(x_ref[...], g_ref[...]).astype(BF16)

    acc = _dot(xn_ref[...], w_ref[...])

    for jt in range(n_tiles):
        @pl.when(j == jt)
        def _(jt=jt):
            if jt * n_chunks < n_rope:
                cosd = cos_ref[...]
                sins = sin_ref[...]
                even_lane = (lax.broadcasted_iota(jnp.int32, cosd.shape, 1) % 2) == 0
            for c in range(n_chunks):
                gc = jt * n_chunks + c
                y = acc[:, c * LANE:(c + 1) * LANE]
                if gc < n_rope:
                    y = _rope_chunk(y, hg_ref[gc:gc + 1, :], cosd, sins, even_lane)
                if gc in query_chunks:
                    y = y * QUERY_SCALE
                o_ref[c] = y.astype(o_ref.dtype)


def odd_norm_proj(x, gain, w, head_gain, cos_dup, sin_signed, *, query_chunks, tm, tn):
    t, d = x.shape
    n = w.shape[1]
    assert t % tm == 0 and n % tn == 0 and tn % LANE == 0
    w = w.astype(BF16)
    n_chunks = tn // LANE
    n_rope = head_gain.shape[0]
    s = cos_dup.shape[0]
    assert s % tm == 0
    sb = s // tm
    return pl.pallas_call(
        functools.partial(_odd_proj_kernel, n_chunks=n_chunks, n_rope=n_rope, n_tiles=n // tn,
                          query_chunks=frozenset(query_chunks)),
        grid=(t // tm, n // tn),
        in_specs=[
            pl.BlockSpec((tm, d), lambda i, j: (i, 0)),
            pl.BlockSpec((1, d), lambda i, j: (0, 0)),
            pl.BlockSpec((d, tn), lambda i, j: (0, j)),
            pl.BlockSpec((n_rope, LANE), lambda i, j: (0, 0)),
            pl.BlockSpec((tm, LANE), lambda i, j: (i % sb, 0)),
            pl.BlockSpec((tm, LANE), lambda i, j: (i % sb, 0)),
        ],
        out_specs=pl.BlockSpec((n_chunks, tm, LANE), lambda i, j: (j, i, 0)),
        out_shape=jax.ShapeDtypeStruct((n // LANE, t, LANE), BF16),
        scratch_shapes=[pltpu.VMEM((tm, d), BF16)],
        compiler_params=_params("parallel", "arbitrary"),
        name="odd_norm_proj",
    )(x, gain.reshape(1, d), w, head_gain, cos_dup, sin_signed)


A_TQ = 256
A_BLOCKS_PER_ITER = 16


def _a_geometry(seq, window, dilation):
    sub = seq // dilation
    half = window // (2 * dilation)
    tq = min(A_TQ, sub)
    win = min(tq + 2 * half, sub)
    assert seq % dilation == 0 and sub % tq == 0 and half % 16 == 0 and win % LANE == 0
    assert win == sub or win == tq + 2 * half
    return sub, half, tq, win


def _dilated_kernel(*refs, geoms):
    ng = len(geoms)
    qkv = refs[:3 * ng]
    bias_rows = refs[3 * ng:4 * ng]
    o_ref = refs[4 * ng]
    m_s, l_s, acc_s = refs[4 * ng + 1:4 * ng + 4]
    bms = refs[4 * ng + 4:]

    for g in range(ng):
        _, _, _, tq, win = geoms[g]
        for variant in range(bms[g].shape[0]):
            bms[g][variant] = _circulant(bias_rows[g][variant, 0], tq)[:, :win]

    order = sorted(range(ng), key=lambda g: -geoms[g][0])
    for pos, g in enumerate(order):
        dilation, sub, half, tq, win = geoms[g]
        q_ref, k_ref, v_ref = qkv[3 * g:3 * g + 3]
        bm_ref = bms[g]
        nq = sub // tq
        first = pos == 0
        nblocks = dilation * nq
        batch = math.gcd(A_BLOCKS_PER_ITER, nblocks)

        def one_block(it, dilation=dilation, sub=sub, half=half, tq=tq, win=win, nq=nq,
                      q_ref=q_ref, k_ref=k_ref, v_ref=v_ref, bm_ref=bm_ref, first=first):
            r = it // nq
            qi = it % nq
            q0 = pl.multiple_of(qi * tq, tq)
            kstart = pl.multiple_of(jnp.clip(q0 - half, 0, sub - win), half)
            if dilation == 1:
                q = q_ref[0, pl.ds(q0, tq), :]
                kw = k_ref[0, pl.ds(kstart, win), :]
                vw = v_ref[0, pl.ds(kstart, win), :]
                rows = pl.ds(q0, tq)
            else:
                q = q_ref[0, 0, r, pl.ds(q0, tq), :]
                kw = k_ref[0, 0, r, pl.ds(kstart, win), :]
                vw = v_ref[0, 0, r, pl.ds(kstart, win), :]
                rows = pl.ds(q0 * dilation + r, tq, stride=dilation)
            variant = 0 if nq == 1 else jnp.where(qi == 0, 0, jnp.where(qi == nq - 1, 2, 1))
            s = _dot_nt(q, kw) + bm_ref[variant]
            mb = jnp.max(s, axis=-1, keepdims=True)
            if first:
                p = jnp.exp2(s - mb)
                return (rows, jnp.broadcast_to(mb, (tq, LANE)),
                        jnp.broadcast_to(jnp.sum(p, axis=-1, keepdims=True), (tq, LANE)),
                        _dot(p.astype(BF16), vw))
            m_old = m_s[rows, :]
            m_new = jnp.maximum(m_old, mb)
            alpha = jnp.exp2(m_old - m_new)
            p = jnp.exp2(s - _lanes(m_new, win))
            return (rows, m_new, alpha * l_s[rows, :] + jnp.sum(p, axis=-1, keepdims=True),
                    alpha * acc_s[rows, :] + _dot(p.astype(BF16), vw))

        def blocks(step, carry, batch=batch, one_block=one_block):
            results = [one_block(step * batch + i) for i in range(batch)]
            for rows, m_new, l_new, acc_new in results:
                m_s[rows, :] = m_new
                l_s[rows, :] = l_new
                acc_s[rows, :] = acc_new
            return carry

        lax.fori_loop(0, nblocks // batch, blocks, 0)

    o_ref[...] = (acc_s[...] / l_s[...]).astype(o_ref.dtype)


def _dilated_bias_rows(t5_cols, dilation, half, tq, win, sub):
    nq = sub // tq
    offs = [0] if nq == 1 else [0, -half, tq - win]
    width = -(-(tq + win - 1) // LANE) * LANE
    x, used = _circulant_positions(win, tq, width)
    rows = []
    for off in offs:
        rel = x + off
        ok = jnp.asarray(used & (np.abs(rel) <= half))[None]
        bias = LOG2E * t5_cols[_t5_bucket(jnp.asarray(dilation * rel, jnp.int32))].T.astype(F32)
        rows.append(jnp.where(ok, bias, NEG_INF))
    return jnp.stack(rows)[:, :, None, :]


def dilated_attention(nat, dil_projs, t5_table, *, batch, seq):
    t = batch * seq
    geoms, args, in_specs, bias_rows, tables = [], [], [], [], []
    for gi, (window, dilation) in enumerate(DILATED_PATTERNS):
        sub, half, tq, win = _a_geometry(seq, window, dilation)
        geoms.append((dilation, sub, half, tq, win))
        for part in range(3):
            if dilation == 1:
                args.append(nat)
                in_specs.append(pl.BlockSpec(
                    (1, seq, LANE), lambda b, h, part=part: (part * A_HEADS + h, b, 0)))
            else:
                args.append(dil_projs[gi - 1])
                in_specs.append(pl.BlockSpec(
                    (1, 1, dilation, sub, LANE),
                    lambda b, h, part=part: (part * A_HEADS + h, b, 0, 0, 0)))
        bias_rows.append(_dilated_bias_rows(t5_table[:, gi * A_HEADS:(gi + 1) * A_HEADS],
                                            dilation, half, tq, win, sub))
        tables.append(pltpu.VMEM((bias_rows[-1].shape[0], tq, win), F32))
    for rows in bias_rows:
        args.append(rows)
        in_specs.append(pl.BlockSpec((rows.shape[0], 1) + rows.shape[2:], lambda b, h: (0, h, 0, 0)))
    return pl.pallas_call(
        functools.partial(_dilated_kernel, geoms=tuple(geoms)),
        grid=(batch, A_HEADS),
        in_specs=in_specs,
        out_specs=pl.BlockSpec((seq, LANE), lambda b, h: (b, h)),
        out_shape=jax.ShapeDtypeStruct((t, A_HEADS * LANE), BF16),
        scratch_shapes=[pltpu.VMEM((seq, LANE), F32)] * 3 + tables,
        compiler_params=_params("parallel", "arbitrary"),
        name="dilated_attn",
    )(*args)


KEY_SPLITS = 4
FULL_ATTN_TQ = 256


def _partial_softmax(s, v):
    m = jnp.max(s, axis=-1, keepdims=True)
    p = jnp.exp2(s - m)
    return m, jnp.sum(p, axis=-1, keepdims=True), _dot(p.astype(BF16), v)


def _merge_partials(parts):
    m = parts[0][0]
    for pm, _, _ in parts[1:]:
        m = jnp.maximum(m, pm)
    weights = [jnp.exp2(pm - m) for pm, _, _ in parts]
    denom = sum(w * pl_ for w, (_, pl_, _) in zip(weights, parts))
    acc = sum(w * pa for w, (_, _, pa) in zip(weights, parts))
    return acc / denom


T5_BAND = T5_MAX_DIST


def _diff_band(tq):
    lo = -((T5_BAND + LANE - 1) // LANE) - 1
    hi = (T5_BAND + tq - 1 + LANE - 1) // LANE
    return lo, hi


def _diff_kernel(q_ref, k_ref, v_ref, row_ref, lq1_ref, lk1_ref, lq2_ref, lk2_ref, sub_ref, o_ref,
                 tz_ref, *, tq, seq, lambda_init):
    qi = pl.program_id(2)
    lo, hi = _diff_band(tq)
    q_chunk0 = qi * (tq // LANE)
    span = seq // KEY_SPLITS

    @pl.when(qi == 0)
    def _():
        band = _circulant(row_ref[0], tq)
        for j in range(hi - lo + 1):
            tz_ref[j] = band[:, j * LANE:(j + 1) * LANE]

    def chain(mi, part):
        k0 = part * span
        bias = jnp.concatenate(
            [tz_ref[jnp.clip(cc - q_chunk0, lo, hi) - lo]
             for cc in range(k0 // LANE, (k0 + span) // LANE)], axis=1)
        s = _dot_nt(q_ref[mi], k_ref[mi, k0:k0 + span, :]) + bias
        vc = jnp.concatenate([v_ref[0, k0:k0 + span, :], v_ref[1, k0:k0 + span, :]], axis=1)
        return _partial_softmax(s, vc)

    chains = {(mi, part): chain(mi, part) for part in range(KEY_SPLITS) for mi in range(2)}

    def attend(mi):
        return _merge_partials([chains[mi, part] for part in range(KEY_SPLITS)])

    lam = (jnp.exp(jnp.sum(lq1_ref[...] * lk1_ref[...], axis=-1, keepdims=True))
           - jnp.exp(jnp.sum(lq2_ref[...] * lk2_ref[...], axis=-1, keepdims=True)) + lambda_init)
    o = attend(0) - lam * attend(1)
    o_ref[...] = (_rms(o, sub_ref[...]) * (1.0 - lambda_init)).astype(o_ref.dtype)


def _diff_bias_rows(t5_cols, tq):
    lo, hi = _diff_band(tq)
    cols = (hi - lo + 1) * LANE
    width = -(-(tq + cols - 1) // LANE) * LANE
    x, _ = _circulant_positions(cols, tq, width)
    rel = jnp.asarray(x + LANE * lo, jnp.int32)
    return (LOG2E * t5_cols[_t5_bucket(rel)].T.astype(F32))[:, None, :]


def diff_attention(proj, t5_cols, lq1, lk1, lq2, lk2, subln, *, chunk0, batch, seq, lambda_init):
    t = proj.shape[1]
    tq = FULL_ATTN_TQ
    assert seq % tq == 0 and seq % (KEY_SPLITS * LANE) == 0 and chunk0 % 2 == 0
    nq = seq // tq
    bias_rows = _diff_bias_rows(t5_cols, tq)
    lo, hi = _diff_band(tq)
    qb, kb, vb = chunk0 // 2, chunk0 // 2 + B_HEADS, chunk0 // 2 + 2 * B_HEADS
    vec = lambda a: a.reshape(1, -1).astype(F32)
    vspec = lambda n: pl.BlockSpec((1, n), lambda b, h, qi: (0, 0))
    return pl.pallas_call(
        functools.partial(_diff_kernel, tq=tq, seq=seq, lambda_init=lambda_init),
        grid=(batch, B_HEADS, nq),
        in_specs=[
            pl.BlockSpec((2, tq, LANE), lambda b, h, qi: (qb + h, b * nq + qi, 0)),
            pl.BlockSpec((2, seq, LANE), lambda b, h, qi: (kb + h, b, 0)),
            pl.BlockSpec((2, seq, LANE), lambda b, h, qi: (vb + h, b, 0)),
            pl.BlockSpec((1, 1, bias_rows.shape[2]), lambda b, h, qi: (h, 0, 0)),
            vspec(B_QK_DIM), vspec(B_QK_DIM), vspec(B_QK_DIM), vspec(B_QK_DIM), vspec(B_V_DIM),
        ],
        out_specs=pl.BlockSpec((tq, B_V_DIM), lambda b, h, qi: (b * nq + qi, h)),
        out_shape=jax.ShapeDtypeStruct((t, B_HEADS * B_V_DIM), BF16),
        scratch_shapes=[pltpu.VMEM((hi - lo + 1, tq, LANE), F32)],
        compiler_params=_params("parallel", "parallel", "arbitrary"),
        name="diff_attn",
    )(proj, proj, proj, bias_rows, vec(lq1), vec(lk1), vec(lq2), vec(lk2), vec(subln))


def _gqa_kernel(q_ref, k_ref, v_ref, o_ref, *, tq, seq, group):
    span = seq // KEY_SPLITS
    bounds = [0] + [span // 2 + i * span for i in range(KEY_SPLITS)] + [seq]
    q = q_ref[...].reshape(group * tq, HEAD_DIM)
    parts = []
    for k0, k1 in zip(bounds[:-1], bounds[1:]):
        parts.append(_partial_softmax(_dot_nt(q, k_ref[0, k0:k1, :]), v_ref[0, k0:k1, :]))
    o = _merge_partials(parts)
    for g in range(group):
        o_ref[:, g * HEAD_DIM:(g + 1) * HEAD_DIM] = o[g * tq:(g + 1) * tq].astype(o_ref.dtype)


def gqa_attention(proj, *, batch, seq):
    t = proj.shape[1]
    tq = FULL_ATTN_TQ
    assert seq % tq == 0 and seq % (2 * KEY_SPLITS * LANE) == 0
    group = C_HEADS // C_KV_HEADS
    nq = seq // tq
    return pl.pallas_call(
        functools.partial(_gqa_kernel, tq=tq, seq=seq, group=group),
        grid=(batch, C_KV_HEADS, nq),
        in_specs=[
            pl.BlockSpec((group, tq, LANE), lambda b, n, qi: (n, b * nq + qi, 0)),
            pl.BlockSpec((1, seq, LANE), lambda b, n, qi: (C_HEADS + n, b, 0)),
            pl.BlockSpec((1, seq, LANE), lambda b, n, qi: (C_HEADS + C_KV_HEADS + n, b, 0)),
        ],
        out_specs=pl.BlockSpec((tq, group * HEAD_DIM), lambda b, n, qi: (b * nq + qi, n)),
        out_shape=jax.ShapeDtypeStruct((t, C_HEADS * HEAD_DIM), BF16),
        compiler_params=_params("parallel", "parallel", "arbitrary"),
        name="gqa_attn",
    )(proj, proj, proj)


NA_QROWS = 4
NA_KROWS = NA_QROWS + NA_ROWS
NA_BLOCKS_PER_STEP = 8


def _na_kernel(q_ref, k_ref, v_ref, rows_ref, o_ref, bm_ref, *, rows, nsub):
    step = pl.program_id(2)
    tq = NA_QROWS * GRID_W
    nkeys = NA_KROWS * GRID_W
    nblk = rows // NA_QROWS
    kr = min(NA_ROWS, rows)

    @pl.when(step == 0)
    def _():
        lane = lax.broadcasted_iota(jnp.int32, (GRID_W, LANE), 1)
        qj = lax.broadcasted_iota(jnp.int32, (GRID_W, LANE), 0)
        kj = lane % GRID_W
        cs = jnp.clip(qj - NA_COLS // 2, 0, GRID_W - NA_COLS)
        col_ok = (kj >= cs) & (kj < cs + NA_COLS)
        left = lane < GRID_W
        neg = jnp.full((GRID_W, LANE), NEG_INF, F32)
        bands = {}

        def band(dr, right):
            if (dr, right) not in bands:
                wide = jnp.broadcast_to(rows_ref[0, dr], (GRID_W, LANE))
                rolled = pltpu.roll(wide, GRID_W if right else 0, 1, stride=1, stride_axis=0)
                bands[dr, right] = jnp.where(col_ok, rolled, NEG_INF)
            return bands[dr, right]

        for variant, blk in enumerate((0, 1, nblk - 1)):
            wrow = min(max(blk * NA_QROWS - NA_ROWS // 2, 0), rows - NA_KROWS)
            for a in range(NA_QROWS):
                qi = blk * NA_QROWS + a
                rs = min(max(qi - kr // 2, 0), rows - kr)
                for pair in range(NA_KROWS // 2):
                    halves = []
                    for right in (False, True):
                        ki = wrow + 2 * pair + int(right)
                        halves.append(band(ki - qi + NA_ROWS - 1, right) if rs <= ki < rs + kr else neg)
                    bm_ref[variant, a * GRID_W:(a + 1) * GRID_W, pair * LANE:(pair + 1) * LANE] = (
                        jnp.where(left, halves[0], halves[1]))

    for c in range(nsub):
        blk = step * nsub + c
        wrow = jnp.clip(blk * NA_QROWS - NA_ROWS // 2, 0, rows - NA_KROWS)
        kstart = pl.multiple_of(wrow * GRID_W, NA_QROWS * GRID_W)
        kw = k_ref[0, pl.ds(kstart, nkeys), :]
        vw = v_ref[0, pl.ds(kstart, nkeys), :]
        variant = jnp.where(blk == 0, 0, jnp.where(blk == nblk - 1, 2, 1))
        s = _dot_nt(q_ref[0, c * tq:(c + 1) * tq, :], kw) + bm_ref[variant]
        m = jnp.max(s, axis=-1, keepdims=True)
        p = jnp.exp2(s - m)
        denom = jnp.sum(p, axis=-1, keepdims=True)
        o_ref[c * tq:(c + 1) * tq, :] = (_dot(p.astype(BF16), vw) / denom).astype(o_ref.dtype)


def _na_bias_rows(rpb):
    r = LOG2E * rpb.astype(F32)
    gap = jnp.zeros(r.shape[:-1] + (LANE - r.shape[-1],), F32)
    return jnp.concatenate([r[..., NA_COLS - 1:], gap, r[..., :NA_COLS - 1]], axis=-1)[:, :, None, :]


def na_attention(proj, rpb, *, chunk0, batch, seq):
    t = proj.shape[1]
    rows = seq // GRID_W
    nsub = NA_BLOCKS_PER_STEP
    assert rows % (NA_QROWS * nsub) == 0 and rows >= NA_KROWS and rows // NA_QROWS >= 3
    assert NA_KROWS % 2 == 0 and 2 * GRID_W == LANE and 2 * NA_COLS - 1 <= LANE
    nblk = rows // NA_QROWS
    nstep = nblk // nsub
    tq = NA_QROWS * GRID_W
    nkeys = NA_KROWS * GRID_W
    bias_rows = _na_bias_rows(rpb)
    return pl.pallas_call(
        functools.partial(_na_kernel, rows=rows, nsub=nsub),
        grid=(batch, D_HEADS, nstep),
        in_specs=[
            pl.BlockSpec((1, nsub * tq, LANE), lambda b, h, step: (chunk0 + h, b * nstep + step, 0)),
            pl.BlockSpec((1, seq, LANE), lambda b, h, step: (chunk0 + D_HEADS + h, b, 0)),
            pl.BlockSpec((1, seq, LANE), lambda b, h, step: (chunk0 + 2 * D_HEADS + h, b, 0)),
            pl.BlockSpec((1,) + bias_rows.shape[1:], lambda b, h, step: (h, 0, 0, 0)),
        ],
        out_specs=pl.BlockSpec((nsub * tq, LANE), lambda b, h, step: (b * nstep + step, h)),
        out_shape=jax.ShapeDtypeStruct((t, D_HEADS * HEAD_DIM), BF16),
        scratch_shapes=[pltpu.VMEM((3, tq, nkeys), F32)],
        compiler_params=_params("parallel", "parallel", "arbitrary"),
        name="na_attn",
    )(proj, proj, proj, bias_rows)


def _out_proj_kernel(a_ref, b_ref, w_ref, h_ref, out_ref, *, na):
    out_ref[...] = h_ref[...] + _dot(a_ref[...], w_ref[:na, :]) + _dot(b_ref[...], w_ref[na:, :])


def out_proj(a, b, w, h, *, tm, tn):
    t, d = h.shape
    na, nb = a.shape[1], b.shape[1]
    w = w.astype(BF16)
    return pl.pallas_call(
        functools.partial(_out_proj_kernel, na=na),
        grid=(t // tm, d // tn),
        in_specs=[pl.BlockSpec((tm, na), lambda i, j: (i, 0)),
                  pl.BlockSpec((tm, nb), lambda i, j: (i, 0)),
                  pl.BlockSpec((na + nb, tn), lambda i, j: (0, j)),
                  pl.BlockSpec((tm, tn), lambda i, j: (i, j))],
        out_specs=pl.BlockSpec((tm, tn), lambda i, j: (i, j)),
        out_shape=jax.ShapeDtypeStruct((t, d), F32),
        compiler_params=_params("parallel", "arbitrary"),
        name="out_proj",
    )(a, b, w, h)


FFN_HALO = 16


def _gelu_tanh(x):
    return 0.5 * x * (1.0 + jnp.tanh(math.sqrt(2.0 / math.pi) * (x + 0.044715 * (x * x * x))))


def _ffn_kernel(x_ref, xp_ref, xnx_ref, g_ref, wg_ref, wu_ref, cw_ref, cb_ref, wd_ref, gf_ref, out_ref,
                xn_ref, *, tm, tiles_per_seq, final_norm):
    i = pl.program_id(0)
    f = pl.program_id(1)
    nf = pl.num_programs(1)
    hl = FFN_HALO

    @pl.when(f == 0)
    def _():
        gain = g_ref[...]
        keep_prev = jnp.where(i % tiles_per_seq != 0, 1.0, 0.0)
        keep_next = jnp.where(i % tiles_per_seq != tiles_per_seq - 1, 1.0, 0.0)
        xn_ref[:hl, :] = (_rms(xp_ref[...], gain) * keep_prev).astype(BF16)
        xn_ref[hl:hl + tm, :] = _rms(x_ref[...], gain).astype(BF16)
        xn_ref[hl + tm:, :] = (_rms(xnx_ref[...], gain) * keep_next).astype(BF16)
        out_ref[...] = x_ref[...]

    ge = _dot(xn_ref[...], wg_ref[...])
    cw = cw_ref[...]
    g = (cw[0:1, :] * ge[hl - 1:hl - 1 + tm] + cw[1:2, :] * ge[hl:hl + tm]
         + cw[2:3, :] * ge[hl + 1:hl + 1 + tm] + cb_ref[...])
    u = _dot(xn_ref[hl:hl + tm, :], wu_ref[...])
    act = (_gelu_tanh(g) * u).astype(BF16)
    out_ref[...] += _dot(act, wd_ref[...])

    if final_norm:
        @pl.when(f == nf - 1)
        def _():
            out_ref[...] = _rms(out_ref[...], gf_ref[...])


def conv_ffn_block(h, gain, w_up, conv_w, conv_b, w_down, final_gain, *, seq, tm, tf, final_norm):
    t, d = h.shape
    ff = w_down.shape[0]
    assert t % tm == 0 and seq % tm == 0 and ff % tf == 0 and tm % FFN_HALO == 0
    nfb = ff // tf
    w_up = w_up.astype(BF16)
    w_down = w_down.astype(BF16)
    hb = tm // FFN_HALO
    last_halo = t // FFN_HALO - 1
    return pl.pallas_call(
        functools.partial(_ffn_kernel, tm=tm, tiles_per_seq=seq // tm, final_norm=final_norm),
        grid=(t // tm, nfb),
        in_specs=[
            pl.BlockSpec((tm, d), lambda i, f: (i, 0)),
            pl.BlockSpec((FFN_HALO, d), lambda i, f: (jnp.maximum(i * hb - 1, 0), 0)),
            pl.BlockSpec((FFN_HALO, d), lambda i, f: (jnp.minimum((i + 1) * hb, last_halo), 0)),
            pl.BlockSpec((1, d), lambda i, f: (0, 0)),
            pl.BlockSpec((d, tf), lambda i, f: (0, f)),
            pl.BlockSpec((d, tf), lambda i, f: (0, nfb + f)),
            pl.BlockSpec((3, tf), lambda i, f: (0, f)),
            pl.BlockSpec((1, tf), lambda i, f: (0, f)),
            pl.BlockSpec((tf, d), lambda i, f: (f, 0)),
            pl.BlockSpec((1, d), lambda i, f: (0, 0)),
        ],
        out_specs=pl.BlockSpec((tm, d), lambda i, f: (i, 0)),
        out_shape=jax.ShapeDtypeStruct((t, d), F32),
        scratch_shapes=[pltpu.VMEM((tm + 2 * FFN_HALO, d), BF16)],
        compiler_params=_params("parallel", "arbitrary", vmem_limit=V7X_VMEM_LIMIT_FFN),
        name="conv_ffn",
    )(h, h, h, gain.reshape(1, d), w_up, w_up, conv_w, conv_b.reshape(1, ff), w_down,
      final_gain.reshape(1, d))


def _rope_tables(seq):
    tpos = jnp.arange(seq, dtype=jnp.int32)
    row = (tpos // GRID_W).astype(F32)
    col = (tpos % GRID_W).astype(F32)
    inv_freq = ROPE_THETA ** (-(jnp.arange(0, ROPE_AXIS_DIM, 2, dtype=F32) / ROPE_AXIS_DIM))
    ang = jnp.concatenate([row[:, None] * inv_freq[None], col[:, None] * inv_freq[None]], axis=-1)
    cos, sin = jnp.cos(ang), jnp.sin(ang)
    cos_dup = jnp.repeat(cos, 2, axis=-1)
    sin_signed = jnp.stack([-sin, sin], axis=-1).reshape(seq, HEAD_DIM)
    return cos_dup, sin_signed


def kernel(x, ln_mix, ln_ffn, ln_final, t5_table, ev_w_in, ev_w_out, diff_lq1, diff_lk1, diff_lq2,
           diff_lk2, diff_subln, od_w_in, od_w_out, gqa_q_norm, gqa_k_norm, na_rpb, ffn_w_up,
           ffn_conv_w, ffn_conv_b, ffn_w_down):
    batch, seq, d = x.shape
    depth = ln_mix.shape[0]
    t = batch * seq
    h = x.reshape(t, d)
    proj_tm, even_tn, odd_tn = 1024, A_HEADS * HEAD_DIM, 12 * LANE
    out_tiles = dict(tm=512, tn=d)
    ffn_tiles = dict(tm=1024, tf=512)

    for layer in range(depth):
        if layer % 2 == 0:
            e = layer // 2
            lambda_init = 0.8 - 0.6 * math.exp(-0.3 * layer)
            nat, *dil_projs = even_norm_proj(h, ln_mix[layer], ev_w_in[e],
                                             batch=batch, seq=seq, tm=proj_tm, tn=even_tn)
            oa = dilated_attention(nat, dil_projs, t5_table, batch=batch, seq=seq)
            ob = diff_attention(nat, t5_table[:, A_GROUPS * A_HEADS:], diff_lq1[e], diff_lk1[e],
                                diff_lq2[e], diff_lk2[e], diff_subln[e], chunk0=3 * A_HEADS,
                                batch=batch, seq=seq, lambda_init=lambda_init)
            h = out_proj(oa, ob, ev_w_out[e], h, **out_tiles)
        else:
            o = layer // 2
            n_rope = C_HEADS + C_KV_HEADS
            head_gain = jnp.concatenate([jnp.tile(gqa_q_norm[o][None], (C_HEADS, 1)),
                                         jnp.tile(gqa_k_norm[o][None], (C_KV_HEADS, 1))], axis=0)
            cos_dup, sin_signed = _rope_tables(seq)
            na_chunk0 = n_rope + C_KV_HEADS
            query_chunks = list(range(C_HEADS)) + list(range(na_chunk0, na_chunk0 + D_HEADS))
            proj = odd_norm_proj(h, ln_mix[layer], od_w_in[o], head_gain.astype(F32),
                                 cos_dup, sin_signed, query_chunks=query_chunks, tm=proj_tm, tn=odd_tn)
            oc = gqa_attention(proj, batch=batch, seq=seq)
            od = na_attention(proj, na_rpb[o], chunk0=na_chunk0, batch=batch, seq=seq)
            h = out_proj(oc, od, od_w_out[o], h, **out_tiles)
        h = conv_ffn_block(h, ln_ffn[layer], ffn_w_up[layer], ffn_conv_w[layer],
                           ffn_conv_b[layer], ffn_w_down[layer], ln_final,
                           seq=seq, final_norm=(layer == depth - 1), **ffn_tiles)
    return h.reshape(batch, seq, d)
```

```python
import functools
import math

import numpy as np
import jax
import jax.numpy as jnp
from jax import lax
from jax.experimental import pallas as pl
from jax.experimental.pallas import tpu as pltpu

HEAD_DIM = 128
GRID_W = 64
RMS_EPS = 1e-6
NEG_INF = -1e30
T5_BUCKETS = 32
T5_MAX_DIST = 1024
DILATED_PATTERNS = ((128, 1), (512, 4), (2048, 16))
A_GROUPS = len(DILATED_PATTERNS)
A_HEADS = 8
B_HEADS = 4
B_QK_DIM = 128
B_V_DIM = 2 * B_QK_DIM
C_HEADS = 8
C_KV_HEADS = 2
ROPE_THETA = 10000.0
ROPE_AXIS_DIM = HEAD_DIM // 2
D_HEADS = 8
NA_ROWS = 8
NA_COLS = 16

LANE = 128
V7X_VMEM_LIMIT = 56 * 1024 * 1024
V7X_VMEM_LIMIT_FFN = 61 * 1024 * 1024

BF16 = jnp.bfloat16
F32 = jnp.float32

LOG2E = math.log2(math.e)
QUERY_SCALE = HEAD_DIM ** -0.5 * LOG2E


def _params(*sem, vmem_limit=V7X_VMEM_LIMIT):
    return pltpu.CompilerParams(dimension_semantics=sem, vmem_limit_bytes=vmem_limit)


def _rms(xf, gain):
    ms = jnp.mean(xf * xf, axis=-1, keepdims=True)
    return xf * lax.rsqrt(ms + RMS_EPS) * gain


def _dot_nt(a, b):
    return lax.dot_general(a, b, (((1,), (1,)), ((), ())), preferred_element_type=F32)


def _dot(a, b):
    return jnp.dot(a, b, preferred_element_type=F32)


def _circulant(row, rows):
    return pltpu.roll(jnp.broadcast_to(row, (rows, row.shape[1])), 0, 1, stride=1, stride_axis=0)


def _circulant_positions(cols, rows, width):
    assert width >= rows + cols - 1
    p = np.arange(width)
    x = np.where(p < cols, p, p - width)
    return x, (p < cols) | (p > width - rows)


def _lanes(x, width):
    return x if width == LANE else jnp.concatenate([x] * (width // LANE), axis=1)


def _t5_bucket(rel):
    nb = T5_BUCKETS // 2
    max_exact = nb // 2
    ret = jnp.where(rel > 0, nb, 0)
    n = jnp.abs(rel)
    n_f = jnp.maximum(n, 1).astype(F32)
    large = max_exact + (jnp.log(n_f / max_exact) / math.log(T5_MAX_DIST / max_exact)
                         * (nb - max_exact)).astype(jnp.int32)
    large = jnp.minimum(large, nb - 1)
    return ret + jnp.where(n < max_exact, n, large)


DEINTERLEAVE_STRIDE = 4


def _store_chunks(acc, o_ref, n_chunks):
    for c in range(n_chunks):
        o_ref[c] = acc[:, c * LANE:(c + 1) * LANE].astype(o_ref.dtype)


def _even_proj_kernel(x_ref, g_ref, w_ref, nat_ref, *rest, plan, n_chunks, tm, qkv_period):
    dil_refs, (xn_ref, slab_ref, tmp_ref) = rest[:-3], rest[-3:]
    j = pl.program_id(1)

    @pl.when(j == 0)
    def _():
        xn_ref[...] = _rms(x_ref[...], g_ref[...]).astype(BF16)

    factor = jnp.where(j % qkv_period == 0, QUERY_SCALE, 1.0)

    for lo, hi, dilation, out_idx in plan:
        @pl.when((j >= lo) & (j < hi))
        def _(dilation=dilation, out_idx=out_idx):
            acc = _dot(xn_ref[...], w_ref[...]) * factor
            if dilation == 1:
                _store_chunks(acc, nat_ref, n_chunks)
                return
            o_ref = dil_refs[out_idx]
            per = tm // dilation
            for c in range(n_chunks):
                slab_ref[c] = acc[:, c * LANE:(c + 1) * LANE]
            if dilation <= DEINTERLEAVE_STRIDE:
                for c in range(n_chunks):
                    for r in range(dilation):
                        o_ref[c, 0, r] = slab_ref[c, pl.ds(r, per, stride=dilation), :].astype(o_ref.dtype)
                return
            st = DEINTERLEAVE_STRIDE
            outer = dilation // st
            for c in range(n_chunks):
                for lo in range(st):
                    tmp_ref[c, lo] = slab_ref[c, pl.ds(lo, tm // st, stride=st), :]
                for hi in range(outer):
                    for lo in range(st):
                        o_ref[c, 0, hi * st + lo] = tmp_ref[c, lo, pl.ds(hi, per, stride=outer), :].astype(o_ref.dtype)


def even_norm_proj(x, gain, w, *, batch, seq, tm, tn):
    t, d = x.shape
    n = w.shape[1]
    w = w.astype(BF16)
    n_chunks = tn // LANE
    group_cols = 3 * A_HEADS * HEAD_DIM
    assert group_cols % tn == 0 and n % tn == 0 and seq % tm == 0 and t == batch * seq
    assert tn == A_HEADS * HEAD_DIM == B_HEADS * 2 * B_QK_DIM and B_QK_DIM == HEAD_DIM
    gt = group_cols // tn
    nj = n // tn
    tiles_per_seq = seq // tm
    dils = [dil for _, dil in DILATED_PATTERNS]
    assert dils[0] == 1 and all(tm % (16 * dil) == 0 for dil in dils)
    plan = [(0, gt, 1, None)]
    for gi in range(1, A_GROUPS):
        plan.append((gi * gt, (gi + 1) * gt, dils[gi], gi - 1))
    plan.append((A_GROUPS * gt, nj, 1, None))
    nat_tiles = gt + (nj - A_GROUPS * gt)

    def nat_map(i, j):
        return (jnp.where(j < gt, j, jnp.where(j < A_GROUPS * gt, gt - 1, j - (A_GROUPS - 1) * gt)), i, 0)

    def dil_map(gi):
        return lambda i, j: (jnp.clip(j - gi * gt, 0, gt - 1), i // tiles_per_seq, 0, i % tiles_per_seq, 0)

    out_specs = [pl.BlockSpec((n_chunks, tm, LANE), nat_map)]
    out_shape = [jax.ShapeDtypeStruct((nat_tiles * n_chunks, t, LANE), BF16)]
    for gi in range(1, A_GROUPS):
        dil = dils[gi]
        out_specs.append(pl.BlockSpec((n_chunks, 1, dil, tm // dil, LANE), dil_map(gi)))
        out_shape.append(jax.ShapeDtypeStruct((gt * n_chunks, batch, dil, seq // dil, LANE), BF16))
    return pl.pallas_call(
        functools.partial(_even_proj_kernel, plan=tuple(plan), n_chunks=n_chunks, tm=tm, qkv_period=gt),
        grid=(t // tm, nj),
        in_specs=[
            pl.BlockSpec((tm, d), lambda i, j: (i, 0)),
            pl.BlockSpec((1, d), lambda i, j: (0, 0)),
            pl.BlockSpec((d, tn), lambda i, j: (0, j)),
        ],
        out_specs=out_specs,
        out_shape=out_shape,
        scratch_shapes=[pltpu.VMEM((tm, d), BF16), pltpu.VMEM((n_chunks, tm, LANE), F32),
                        pltpu.VMEM((n_chunks, DEINTERLEAVE_STRIDE, tm // DEINTERLEAVE_STRIDE, LANE), F32)],
        compiler_params=_params("parallel", "arbitrary"),
        name="even_norm_proj",
    )(x, gain.reshape(1, d), w)


def _rope_chunk(y, hg, cosd, sins, even_lane):
    y = _rms(y, hg)
    partner = jnp.where(even_lane, pltpu.roll(y, LANE - 1, axis=1), pltpu.roll(y, 1, axis=1))
    return y * cosd + partner * sins


def _odd_proj_kernel(x_ref, g_ref, w_ref, hg_ref, cos_ref, sin_ref, o_ref, xn_ref, *,
                     n_chunks, n_rope, n_tiles, query_chunks):
    j = pl.program_id(1)

    @pl.when(j == 0)
    def _():
        xn_ref[...] = _rms(x_ref[...], g_ref[...]).astype(BF16)

    acc = _dot(xn_ref[...], w_ref[...])

    for jt in range(n_tiles):
        @pl.when(j == jt)
        def _(jt=jt):
            if jt * n_chunks < n_rope:
                cosd = cos_ref[...]
                sins = sin_ref[...]
                even_lane = (lax.broadcasted_iota(jnp.int32, cosd.shape, 1) % 2) == 0
            for c in range(n_chunks):
                gc = jt * n_chunks + c
                y = acc[:, c * LANE:(c + 1) * LANE]
                if gc < n_rope:
                    y = _rope_chunk(y, hg_ref[gc:gc + 1, :], cosd, sins, even_lane)
                if gc in query_chunks:
                    y = y * QUERY_SCALE
                o_ref[c] = y.astype(o_ref.dtype)


def odd_norm_proj(x, gain, w, head_gain, cos_dup, sin_signed, *, query_chunks, tm, tn):
    t, d = x.shape
    n = w.shape[1]
    assert t % tm == 0 and n % tn == 0 and tn % LANE == 0
    w = w.astype(BF16)
    n_chunks = tn // LANE
    n_rope = head_gain.shape[0]
    s = cos_dup.shape[0]
    assert s % tm == 0
    sb = s // tm
    return pl.pallas_call(
        functools.partial(_odd_proj_kernel, n_chunks=n_chunks, n_rope=n_rope, n_tiles=n // tn,
                          query_chunks=frozenset(query_chunks)),
        grid=(t // tm, n // tn),
        in_specs=[
            pl.BlockSpec((tm, d), lambda i, j: (i, 0)),
            pl.BlockSpec((1, d), lambda i, j: (0, 0)),
            pl.BlockSpec((d, tn), lambda i, j: (0, j)),
            pl.BlockSpec((n_rope, LANE), lambda i, j: (0, 0)),
            pl.BlockSpec((tm, LANE), lambda i, j: (i % sb, 0)),
            pl.BlockSpec((tm, LANE), lambda i, j: (i % sb, 0)),
        ],
        out_specs=pl.BlockSpec((n_chunks, tm, LANE), lambda i, j: (j, i, 0)),
        out_shape=jax.ShapeDtypeStruct((n // LANE, t, LANE), BF16),
        scratch_shapes=[pltpu.VMEM((tm, d), BF16)],
        compiler_params=_params("parallel", "arbitrary"),
        name="odd_norm_proj",
    )(x, gain.reshape(1, d), w, head_gain, cos_dup, sin_signed)


A_TQ = 256
A_BLOCKS_PER_ITER = 16


def _a_geometry(seq, window, dilation):
    sub = seq // dilation
    half = window // (2 * dilation)
    tq = min(A_TQ, sub)
    win = min(tq + 2 * half, sub)
    assert seq % dilation == 0 and sub % tq == 0 and half % 16 == 0 and win % LANE == 0
    assert win == sub or win == tq + 2 * half
    return sub, half, tq, win


def _dilated_kernel(*refs, geoms):
    ng = len(geoms)
    qkv = refs[:3 * ng]
    bias_rows = refs[3 * ng:4 * ng]
    o_ref = refs[4 * ng]
    m_s, l_s, acc_s = refs[4 * ng + 1:4 * ng + 4]
    bms = refs[4 * ng + 4:]

    for g in range(ng):
        _, _, _, tq, win = geoms[g]
        for variant in range(bms[g].shape[0]):
            bms[g][variant] = _circulant(bias_rows[g][variant, 0], tq)[:, :win]

    order = sorted(range(ng), key=lambda g: -geoms[g][0])
    for pos, g in enumerate(order):
        dilation, sub, half, tq, win = geoms[g]
        q_ref, k_ref, v_ref = qkv[3 * g:3 * g + 3]
        bm_ref = bms[g]
        nq = sub // tq
        first = pos == 0
        nblocks = dilation * nq
        batch = math.gcd(A_BLOCKS_PER_ITER, nblocks)

        def one_block(it, dilation=dilation, sub=sub, half=half, tq=tq, win=win, nq=nq,
                      q_ref=q_ref, k_ref=k_ref, v_ref=v_ref, bm_ref=bm_ref, first=first):
            r = it // nq
            qi = it % nq
            q0 = pl.multiple_of(qi * tq, tq)
            kstart = pl.multiple_of(jnp.clip(q0 - half, 0, sub - win), half)
            if dilation == 1:
                q = q_ref[0, pl.ds(q0, tq), :]
                kw = k_ref[0, pl.ds(kstart, win), :]
                vw = v_ref[0, pl.ds(kstart, win), :]
                rows = pl.ds(q0, tq)
            else:
                q = q_ref[0, 0, r, pl.ds(q0, tq), :]
                kw = k_ref[0, 0, r, pl.ds(kstart, win), :]
                vw = v_ref[0, 0, r, pl.ds(kstart, win), :]
                rows = pl.ds(q0 * dilation + r, tq, stride=dilation)
            variant = 0 if nq == 1 else jnp.where(qi == 0, 0, jnp.where(qi == nq - 1, 2, 1))
            s = _dot_nt(q, kw) + bm_ref[variant]
            mb = jnp.max(s, axis=-1, keepdims=True)
            if first:
                p = jnp.exp2(s - mb)
                return (rows, jnp.broadcast_to(mb, (tq, LANE)),
                        jnp.broadcast_to(jnp.sum(p, axis=-1, keepdims=True), (tq, LANE)),
                        _dot(p.astype(BF16), vw))
            m_old = m_s[rows, :]
            m_new = jnp.maximum(m_old, mb)
            alpha = jnp.exp2(m_old - m_new)
            p = jnp.exp2(s - _lanes(m_new, win))
            return (rows, m_new, alpha * l_s[rows, :] + jnp.sum(p, axis=-1, keepdims=True),
                    alpha * acc_s[rows, :] + _dot(p.astype(BF16), vw))

        def blocks(step, carry, batch=batch, one_block=one_block):
            results = [one_block(step * batch + i) for i in range(batch)]
            for rows, m_new, l_new, acc_new in results:
                m_s[rows, :] = m_new
                l_s[rows, :] = l_new
                acc_s[rows, :] = acc_new
            return carry

        lax.fori_loop(0, nblocks // batch, blocks, 0)

    o_ref[...] = (acc_s[...] / l_s[...]).astype(o_ref.dtype)


def _dilated_bias_rows(t5_cols, dilation, half, tq, win, sub):
    nq = sub // tq
    offs = [0] if nq == 1 else [0, -half, tq - win]
    width = -(-(tq + win - 1) // LANE) * LANE
    x, used = _circulant_positions(win, tq, width)
    rows = []
    for off in offs:
        rel = x + off
        ok = jnp.asarray(used & (np.abs(rel) <= half))[None]
        bias = LOG2E * t5_cols[_t5_bucket(jnp.asarray(dilation * rel, jnp.int32))].T.astype(F32)
        rows.append(jnp.where(ok, bias, NEG_INF))
    return jnp.stack(rows)[:, :, None, :]


def dilated_attention(nat, dil_projs, t5_table, *, batch, seq):
    t = batch * seq
    geoms, args, in_specs, bias_rows, tables = [], [], [], [], []
    for gi, (window, dilation) in enumerate(DILATED_PATTERNS):
        sub, half, tq, win = _a_geometry(seq, window, dilation)
        geoms.append((dilation, sub, half, tq, win))
        for part in range(3):
            if dilation == 1:
                args.append(nat)
                in_specs.append(pl.BlockSpec(
                    (1, seq, LANE), lambda b, h, part=part: (part * A_HEADS + h, b, 0)))
            else:
                args.append(dil_projs[gi - 1])
                in_specs.append(pl.BlockSpec(
                    (1, 1, dilation, sub, LANE),
                    lambda b, h, part=part: (part * A_HEADS + h, b, 0, 0, 0)))
        bias_rows.append(_dilated_bias_rows(t5_table[:, gi * A_HEADS:(gi + 1) * A_HEADS],
                                            dilation, half, tq, win, sub))
        tables.append(pltpu.VMEM((bias_rows[-1].shape[0], tq, win), F32))
    for rows in bias_rows:
        args.append(rows)
        in_specs.append(pl.BlockSpec((rows.shape[0], 1) + rows.shape[2:], lambda b, h: (0, h, 0, 0)))
    return pl.pallas_call(
        functools.partial(_dilated_kernel, geoms=tuple(geoms)),
        grid=(batch, A_HEADS),
        in_specs=in_specs,
        out_specs=pl.BlockSpec((seq, LANE), lambda b, h: (b, h)),
        out_shape=jax.ShapeDtypeStruct((t, A_HEADS * LANE), BF16),
        scratch_shapes=[pltpu.VMEM((seq, LANE), F32)] * 3 + tables,
        compiler_params=_params("parallel", "arbitrary"),
        name="dilated_attn",
    )(*args)


KEY_SPLITS = 4
FULL_ATTN_TQ = 256


def _partial_softmax(s, v):
    m = jnp.max(s, axis=-1, keepdims=True)
    p = jnp.exp2(s - m)
    return m, jnp.sum(p, axis=-1, keepdims=True), _dot(p.astype(BF16), v)


def _merge_partials(parts):
    m = parts[0][0]
    for pm, _, _ in parts[1:]:
        m = jnp.maximum(m, pm)
    weights = [jnp.exp2(pm - m) for pm, _, _ in parts]
    denom = sum(w * pl_ for w, (_, pl_, _) in zip(weights, parts))
    acc = sum(w * pa for w, (_, _, pa) in zip(weights, parts))
    return acc / denom


T5_BAND = T5_MAX_DIST


def _diff_band(tq):
    lo = -((T5_BAND + LANE - 1) // LANE) - 1
    hi = (T5_BAND + tq - 1 + LANE - 1) // LANE
    return lo, hi


def _diff_kernel(q_ref, k_ref, v_ref, row_ref, lq1_ref, lk1_ref, lq2_ref, lk2_ref, sub_ref, o_ref,
                 tz_ref, *, tq, seq, lambda_init):
    qi = pl.program_id(2)
    lo, hi = _diff_band(tq)
    q_chunk0 = qi * (tq // LANE)
    span = seq // KEY_SPLITS

    @pl.when(qi == 0)
    def _():
        band = _circulant(row_ref[0], tq)
        for j in range(hi - lo + 1):
            tz_ref[j] = band[:, j * LANE:(j + 1) * LANE]

    def chain(mi, part):
        k0 = part * span
        bias = jnp.concatenate(
            [tz_ref[jnp.clip(cc - q_chunk0, lo, hi) - lo]
             for cc in range(k0 // LANE, (k0 + span) // LANE)], axis=1)
        s = _dot_nt(q_ref[mi], k_ref[mi, k0:k0 + span, :]) + bias
        vc = jnp.concatenate([v_ref[0, k0:k0 + span, :], v_ref[1, k0:k0 + span, :]], axis=1)
        return _partial_softmax(s, vc)

    chains = {(mi, part): chain(mi, part) for part in range(KEY_SPLITS) for mi in range(2)}

    def attend(mi):
        return _merge_partials([chains[mi, part] for part in range(KEY_SPLITS)])

    lam = (jnp.exp(jnp.sum(lq1_ref[...] * lk1_ref[...], axis=-1, keepdims=True))
           - jnp.exp(jnp.sum(lq2_ref[...] * lk2_ref[...], axis=-1, keepdims=True)) + lambda_init)
    o = attend(0) - lam * attend(1)
    o_ref[...] = (_rms(o, sub_ref[...]) * (1.0 - lambda_init)).astype(o_ref.dtype)


def _diff_bias_rows(t5_cols, tq):
    lo, hi = _diff_band(tq)
    cols = (hi - lo + 1) * LANE
    width = -(-(tq + cols - 1) // LANE) * LANE
    x, _ = _circulant_positions(cols, tq, width)
    rel = jnp.asarray(x + LANE * lo, jnp.int32)
    return (LOG2E * t5_cols[_t5_bucket(rel)].T.astype(F32))[:, None, :]


def diff_attention(proj, t5_cols, lq1, lk1, lq2, lk2, subln, *, chunk0, batch, seq, lambda_init):
    t = proj.shape[1]
    tq = FULL_ATTN_TQ
    assert seq % tq == 0 and seq % (KEY_SPLITS * LANE) == 0 and chunk0 % 2 == 0
    nq = seq // tq
    bias_rows = _diff_bias_rows(t5_cols, tq)
    lo, hi = _diff_band(tq)
    qb, kb, vb = chunk0 // 2, chunk0 // 2 + B_HEADS, chunk0 // 2 + 2 * B_HEADS
    vec = lambda a: a.reshape(1, -1).astype(F32)
    vspec = lambda n: pl.BlockSpec((1, n), lambda b, h, qi: (0, 0))
    return pl.pallas_call(
        functools.partial(_diff_kernel, tq=tq, seq=seq, lambda_init=lambda_init),
        grid=(batch, B_HEADS, nq),
        in_specs=[
            pl.BlockSpec((2, tq, LANE), lambda b, h, qi: (qb + h, b * nq + qi, 0)),
            pl.BlockSpec((2, seq, LANE), lambda b, h, qi: (kb + h, b, 0)),
            pl.BlockSpec((2, seq, LANE), lambda b, h, qi: (vb + h, b, 0)),
            pl.BlockSpec((1, 1, bias_rows.shape[2]), lambda b, h, qi: (h, 0, 0)),
            vspec(B_QK_DIM), vspec(B_QK_DIM), vspec(B_QK_DIM), vspec(B_QK_DIM), vspec(B_V_DIM),
        ],
        out_specs=pl.BlockSpec((tq, B_V_DIM), lambda b, h, qi: (b * nq + qi, h)),
        out_shape=jax.ShapeDtypeStruct((t, B_HEADS * B_V_DIM), BF16),
        scratch_shapes=[pltpu.VMEM((hi - lo + 1, tq, LANE), F32)],
        compiler_params=_params("parallel", "parallel", "arbitrary"),
        name="diff_attn",
    )(proj, proj, proj, bias_rows, vec(lq1), vec(lk1), vec(lq2), vec(lk2), vec(subln))


def _gqa_kernel(q_ref, k_ref, v_ref, o_ref, *, tq, seq, group):
    span = seq // KEY_SPLITS
    bounds = [0] + [span // 2 + i * span for i in range(KEY_SPLITS)] + [seq]
    q = q_ref[...].reshape(group * tq, HEAD_DIM)
    parts = []
    for k0, k1 in zip(bounds[:-1], bounds[1:]):
        parts.append(_partial_softmax(_dot_nt(q, k_ref[0, k0:k1, :]), v_ref[0, k0:k1, :]))
    o = _merge_partials(parts)
    for g in range(group):
        o_ref[:, g * HEAD_DIM:(g + 1) * HEAD_DIM] = o[g * tq:(g + 1) * tq].astype(o_ref.dtype)


def gqa_attention(proj, *, batch, seq):
    t = proj.shape[1]
    tq = FULL_ATTN_TQ
    assert seq % tq == 0 and seq % (2 * KEY_SPLITS * LANE) == 0
    group = C_HEADS // C_KV_HEADS
    nq = seq // tq
    return pl.pallas_call(
        functools.partial(_gqa_kernel, tq=tq, seq=seq, group=group),
        grid=(batch, C_KV_HEADS, nq),
        in_specs=[
            pl.BlockSpec((group, tq, LANE), lambda b, n, qi: (n, b * nq + qi, 0)),
            pl.BlockSpec((1, seq, LANE), lambda b, n, qi: (C_HEADS + n, b, 0)),
            pl.BlockSpec((1, seq, LANE), lambda b, n, qi: (C_HEADS + C_KV_HEADS + n, b, 0)),
        ],
        out_specs=pl.BlockSpec((tq, group * HEAD_DIM), lambda b, n, qi: (b * nq + qi, n)),
        out_shape=jax.ShapeDtypeStruct((t, C_HEADS * HEAD_DIM), BF16),
        compiler_params=_params("parallel", "parallel", "arbitrary"),
        name="gqa_attn",
    )(proj, proj, proj)


NA_QROWS = 4
NA_KROWS = NA_QROWS + NA_ROWS
NA_BLOCKS_PER_STEP = 16


def _na_kernel(q_ref, k_ref, v_ref, rows_ref, o_ref, bm_ref, *, rows, nsub):
    step = pl.program_id(2)
    tq = NA_QROWS * GRID_W
    nkeys = NA_KROWS * GRID_W
    nblk = rows // NA_QROWS
    kr = min(NA_ROWS, rows)

    @pl.when(step == 0)
    def _():
        lane = lax.broadcasted_iota(jnp.int32, (GRID_W, LANE), 1)
        qj = lax.broadcasted_iota(jnp.int32, (GRID_W, LANE), 0)
        kj = lane % GRID_W
        cs = jnp.clip(qj - NA_COLS // 2, 0, GRID_W - NA_COLS)
        col_ok = (kj >= cs) & (kj < cs + NA_COLS)
        left = lane < GRID_W
        neg = jnp.full((GRID_W, LANE), NEG_INF, F32)
        bands = {}

        def band(dr, right):
            if (dr, right) not in bands:
                wide = jnp.broadcast_to(rows_ref[0, dr], (GRID_W, LANE))
                rolled = pltpu.roll(wide, GRID_W if right else 0, 1, stride=1, stride_axis=0)
                bands[dr, right] = jnp.where(col_ok, rolled, NEG_INF)
            return bands[dr, right]

        for variant, blk in enumerate((0, 1, nblk - 1)):
            wrow = min(max(blk * NA_QROWS - NA_ROWS // 2, 0), rows - NA_KROWS)
            for a in range(NA_QROWS):
                qi = blk * NA_QROWS + a
                rs = min(max(qi - kr // 2, 0), rows - kr)
                for pair in range(NA_KROWS // 2):
                    halves = []
                    for right in (False, True):
                        ki = wrow + 2 * pair + int(right)
                        halves.append(band(ki - qi + NA_ROWS - 1, right) if rs <= ki < rs + kr else neg)
                    bm_ref[variant, a * GRID_W:(a + 1) * GRID_W, pair * LANE:(pair + 1) * LANE] = (
                        jnp.where(left, halves[0], halves[1]))

    for c in range(nsub):
        blk = step * nsub + c
        wrow = jnp.clip(blk * NA_QROWS - NA_ROWS // 2, 0, rows - NA_KROWS)
        kstart = pl.multiple_of(wrow * GRID_W, NA_QROWS * GRID_W)
        kw = k_ref[0, pl.ds(kstart, nkeys), :]
        vw = v_ref[0, pl.ds(kstart, nkeys), :]
        variant = jnp.where(blk == 0, 0, jnp.where(blk == nblk - 1, 2, 1))
        s = _dot_nt(q_ref[0, c * tq:(c + 1) * tq, :], kw) + bm_ref[variant]
        m = jnp.max(s, axis=-1, keepdims=True)
        p = jnp.exp2(s - m)
        denom = jnp.sum(p, axis=-1, keepdims=True)
        o_ref[c * tq:(c + 1) * tq, :] = (_dot(p.astype(BF16), vw) / denom).astype(o_ref.dtype)


def _na_bias_rows(rpb):
    r = LOG2E * rpb.astype(F32)
    gap = jnp.zeros(r.shape[:-1] + (LANE - r.shape[-1],), F32)
    return jnp.concatenate([r[..., NA_COLS - 1:], gap, r[..., :NA_COLS - 1]], axis=-1)[:, :, None, :]


def na_attention(proj, rpb, *, chunk0, batch, seq):
    t = proj.shape[1]
    rows = seq // GRID_W
    nsub = NA_BLOCKS_PER_STEP
    assert rows % (NA_QROWS * nsub) == 0 and rows >= NA_KROWS and rows // NA_QROWS >= 3
    assert NA_KROWS % 2 == 0 and 2 * GRID_W == LANE and 2 * NA_COLS - 1 <= LANE
    nblk = rows // NA_QROWS
    nstep = nblk // nsub
    tq = NA_QROWS * GRID_W
    nkeys = NA_KROWS * GRID_W
    bias_rows = _na_bias_rows(rpb)
    return pl.pallas_call(
        functools.partial(_na_kernel, rows=rows, nsub=nsub),
        grid=(batch, D_HEADS, nstep),
        in_specs=[
            pl.BlockSpec((1, nsub * tq, LANE), lambda b, h, step: (chunk0 + h, b * nstep + step, 0)),
            pl.BlockSpec((1, seq, LANE), lambda b, h, step: (chunk0 + D_HEADS + h, b, 0)),
            pl.BlockSpec((1, seq, LANE), lambda b, h, step: (chunk0 + 2 * D_HEADS + h, b, 0)),
            pl.BlockSpec((1,) + bias_rows.shape[1:], lambda b, h, step: (h, 0, 0, 0)),
        ],
        out_specs=pl.BlockSpec((nsub * tq, LANE), lambda b, h, step: (b * nstep + step, h)),
        out_shape=jax.ShapeDtypeStruct((t, D_HEADS * HEAD_DIM), BF16),
        scratch_shapes=[pltpu.VMEM((3, tq, nkeys), F32)],
        compiler_params=_params("parallel", "parallel", "arbitrary"),
        name="na_attn",
    )(proj, proj, proj, bias_rows)


def _out_proj_kernel(a_ref, b_ref, w_ref, h_ref, out_ref, *, na):
    out_ref[...] = h_ref[...] + _dot(a_ref[...], w_ref[:na, :]) + _dot(b_ref[...], w_ref[na:, :])


def out_proj(a, b, w, h, *, tm, tn):
    t, d = h.shape
    na, nb = a.shape[1], b.shape[1]
    w = w.astype(BF16)
    return pl.pallas_call(
        functools.partial(_out_proj_kernel, na=na),
        grid=(t // tm, d // tn),
        in_specs=[pl.BlockSpec((tm, na), lambda i, j: (i, 0)),
                  pl.BlockSpec((tm, nb), lambda i, j: (i, 0)),
                  pl.BlockSpec((na + nb, tn), lambda i, j: (0, j)),
                  pl.BlockSpec((tm, tn), lambda i, j: (i, j))],
        out_specs=pl.BlockSpec((tm, tn), lambda i, j: (i, j)),
        out_shape=jax.ShapeDtypeStruct((t, d), F32),
        compiler_params=_params("parallel", "arbitrary"),
        name="out_proj",
    )(a, b, w, h)


FFN_HALO = 16


def _gelu_tanh(x):
    return 0.5 * x * (1.0 + jnp.tanh(math.sqrt(2.0 / math.pi) * (x + 0.044715 * (x * x * x))))


def _ffn_kernel(x_ref, xp_ref, xnx_ref, g_ref, wg_ref, wu_ref, cw_ref, cb_ref, wd_ref, gf_ref, out_ref,
                xn_ref, *, tm, tiles_per_seq, final_norm):
    i = pl.program_id(0)
    f = pl.program_id(1)
    nf = pl.num_programs(1)
    hl = FFN_HALO

    @pl.when(f == 0)
    def _():
        gain = g_ref[...]
        keep_prev = jnp.where(i % tiles_per_seq != 0, 1.0, 0.0)
        keep_next = jnp.where(i % tiles_per_seq != tiles_per_seq - 1, 1.0, 0.0)
        xn_ref[:hl, :] = (_rms(xp_ref[...], gain) * keep_prev).astype(BF16)
        xn_ref[hl:hl + tm, :] = _rms(x_ref[...], gain).astype(BF16)
        xn_ref[hl + tm:, :] = (_rms(xnx_ref[...], gain) * keep_next).astype(BF16)
        out_ref[...] = x_ref[...]

    ge = _dot(xn_ref[...], wg_ref[...])
    cw = cw_ref[...]
    g = (cw[0:1, :] * ge[hl - 1:hl - 1 + tm] + cw[1:2, :] * ge[hl:hl + tm]
         + cw[2:3, :] * ge[hl + 1:hl + 1 + tm] + cb_ref[...])
    u = _dot(xn_ref[hl:hl + tm, :], wu_ref[...])
    act = (_gelu_tanh(g) * u).astype(BF16)
    out_ref[...] += _dot(act, wd_ref[...])

    if final_norm:
        @pl.when(f == nf - 1)
        def _():
            out_ref[...] = _rms(out_ref[...], gf_ref[...])


def conv_ffn_block(h, gain, w_up, conv_w, conv_b, w_down, final_gain, *, seq, tm, tf, final_norm):
    t, d = h.shape
    ff = w_down.shape[0]
    assert t % tm == 0 and seq % tm == 0 and ff % tf == 0 and tm % FFN_HALO == 0
    nfb = ff // tf
    w_up = w_up.astype(BF16)
    w_down = w_down.astype(BF16)
    hb = tm // FFN_HALO
    last_halo = t // FFN_HALO - 1
    return pl.pallas_call(
        functools.partial(_ffn_kernel, tm=tm, tiles_per_seq=seq // tm, final_norm=final_norm),
        grid=(t // tm, nfb),
        in_specs=[
            pl.BlockSpec((tm, d), lambda i, f: (i, 0)),
            pl.BlockSpec((FFN_HALO, d), lambda i, f: (jnp.maximum(i * hb - 1, 0), 0)),
            pl.BlockSpec((FFN_HALO, d), lambda i, f: (jnp.minimum((i + 1) * hb, last_halo), 0)),
            pl.BlockSpec((1, d), lambda i, f: (0, 0)),
            pl.BlockSpec((d, tf), lambda i, f: (0, f)),
            pl.BlockSpec((d, tf), lambda i, f: (0, nfb + f)),
            pl.BlockSpec((3, tf), lambda i, f: (0, f)),
            pl.BlockSpec((1, tf), lambda i, f: (0, f)),
            pl.BlockSpec((tf, d), lambda i, f: (f, 0)),
            pl.BlockSpec((1, d), lambda i, f: (0, 0)),
        ],
        out_specs=pl.BlockSpec((tm, d), lambda i, f: (i, 0)),
        out_shape=jax.ShapeDtypeStruct((t, d), F32),
        scratch_shapes=[pltpu.VMEM((tm + 2 * FFN_HALO, d), BF16)],
        compiler_params=_params("parallel", "arbitrary", vmem_limit=V7X_VMEM_LIMIT_FFN),
        name="conv_ffn",
    )(h, h, h, gain.reshape(1, d), w_up, w_up, conv_w, conv_b.reshape(1, ff), w_down,
      final_gain.reshape(1, d))


def _rope_tables(seq):
    tpos = jnp.arange(seq, dtype=jnp.int32)
    row = (tpos // GRID_W).astype(F32)
    col = (tpos % GRID_W).astype(F32)
    inv_freq = ROPE_THETA ** (-(jnp.arange(0, ROPE_AXIS_DIM, 2, dtype=F32) / ROPE_AXIS_DIM))
    ang = jnp.concatenate([row[:, None] * inv_freq[None], col[:, None] * inv_freq[None]], axis=-1)
    cos, sin = jnp.cos(ang), jnp.sin(ang)
    cos_dup = jnp.repeat(cos, 2, axis=-1)
    sin_signed = jnp.stack([-sin, sin], axis=-1).reshape(seq, HEAD_DIM)
    return cos_dup, sin_signed


def kernel(x, ln_mix, ln_ffn, ln_final, t5_table, ev_w_in, ev_w_out, diff_lq1, diff_lk1, diff_lq2,
           diff_lk2, diff_subln, od_w_in, od_w_out, gqa_q_norm, gqa_k_norm, na_rpb, ffn_w_up,
           ffn_conv_w, ffn_conv_b, ffn_w_down):
    batch, seq, d = x.shape
    depth = ln_mix.shape[0]
    t = batch * seq
    h = x.reshape(t, d)
    proj_tm, even_tn, odd_tn = 1024, A_HEADS * HEAD_DIM, 12 * LANE
    out_tiles = dict(tm=512, tn=d)
    ffn_tiles = dict(tm=1024, tf=512)

    for layer in range(depth):
        if layer % 2 == 0:
            e = layer // 2
            lambda_init = 0.8 - 0.6 * math.exp(-0.3 * layer)
            nat, *dil_projs = even_norm_proj(h, ln_mix[layer], ev_w_in[e],
                                             batch=batch, seq=seq, tm=proj_tm, tn=even_tn)
            oa = dilated_attention(nat, dil_projs, t5_table, batch=batch, seq=seq)
            ob = diff_attention(nat, t5_table[:, A_GROUPS * A_HEADS:], diff_lq1[e], diff_lk1[e],
                                diff_lq2[e], diff_lk2[e], diff_subln[e], chunk0=3 * A_HEADS,
                                batch=batch, seq=seq, lambda_init=lambda_init)
            h = out_proj(oa, ob, ev_w_out[e], h, **out_tiles)
        else:
            o = layer // 2
            n_rope = C_HEADS + C_KV_HEADS
            head_gain = jnp.concatenate([jnp.tile(gqa_q_norm[o][None], (C_HEADS, 1)),
                                         jnp.tile(gqa_k_norm[o][None], (C_KV_HEADS, 1))], axis=0)
            cos_dup, sin_signed = _rope_tables(seq)
            na_chunk0 = n_rope + C_KV_HEADS
            query_chunks = list(range(C_HEADS)) + list(range(na_chunk0, na_chunk0 + D_HEADS))
            proj = odd_norm_proj(h, ln_mix[layer], od_w_in[o], head_gain.astype(F32),
                                 cos_dup, sin_signed, query_chunks=query_chunks, tm=proj_tm, tn=odd_tn)
            oc = gqa_attention(proj, batch=batch, seq=seq)
            od = na_attention(proj, na_rpb[o], chunk0=na_chunk0, batch=batch, seq=seq)
            h = out_proj(oc, od, od_w_out[o], h, **out_tiles)
        h = conv_ffn_block(h, ln_ffn[layer], ffn_w_up[layer], ffn_conv_w[layer],
                           ffn_conv_b[layer], ffn_w_down[layer], ln_final,
                           seq=seq, final_norm=(layer == depth - 1), **ffn_tiles)
    return h.reshape(batch, seq, d)
```

```python
import functools
import math

import numpy as np
import jax
import jax.numpy as jnp
from jax import lax
from jax.experimental import pallas as pl
from jax.experimental.pallas import tpu as pltpu

HEAD_DIM = 128
GRID_W = 64
RMS_EPS = 1e-6
NEG_INF = -1e30
T5_BUCKETS = 32
T5_MAX_DIST = 1024
DILATED_PATTERNS = ((128, 1), (512, 4), (2048, 16))
A_GROUPS = len(DILATED_PATTERNS)
A_HEADS = 8
B_HEADS = 4
B_QK_DIM = 128
B_V_DIM = 2 * B_QK_DIM
C_HEADS = 8
C_KV_HEADS = 2
ROPE_THETA = 10000.0
ROPE_AXIS_DIM = HEAD_DIM // 2
D_HEADS = 8
NA_ROWS = 8
NA_COLS = 16

LANE = 128
V7X_VMEM_LIMIT = 56 * 1024 * 1024
V7X_VMEM_LIMIT_FFN = 61 * 1024 * 1024

BF16 = jnp.bfloat16
F32 = jnp.float32

LOG2E = math.log2(math.e)
QUERY_SCALE = HEAD_DIM ** -0.5 * LOG2E


def _params(*sem, vmem_limit=V7X_VMEM_LIMIT):
    return pltpu.CompilerParams(dimension_semantics=sem, vmem_limit_bytes=vmem_limit)


def _rms(xf, gain):
    ms = jnp.mean(xf * xf, axis=-1, keepdims=True)
    return xf * lax.rsqrt(ms + RMS_EPS) * gain


def _dot_nt(a, b):
    return lax.dot_general(a, b, (((1,), (1,)), ((), ())), preferred_element_type=F32)


def _dot(a, b):
    return jnp.dot(a, b, preferred_element_type=F32)


def _circulant(row, rows):
    return pltpu.roll(jnp.broadcast_to(row, (rows, row.shape[1])), 0, 1, stride=1, stride_axis=0)


def _circulant_positions(cols, rows, width):
    assert width >= rows + cols - 1
    p = np.arange(width)
    x = np.where(p < cols, p, p - width)
    return x, (p < cols) | (p > width - rows)


def _lanes(x, width):
    return x if width == LANE else jnp.concatenate([x] * (width // LANE), axis=1)


CAST_BLOCK_BYTES = 8 * 1024 * 1024


def _cast_kernel(w_ref, o_ref):
    o_ref[...] = w_ref[0].astype(o_ref.dtype)


def to_bf16(stack, index):
    _, k, n = stack.shape
    fits = [r for r in range(16, k + 1, 16) if k % r == 0 and 4 * r * n <= CAST_BLOCK_BYTES]
    rows = max(fits) if fits else k
    return pl.pallas_call(
        _cast_kernel,
        grid=(k // rows,),
        in_specs=[pl.BlockSpec((1, rows, n), lambda i: (index, i, 0))],
        out_specs=pl.BlockSpec((rows, n), lambda i: (i, 0)),
        out_shape=jax.ShapeDtypeStruct((k, n), BF16),
        compiler_params=_params("parallel"),
        name="cast_bf16",
    )(stack)


def _t5_bucket(rel):
    nb = T5_BUCKETS // 2
    max_exact = nb // 2
    ret = jnp.where(rel > 0, nb, 0)
    n = jnp.abs(rel)
    n_f = jnp.maximum(n, 1).astype(F32)
    large = max_exact + (jnp.log(n_f / max_exact) / math.log(T5_MAX_DIST / max_exact)
                         * (nb - max_exact)).astype(jnp.int32)
    large = jnp.minimum(large, nb - 1)
    return ret + jnp.where(n < max_exact, n, large)


DEINTERLEAVE_STRIDE = 4


def _store_chunks(acc, o_ref, n_chunks):
    for c in range(n_chunks):
        o_ref[c] = acc[:, c * LANE:(c + 1) * LANE].astype(o_ref.dtype)


def _even_proj_kernel(x_ref, g_ref, w_ref, nat_ref, *rest, plan, n_chunks, tm, qkv_period):
    dil_refs, (xn_ref, slab_ref, tmp_ref) = rest[:-3], rest[-3:]
    j = pl.program_id(1)

    @pl.when(j == 0)
    def _():
        xn_ref[...] = _rms(x_ref[...], g_ref[...]).astype(BF16)

    factor = jnp.where(j % qkv_period == 0, QUERY_SCALE, 1.0)

    for lo, hi, dilation, out_idx in plan:
        @pl.when((j >= lo) & (j < hi))
        def _(dilation=dilation, out_idx=out_idx):
            acc = _dot(xn_ref[...], w_ref[...]) * factor
            if dilation == 1:
                _store_chunks(acc, nat_ref, n_chunks)
                return
            o_ref = dil_refs[out_idx]
            per = tm // dilation
            for c in range(n_chunks):
                slab_ref[c] = acc[:, c * LANE:(c + 1) * LANE]
            if dilation <= DEINTERLEAVE_STRIDE:
                for c in range(n_chunks):
                    for r in range(dilation):
                        o_ref[c, 0, r] = slab_ref[c, pl.ds(r, per, stride=dilation), :].astype(o_ref.dtype)
                return
            st = DEINTERLEAVE_STRIDE
            outer = dilation // st
            for c in range(n_chunks):
                for lo in range(st):
                    tmp_ref[c, lo] = slab_ref[c, pl.ds(lo, tm // st, stride=st), :]
                for hi in range(outer):
                    for lo in range(st):
                        o_ref[c, 0, hi * st + lo] = tmp_ref[c, lo, pl.ds(hi, per, stride=outer), :].astype(o_ref.dtype)


def even_norm_proj(x, gain, w, *, batch, seq, tm, tn):
    t, d = x.shape
    n = w.shape[1]
    w = w.astype(BF16)
    n_chunks = tn // LANE
    group_cols = 3 * A_HEADS * HEAD_DIM
    assert group_cols % tn == 0 and n % tn == 0 and seq % tm == 0 and t == batch * seq
    assert tn == A_HEADS * HEAD_DIM == B_HEADS * 2 * B_QK_DIM and B_QK_DIM == HEAD_DIM
    gt = group_cols // tn
    nj = n // tn
    tiles_per_seq = seq // tm
    dils = [dil for _, dil in DILATED_PATTERNS]
    assert dils[0] == 1 and all(tm % (16 * dil) == 0 for dil in dils)
    plan = [(0, gt, 1, None)]
    for gi in range(1, A_GROUPS):
        plan.append((gi * gt, (gi + 1) * gt, dils[gi], gi - 1))
    plan.append((A_GROUPS * gt, nj, 1, None))
    nat_tiles = gt + (nj - A_GROUPS * gt)

    def nat_map(i, j):
        return (jnp.where(j < gt, j, jnp.where(j < A_GROUPS * gt, gt - 1, j - (A_GROUPS - 1) * gt)), i, 0)

    def dil_map(gi):
        return lambda i, j: (jnp.clip(j - gi * gt, 0, gt - 1), i // tiles_per_seq, 0, i % tiles_per_seq, 0)

    out_specs = [pl.BlockSpec((n_chunks, tm, LANE), nat_map)]
    out_shape = [jax.ShapeDtypeStruct((nat_tiles * n_chunks, t, LANE), BF16)]
    for gi in range(1, A_GROUPS):
        dil = dils[gi]
        out_specs.append(pl.BlockSpec((n_chunks, 1, dil, tm // dil, LANE), dil_map(gi)))
        out_shape.append(jax.ShapeDtypeStruct((gt * n_chunks, batch, dil, seq // dil, LANE), BF16))
    return pl.pallas_call(
        functools.partial(_even_proj_kernel, plan=tuple(plan), n_chunks=n_chunks, tm=tm, qkv_period=gt),
        grid=(t // tm, nj),
        in_specs=[
            pl.BlockSpec((tm, d), lambda i, j: (i, 0)),
            pl.BlockSpec((1, d), lambda i, j: (0, 0)),
            pl.BlockSpec((d, tn), lambda i, j: (0, j)),
        ],
        out_specs=out_specs,
        out_shape=out_shape,
        scratch_shapes=[pltpu.VMEM((tm, d), BF16), pltpu.VMEM((n_chunks, tm, LANE), F32),
                        pltpu.VMEM((n_chunks, DEINTERLEAVE_STRIDE, tm // DEINTERLEAVE_STRIDE, LANE), F32)],
        compiler_params=_params("parallel", "arbitrary"),
        name="even_norm_proj",
    )(x, gain.reshape(1, d), w)


def _rope_chunk(y, hg, cosd, sins, even_lane):
    y = _rms(y, hg)
    partner = jnp.where(even_lane, pltpu.roll(y, LANE - 1, axis=1), pltpu.roll(y, 1, axis=1))
    return y * cosd + partner * sins


def _odd_proj_kernel(x_ref, g_ref, w_ref, hg_ref, cos_ref, sin_ref, o_ref, xn_ref, *,
                     n_chunks, n_rope, n_tiles, query_chunks):
    j = pl.program_id(1)

    @pl.when(j == 0)
    def _():
        xn_ref[...] = _rms(x_ref[...], g_ref[...]).astype(BF16)

    acc = _dot(xn_ref[...], w_ref[...])

    for jt in range(n_tiles):
        @pl.when(j == jt)
        def _(jt=jt):
            if jt * n_chunks < n_rope:
                cosd = cos_ref[...]
                sins = sin_ref[...]
                even_lane = (lax.broadcasted_iota(jnp.int32, cosd.shape, 1) % 2) == 0
            for c in range(n_chunks):
                gc = jt * n_chunks + c
                y = acc[:, c * LANE:(c + 1) * LANE]
                if gc < n_rope:
                    y = _rope_chunk(y, hg_ref[gc:gc + 1, :], cosd, sins, even_lane)
                if gc in query_chunks:
                    y = y * QUERY_SCALE
                o_ref[c] = y.astype(o_ref.dtype)


def odd_norm_proj(x, gain, w, head_gain, cos_dup, sin_signed, *, query_chunks, tm, tn):
    t, d = x.shape
    n = w.shape[1]
    assert t % tm == 0 and n % tn == 0 and tn % LANE == 0
    w = w.astype(BF16)
    n_chunks = tn // LANE
    n_rope = head_gain.shape[0]
    s = cos_dup.shape[0]
    assert s % tm == 0
    sb = s // tm
    return pl.pallas_call(
        functools.partial(_odd_proj_kernel, n_chunks=n_chunks, n_rope=n_rope, n_tiles=n // tn,
                          query_chunks=frozenset(query_chunks)),
        grid=(t // tm, n // tn),
        in_specs=[
            pl.BlockSpec((tm, d), lambda i, j: (i, 0)),
            pl.BlockSpec((1, d), lambda i, j: (0, 0)),
            pl.BlockSpec((d, tn), lambda i, j: (0, j)),
            pl.BlockSpec((n_rope, LANE), lambda i, j: (0, 0)),
            pl.BlockSpec((tm, LANE), lambda i, j: (i % sb, 0)),
            pl.BlockSpec((tm, LANE), lambda i, j: (i % sb, 0)),
        ],
        out_specs=pl.BlockSpec((n_chunks, tm, LANE), lambda i, j: (j, i, 0)),
        out_shape=jax.ShapeDtypeStruct((n // LANE, t, LANE), BF16),
        scratch_shapes=[pltpu.VMEM((tm, d), BF16)],
        compiler_params=_params("parallel", "arbitrary"),
        name="odd_norm_proj",
    )(x, gain.reshape(1, d), w, head_gain, cos_dup, sin_signed)


A_TQ = 256
A_BLOCKS_PER_ITER = 16


def _a_geometry(seq, window, dilation):
    sub = seq // dilation
    half = window // (2 * dilation)
    tq = min(A_TQ, sub)
    win = min(tq + 2 * half, sub)
    assert seq % dilation == 0 and sub % tq == 0 and half % 16 == 0 and win % LANE == 0
    assert win == sub or win == tq + 2 * half
    return sub, half, tq, win


def _dilated_kernel(*refs, geoms):
    ng = len(geoms)
    qkv = refs[:3 * ng]
    bias_rows = refs[3 * ng:4 * ng]
    o_ref = refs[4 * ng]
    m_s, l_s, acc_s = refs[4 * ng + 1:4 * ng + 4]
    bms = refs[4 * ng + 4:]

    for g in range(ng):
        _, _, _, tq, win = geoms[g]
        for variant in range(bms[g].shape[0]):
            bms[g][variant] = _circulant(bias_rows[g][variant, 0], tq)[:, :win]

    order = sorted(range(ng), key=lambda g: -geoms[g][0])
    for pos, g in enumerate(order):
        dilation, sub, half, tq, win = geoms[g]
        q_ref, k_ref, v_ref = qkv[3 * g:3 * g + 3]
        bm_ref = bms[g]
        nq = sub // tq
        first = pos == 0
        nblocks = dilation * nq
        batch = math.gcd(A_BLOCKS_PER_ITER, nblocks)

        def one_block(it, dilation=dilation, sub=sub, half=half, tq=tq, win=win, nq=nq,
                      q_ref=q_ref, k_ref=k_ref, v_ref=v_ref, bm_ref=bm_ref, first=first):
            r = it // nq
            qi = it % nq
            q0 = pl.multiple_of(qi * tq, tq)
            kstart = pl.multiple_of(jnp.clip(q0 - half, 0, sub - win), half)
            if dilation == 1:
                q = q_ref[0, pl.ds(q0, tq), :]
                kw = k_ref[0, pl.ds(kstart, win), :]
                vw = v_ref[0, pl.ds(kstart, win), :]
                rows = pl.ds(q0, tq)
            else:
                q = q_ref[0, 0, r, pl.ds(q0, tq), :]
                kw = k_ref[0, 0, r, pl.ds(kstart, win), :]
                vw = v_ref[0, 0, r, pl.ds(kstart, win), :]
                rows = pl.ds(q0 * dilation + r, tq, stride=dilation)
            variant = 0 if nq == 1 else jnp.where(qi == 0, 0, jnp.where(qi == nq - 1, 2, 1))
            s = _dot_nt(q, kw) + bm_ref[variant]
            mb = jnp.max(s, axis=-1, keepdims=True)
            if first:
                p = jnp.exp2(s - mb)
                return (rows, jnp.broadcast_to(mb, (tq, LANE)),
                        jnp.broadcast_to(jnp.sum(p, axis=-1, keepdims=True), (tq, LANE)),
                        _dot(p.astype(BF16), vw))
            m_old = m_s[rows, :]
            m_new = jnp.maximum(m_old, mb)
            alpha = jnp.exp2(m_old - m_new)
            p = jnp.exp2(s - _lanes(m_new, win))
            return (rows, m_new, alpha * l_s[rows, :] + jnp.sum(p, axis=-1, keepdims=True),
                    alpha * acc_s[rows, :] + _dot(p.astype(BF16), vw))

        def blocks(step, carry, batch=batch, one_block=one_block):
            results = [one_block(step * batch + i) for i in range(batch)]
            for rows, m_new, l_new, acc_new in results:
                m_s[rows, :] = m_new
                l_s[rows, :] = l_new
                acc_s[rows, :] = acc_new
            return carry

        lax.fori_loop(0, nblocks // batch, blocks, 0)

    o_ref[...] = (acc_s[...] / l_s[...]).astype(o_ref.dtype)


def _dilated_bias_rows(t5_cols, dilation, half, tq, win, sub):
    nq = sub // tq
    offs = [0] if nq == 1 else [0, -half, tq - win]
    width = -(-(tq + win - 1) // LANE) * LANE
    x, used = _circulant_positions(win, tq, width)
    rows = []
    for off in offs:
        rel = x + off
        ok = jnp.asarray(used & (np.abs(rel) <= half))[None]
        bias = LOG2E * t5_cols[_t5_bucket(jnp.asarray(dilation * rel, jnp.int32))].T.astype(F32)
        rows.append(jnp.where(ok, bias, NEG_INF))
    return jnp.stack(rows)[:, :, None, :]


def dilated_attention(nat, dil_projs, t5_table, *, batch, seq):
    t = batch * seq
    geoms, args, in_specs, bias_rows, tables = [], [], [], [], []
    for gi, (window, dilation) in enumerate(DILATED_PATTERNS):
        sub, half, tq, win = _a_geometry(seq, window, dilation)
        geoms.append((dilation, sub, half, tq, win))
        for part in range(3):
            if dilation == 1:
                args.append(nat)
                in_specs.append(pl.BlockSpec(
                    (1, seq, LANE), lambda b, h, part=part: (part * A_HEADS + h, b, 0)))
            else:
                args.append(dil_projs[gi - 1])
                in_specs.append(pl.BlockSpec(
                    (1, 1, dilation, sub, LANE),
                    lambda b, h, part=part: (part * A_HEADS + h, b, 0, 0, 0)))
        bias_rows.append(_dilated_bias_rows(t5_table[:, gi * A_HEADS:(gi + 1) * A_HEADS],
                                            dilation, half, tq, win, sub))
        tables.append(pltpu.VMEM((bias_rows[-1].shape[0], tq, win), F32))
    for rows in bias_rows:
        args.append(rows)
        in_specs.append(pl.BlockSpec((rows.shape[0], 1) + rows.shape[2:], lambda b, h: (0, h, 0, 0)))
    return pl.pallas_call(
        functools.partial(_dilated_kernel, geoms=tuple(geoms)),
        grid=(batch, A_HEADS),
        in_specs=in_specs,
        out_specs=pl.BlockSpec((seq, LANE), lambda b, h: (b, h)),
        out_shape=jax.ShapeDtypeStruct((t, A_HEADS * LANE), BF16),
        scratch_shapes=[pltpu.VMEM((seq, LANE), F32)] * 3 + tables,
        compiler_params=_params("parallel", "arbitrary"),
        name="dilated_attn",
    )(*args)


KEY_SPLITS = 4
FULL_ATTN_TQ = 256


def _partial_softmax(s, v):
    m = jnp.max(s, axis=-1, keepdims=True)
    p = jnp.exp2(s - m)
    return m, jnp.sum(p, axis=-1, keepdims=True), _dot(p.astype(BF16), v)


def _merge_partials(parts):
    m = parts[0][0]
    for pm, _, _ in parts[1:]:
        m = jnp.maximum(m, pm)
    weights = [jnp.exp2(pm - m) for pm, _, _ in parts]
    denom = sum(w * pl_ for w, (_, pl_, _) in zip(weights, parts))
    acc = sum(w * pa for w, (_, _, pa) in zip(weights, parts))
    return acc / denom


T5_BAND = T5_MAX_DIST


def _diff_band(tq):
    lo = -((T5_BAND + LANE - 1) // LANE) - 1
    hi = (T5_BAND + tq - 1 + LANE - 1) // LANE
    return lo, hi


def _diff_kernel(q_ref, k_ref, v_ref, row_ref, lq1_ref, lk1_ref, lq2_ref, lk2_ref, sub_ref, o_ref,
                 tz_ref, *, tq, seq, lambda_init):
    qi = pl.program_id(2)
    lo, hi = _diff_band(tq)
    q_chunk0 = qi * (tq // LANE)
    span = seq // KEY_SPLITS

    @pl.when(qi == 0)
    def _():
        band = _circulant(row_ref[0], tq)
        for j in range(hi - lo + 1):
            tz_ref[j] = band[:, j * LANE:(j + 1) * LANE]

    def chain(mi, part):
        k0 = part * span
        bias = jnp.concatenate(
            [tz_ref[jnp.clip(cc - q_chunk0, lo, hi) - lo]
             for cc in range(k0 // LANE, (k0 + span) // LANE)], axis=1)
        s = _dot_nt(q_ref[mi], k_ref[mi, k0:k0 + span, :]) + bias
        vc = jnp.concatenate([v_ref[0, k0:k0 + span, :], v_ref[1, k0:k0 + span, :]], axis=1)
        return _partial_softmax(s, vc)

    chains = {(mi, part): chain(mi, part) for part in range(KEY_SPLITS) for mi in range(2)}

    def attend(mi):
        return _merge_partials([chains[mi, part] for part in range(KEY_SPLITS)])

    lam = (jnp.exp(jnp.sum(lq1_ref[...] * lk1_ref[...], axis=-1, keepdims=True))
           - jnp.exp(jnp.sum(lq2_ref[...] * lk2_ref[...], axis=-1, keepdims=True)) + lambda_init)
    o = attend(0) - lam * attend(1)
    o_ref[...] = (_rms(o, sub_ref[...]) * (1.0 - lambda_init)).astype(o_ref.dtype)


def _diff_bias_rows(t5_cols, tq):
    lo, hi = _diff_band(tq)
    cols = (hi - lo + 1) * LANE
    width = -(-(tq + cols - 1) // LANE) * LANE
    x, _ = _circulant_positions(cols, tq, width)
    rel = jnp.asarray(x + LANE * lo, jnp.int32)
    return (LOG2E * t5_cols[_t5_bucket(rel)].T.astype(F32))[:, None, :]


def diff_attention(proj, t5_cols, lq1, lk1, lq2, lk2, subln, *, chunk0, batch, seq, lambda_init):
    t = proj.shape[1]
    tq = FULL_ATTN_TQ
    assert seq % tq == 0 and seq % (KEY_SPLITS * LANE) == 0 and chunk0 % 2 == 0
    nq = seq // tq
    bias_rows = _diff_bias_rows(t5_cols, tq)
    lo, hi = _diff_band(tq)
    qb, kb, vb = chunk0 // 2, chunk0 // 2 + B_HEADS, chunk0 // 2 + 2 * B_HEADS
    vec = lambda a: a.reshape(1, -1).astype(F32)
    vspec = lambda n: pl.BlockSpec((1, n), lambda b, h, qi: (0, 0))
    return pl.pallas_call(
        functools.partial(_diff_kernel, tq=tq, seq=seq, lambda_init=lambda_init),
        grid=(batch, B_HEADS, nq),
        in_specs=[
            pl.BlockSpec((2, tq, LANE), lambda b, h, qi: (qb + h, b * nq + qi, 0)),
            pl.BlockSpec((2, seq, LANE), lambda b, h, qi: (kb + h, b, 0)),
            pl.BlockSpec((2, seq, LANE), lambda b, h, qi: (vb + h, b, 0)),
            pl.BlockSpec((1, 1, bias_rows.shape[2]), lambda b, h, qi: (h, 0, 0)),
            vspec(B_QK_DIM), vspec(B_QK_DIM), vspec(B_QK_DIM), vspec(B_QK_DIM), vspec(B_V_DIM),
        ],
        out_specs=pl.BlockSpec((tq, B_V_DIM), lambda b, h, qi: (b * nq + qi, h)),
        out_shape=jax.ShapeDtypeStruct((t, B_HEADS * B_V_DIM), BF16),
        scratch_shapes=[pltpu.VMEM((hi - lo + 1, tq, LANE), F32)],
        compiler_params=_params("parallel", "parallel", "arbitrary"),
        name="diff_attn",
    )(proj, proj, proj, bias_rows, vec(lq1), vec(lk1), vec(lq2), vec(lk2), vec(subln))


def _gqa_kernel(q_ref, k_ref, v_ref, o_ref, *, tq, seq, group):
    span = seq // KEY_SPLITS
    bounds = [0] + [span // 2 + i * span for i in range(KEY_SPLITS)] + [seq]
    q = q_ref[...].reshape(group * tq, HEAD_DIM)
    parts = []
    for k0, k1 in zip(bounds[:-1], bounds[1:]):
        parts.append(_partial_softmax(_dot_nt(q, k_ref[0, k0:k1, :]), v_ref[0, k0:k1, :]))
    o = _merge_partials(parts)
    for g in range(group):
        o_ref[:, g * HEAD_DIM:(g + 1) * HEAD_DIM] = o[g * tq:(g + 1) * tq].astype(o_ref.dtype)


def gqa_attention(proj, *, batch, seq):
    t = proj.shape[1]
    tq = FULL_ATTN_TQ
    assert seq % tq == 0 and seq % (2 * KEY_SPLITS * LANE) == 0
    group = C_HEADS // C_KV_HEADS
    nq = seq // tq
    return pl.pallas_call(
        functools.partial(_gqa_kernel, tq=tq, seq=seq, group=group),
        grid=(batch, C_KV_HEADS, nq),
        in_specs=[
            pl.BlockSpec((group, tq, LANE), lambda b, n, qi: (n, b * nq + qi, 0)),
            pl.BlockSpec((1, seq, LANE), lambda b, n, qi: (C_HEADS + n, b, 0)),
            pl.BlockSpec((1, seq, LANE), lambda b, n, qi: (C_HEADS + C_KV_HEADS + n, b, 0)),
        ],
        out_specs=pl.BlockSpec((tq, group * HEAD_DIM), lambda b, n, qi: (b * nq + qi, n)),
        out_shape=jax.ShapeDtypeStruct((t, C_HEADS * HEAD_DIM), BF16),
        compiler_params=_params("parallel", "parallel", "arbitrary"),
        name="gqa_attn",
    )(proj, proj, proj)


NA_QROWS = 4
NA_KROWS = NA_QROWS + NA_ROWS
NA_BLOCKS_PER_STEP = 16


def _na_kernel(q_ref, k_ref, v_ref, rows_ref, o_ref, bm_ref, *, rows, nsub):
    step = pl.program_id(2)
    tq = NA_QROWS * GRID_W
    nkeys = NA_KROWS * GRID_W
    nblk = rows // NA_QROWS
    kr = min(NA_ROWS, rows)

    @pl.when(step == 0)
    def _():
        lane = lax.broadcasted_iota(jnp.int32, (GRID_W, LANE), 1)
        qj = lax.broadcasted_iota(jnp.int32, (GRID_W, LANE), 0)
        kj = lane % GRID_W
        cs = jnp.clip(qj - NA_COLS // 2, 0, GRID_W - NA_COLS)
        col_ok = (kj >= cs) & (kj < cs + NA_COLS)
        left = lane < GRID_W
        neg = jnp.full((GRID_W, LANE), NEG_INF, F32)
        bands = {}

        def band(dr, right):
            if (dr, right) not in bands:
                wide = jnp.broadcast_to(rows_ref[0, dr], (GRID_W, LANE))
                rolled = pltpu.roll(wide, GRID_W if right else 0, 1, stride=1, stride_axis=0)
                bands[dr, right] = jnp.where(col_ok, rolled, NEG_INF)
            return bands[dr, right]

        for variant, blk in enumerate((0, 1, nblk - 1)):
            wrow = min(max(blk * NA_QROWS - NA_ROWS // 2, 0), rows - NA_KROWS)
            for a in range(NA_QROWS):
                qi = blk * NA_QROWS + a
                rs = min(max(qi - kr // 2, 0), rows - kr)
                for pair in range(NA_KROWS // 2):
                    halves = []
                    for right in (False, True):
                        ki = wrow + 2 * pair + int(right)
                        halves.append(band(ki - qi + NA_ROWS - 1, right) if rs <= ki < rs + kr else neg)
                    bm_ref[variant, a * GRID_W:(a + 1) * GRID_W, pair * LANE:(pair + 1) * LANE] = (
                        jnp.where(left, halves[0], halves[1]))

    for c in range(nsub):
        blk = step * nsub + c
        wrow = jnp.clip(blk * NA_QROWS - NA_ROWS // 2, 0, rows - NA_KROWS)
        kstart = pl.multiple_of(wrow * GRID_W, NA_QROWS * GRID_W)
        kw = k_ref[0, pl.ds(kstart, nkeys), :]
        vw = v_ref[0, pl.ds(kstart, nkeys), :]
        variant = jnp.where(blk == 0, 0, jnp.where(blk == nblk - 1, 2, 1))
        s = _dot_nt(q_ref[0, c * tq:(c + 1) * tq, :], kw) + bm_ref[variant]
        m = jnp.max(s, axis=-1, keepdims=True)
        p = jnp.exp2(s - m)
        denom = jnp.sum(p, axis=-1, keepdims=True)
        o_ref[c * tq:(c + 1) * tq, :] = (_dot(p.astype(BF16), vw) / denom).astype(o_ref.dtype)


def _na_bias_rows(rpb):
    r = LOG2E * rpb.astype(F32)
    gap = jnp.zeros(r.shape[:-1] + (LANE - r.shape[-1],), F32)
    return jnp.concatenate([r[..., NA_COLS - 1:], gap, r[..., :NA_COLS - 1]], axis=-1)[:, :, None, :]


def na_attention(proj, rpb, *, chunk0, batch, seq):
    t = proj.shape[1]
    rows = seq // GRID_W
    nsub = NA_BLOCKS_PER_STEP
    assert rows % (NA_QROWS * nsub) == 0 and rows >= NA_KROWS and rows // NA_QROWS >= 3
    assert NA_KROWS % 2 == 0 and 2 * GRID_W == LANE and 2 * NA_COLS - 1 <= LANE
    nblk = rows // NA_QROWS
    nstep = nblk // nsub
    tq = NA_QROWS * GRID_W
    nkeys = NA_KROWS * GRID_W
    bias_rows = _na_bias_rows(rpb)
    return pl.pallas_call(
        functools.partial(_na_kernel, rows=rows, nsub=nsub),
        grid=(batch, D_HEADS, nstep),
        in_specs=[
            pl.BlockSpec((1, nsub * tq, LANE), lambda b, h, step: (chunk0 + h, b * nstep + step, 0)),
            pl.BlockSpec((1, seq, LANE), lambda b, h, step: (chunk0 + D_HEADS + h, b, 0)),
            pl.BlockSpec((1, seq, LANE), lambda b, h, step: (chunk0 + 2 * D_HEADS + h, b, 0)),
            pl.BlockSpec((1,) + bias_rows.shape[1:], lambda b, h, step: (h, 0, 0, 0)),
        ],
        out_specs=pl.BlockSpec((nsub * tq, LANE), lambda b, h, step: (b * nstep + step, h)),
        out_shape=jax.ShapeDtypeStruct((t, D_HEADS * HEAD_DIM), BF16),
        scratch_shapes=[pltpu.VMEM((3, tq, nkeys), F32)],
        compiler_params=_params("parallel", "parallel", "arbitrary"),
        name="na_attn",
    )(proj, proj, proj, bias_rows)


def _out_proj_kernel(a_ref, b_ref, w_ref, h_ref, out_ref, *, na):
    out_ref[...] = h_ref[...] + _dot(a_ref[...], w_ref[:na, :]) + _dot(b_ref[...], w_ref[na:, :])


def out_proj(a, b, w, h, *, tm, tn):
    t, d = h.shape
    na, nb = a.shape[1], b.shape[1]
    w = w.astype(BF16)
    return pl.pallas_call(
        functools.partial(_out_proj_kernel, na=na),
        grid=(t // tm, d // tn),
        in_specs=[pl.BlockSpec((tm, na), lambda i, j: (i, 0)),
                  pl.BlockSpec((tm, nb), lambda i, j: (i, 0)),
                  pl.BlockSpec((na + nb, tn), lambda i, j: (0, j)),
                  pl.BlockSpec((tm, tn), lambda i, j: (i, j))],
        out_specs=pl.BlockSpec((tm, tn), lambda i, j: (i, j)),
        out_shape=jax.ShapeDtypeStruct((t, d), F32),
        compiler_params=_params("parallel", "arbitrary"),
        name="out_proj",
    )(a, b, w, h)


FFN_HALO = 16


def _gelu_tanh(x):
    return 0.5 * x * (1.0 + jnp.tanh(math.sqrt(2.0 / math.pi) * (x + 0.044715 * (x * x * x))))


def _ffn_kernel(x_ref, xp_ref, xnx_ref, g_ref, wg_ref, wu_ref, cw_ref, cb_ref, wd_ref, gf_ref, out_ref,
                xn_ref, *, tm, tiles_per_seq, final_norm):
    i = pl.program_id(0)
    f = pl.program_id(1)
    nf = pl.num_programs(1)
    hl = FFN_HALO

    @pl.when(f == 0)
    def _():
        gain = g_ref[...]
        keep_prev = jnp.where(i % tiles_per_seq != 0, 1.0, 0.0)
        keep_next = jnp.where(i % tiles_per_seq != tiles_per_seq - 1, 1.0, 0.0)
        xn_ref[:hl, :] = (_rms(xp_ref[...], gain) * keep_prev).astype(BF16)
        xn_ref[hl:hl + tm, :] = _rms(x_ref[...], gain).astype(BF16)
        xn_ref[hl + tm:, :] = (_rms(xnx_ref[...], gain) * keep_next).astype(BF16)
        out_ref[...] = x_ref[...]

    ge = _dot(xn_ref[...], wg_ref[...])
    cw = cw_ref[...]
    g = (cw[0:1, :] * ge[hl - 1:hl - 1 + tm] + cw[1:2, :] * ge[hl:hl + tm]
         + cw[2:3, :] * ge[hl + 1:hl + 1 + tm] + cb_ref[...])
    u = _dot(xn_ref[hl:hl + tm, :], wu_ref[...])
    act = (_gelu_tanh(g) * u).astype(BF16)
    out_ref[...] += _dot(act, wd_ref[...])

    if final_norm:
        @pl.when(f == nf - 1)
        def _():
            out_ref[...] = _rms(out_ref[...], gf_ref[...])


def conv_ffn_block(h, gain, w_up, conv_w, conv_b, w_down, final_gain, *, seq, tm, tf, final_norm):
    t, d = h.shape
    ff = w_down.shape[0]
    assert t % tm == 0 and seq % tm == 0 and ff % tf == 0 and tm % FFN_HALO == 0
    nfb = ff // tf
    w_up = w_up.astype(BF16)
    w_down = w_down.astype(BF16)
    hb = tm // FFN_HALO
    last_halo = t // FFN_HALO - 1
    return pl.pallas_call(
        functools.partial(_ffn_kernel, tm=tm, tiles_per_seq=seq // tm, final_norm=final_norm),
        grid=(t // tm, nfb),
        in_specs=[
            pl.BlockSpec((tm, d), lambda i, f: (i, 0)),
            pl.BlockSpec((FFN_HALO, d), lambda i, f: (jnp.maximum(i * hb - 1, 0), 0)),
            pl.BlockSpec((FFN_HALO, d), lambda i, f: (jnp.minimum((i + 1) * hb, last_halo), 0)),
            pl.BlockSpec((1, d), lambda i, f: (0, 0)),
            pl.BlockSpec((d, tf), lambda i, f: (0, f)),
            pl.BlockSpec((d, tf), lambda i, f: (0, nfb + f)),
            pl.BlockSpec((3, tf), lambda i, f: (0, f)),
            pl.BlockSpec((1, tf), lambda i, f: (0, f)),
            pl.BlockSpec((tf, d), lambda i, f: (f, 0)),
            pl.BlockSpec((1, d), lambda i, f: (0, 0)),
        ],
        out_specs=pl.BlockSpec((tm, d), lambda i, f: (i, 0)),
        out_shape=jax.ShapeDtypeStruct((t, d), F32),
        scratch_shapes=[pltpu.VMEM((tm + 2 * FFN_HALO, d), BF16)],
        compiler_params=_params("parallel", "arbitrary", vmem_limit=V7X_VMEM_LIMIT_FFN),
        name="conv_ffn",
    )(h, h, h, gain.reshape(1, d), w_up, w_up, conv_w, conv_b.reshape(1, ff), w_down,
      final_gain.reshape(1, d))


def _rope_tables(seq):
    tpos = jnp.arange(seq, dtype=jnp.int32)
    row = (tpos // GRID_W).astype(F32)
    col = (tpos % GRID_W).astype(F32)
    inv_freq = ROPE_THETA ** (-(jnp.arange(0, ROPE_AXIS_DIM, 2, dtype=F32) / ROPE_AXIS_DIM))
    ang = jnp.concatenate([row[:, None] * inv_freq[None], col[:, None] * inv_freq[None]], axis=-1)
    cos, sin = jnp.cos(ang), jnp.sin(ang)
    cos_dup = jnp.repeat(cos, 2, axis=-1)
    sin_signed = jnp.stack([-sin, sin], axis=-1).reshape(seq, HEAD_DIM)
    return cos_dup, sin_signed


def kernel(x, ln_mix, ln_ffn, ln_final, t5_table, ev_w_in, ev_w_out, diff_lq1, diff_lk1, diff_lq2,
           diff_lk2, diff_subln, od_w_in, od_w_out, gqa_q_norm, gqa_k_norm, na_rpb, ffn_w_up,
           ffn_conv_w, ffn_conv_b, ffn_w_down):
    batch, seq, d = x.shape
    depth = ln_mix.shape[0]
    t = batch * seq
    h = x.reshape(t, d)
    proj_tm, even_tn, odd_tn = 1024, A_HEADS * HEAD_DIM, 12 * LANE
    out_tiles = dict(tm=512, tn=d)
    ffn_tiles = dict(tm=1024, tf=512)

    for layer in range(depth):
        if layer % 2 == 0:
            e = layer // 2
            lambda_init = 0.8 - 0.6 * math.exp(-0.3 * layer)
            nat, *dil_projs = even_norm_proj(h, ln_mix[layer], to_bf16(ev_w_in, e),
                                             batch=batch, seq=seq, tm=proj_tm, tn=even_tn)
            oa = dilated_attention(nat, dil_projs, t5_table, batch=batch, seq=seq)
            ob = diff_attention(nat, t5_table[:, A_GROUPS * A_HEADS:], diff_lq1[e], diff_lk1[e],
                                diff_lq2[e], diff_lk2[e], diff_subln[e], chunk0=3 * A_HEADS,
                                batch=batch, seq=seq, lambda_init=lambda_init)
            h = out_proj(oa, ob, to_bf16(ev_w_out, e), h, **out_tiles)
        else:
            o = layer // 2
            n_rope = C_HEADS + C_KV_HEADS
            head_gain = jnp.concatenate([jnp.tile(gqa_q_norm[o][None], (C_HEADS, 1)),
                                         jnp.tile(gqa_k_norm[o][None], (C_KV_HEADS, 1))], axis=0)
            cos_dup, sin_signed = _rope_tables(seq)
            na_chunk0 = n_rope + C_KV_HEADS
            query_chunks = list(range(C_HEADS)) + list(range(na_chunk0, na_chunk0 + D_HEADS))
            proj = odd_norm_proj(h, ln_mix[layer], to_bf16(od_w_in, o), head_gain.astype(F32),
                                 cos_dup, sin_signed, query_chunks=query_chunks, tm=proj_tm, tn=odd_tn)
            oc = gqa_attention(proj, batch=batch, seq=seq)
            od = na_attention(proj, na_rpb[o], chunk0=na_chunk0, batch=batch, seq=seq)
            h = out_proj(oc, od, to_bf16(od_w_out, o), h, **out_tiles)
        h = conv_ffn_block(h, ln_ffn[layer], to_bf16(ffn_w_up, layer), ffn_conv_w[layer],
                           ffn_conv_b[layer], to_bf16(ffn_w_down, layer), ln_final,
                           seq=seq, final_norm=(layer == depth - 1), **ffn_tiles)
    return h.reshape(batch, seq, d)
```

```python
import functools
import math

import numpy as np
import jax
import jax.numpy as jnp
from jax import lax
from jax.experimental import pallas as pl
from jax.experimental.pallas import tpu as pltpu

HEAD_DIM = 128
GRID_W = 64
RMS_EPS = 1e-6
NEG_INF = -1e30
T5_BUCKETS = 32
T5_MAX_DIST = 1024
DILATED_PATTERNS = ((128, 1), (512, 4), (2048, 16))
A_GROUPS = len(DILATED_PATTERNS)
A_HEADS = 8
B_HEADS = 4
B_QK_DIM = 128
B_V_DIM = 2 * B_QK_DIM
C_HEADS = 8
C_KV_HEADS = 2
ROPE_THETA = 10000.0
ROPE_AXIS_DIM = HEAD_DIM // 2
D_HEADS = 8
NA_ROWS = 8
NA_COLS = 16

LANE = 128
V7X_VMEM_LIMIT = 56 * 1024 * 1024
V7X_VMEM_LIMIT_FFN = 61 * 1024 * 1024

BF16 = jnp.bfloat16
F32 = jnp.float32

LOG2E = math.log2(math.e)
QUERY_SCALE = HEAD_DIM ** -0.5 * LOG2E


def _params(*sem, vmem_limit=V7X_VMEM_LIMIT):
    return pltpu.CompilerParams(dimension_semantics=sem, vmem_limit_bytes=vmem_limit)


def _rms(xf, gain):
    ms = jnp.mean(xf * xf, axis=-1, keepdims=True)
    return xf * lax.rsqrt(ms + RMS_EPS) * gain


def _dot_nt(a, b):
    return lax.dot_general(a, b, (((1,), (1,)), ((), ())), preferred_element_type=F32)


def _dot(a, b):
    return jnp.dot(a, b, preferred_element_type=F32)


def _circulant(row, rows):
    return pltpu.roll(jnp.broadcast_to(row, (rows, row.shape[1])), 0, 1, stride=1, stride_axis=0)


def _circulant_positions(cols, rows, width):
    assert width >= rows + cols - 1
    p = np.arange(width)
    x = np.where(p < cols, p, p - width)
    return x, (p < cols) | (p > width - rows)


def _lanes(x, width):
    return x if width == LANE else jnp.concatenate([x] * (width // LANE), axis=1)


CAST_BLOCK_BYTES = 8 * 1024 * 1024


def _cast_kernel(w_ref, o_ref):
    o_ref[...] = w_ref[0].astype(o_ref.dtype)


def to_bf16(stack, index):
    _, k, n = stack.shape
    fits = [r for r in range(16, k + 1, 16) if k % r == 0 and 4 * r * n <= CAST_BLOCK_BYTES]
    rows = max(fits) if fits else k
    return pl.pallas_call(
        _cast_kernel,
        grid=(k // rows,),
        in_specs=[pl.BlockSpec((1, rows, n), lambda i: (index, i, 0))],
        out_specs=pl.BlockSpec((rows, n), lambda i: (i, 0)),
        out_shape=jax.ShapeDtypeStruct((k, n), BF16),
        compiler_params=_params("parallel"),
        name="cast_bf16",
    )(stack)


def _t5_bucket(rel):
    nb = T5_BUCKETS // 2
    max_exact = nb // 2
    ret = jnp.where(rel > 0, nb, 0)
    n = jnp.abs(rel)
    n_f = jnp.maximum(n, 1).astype(F32)
    large = max_exact + (jnp.log(n_f / max_exact) / math.log(T5_MAX_DIST / max_exact)
                         * (nb - max_exact)).astype(jnp.int32)
    large = jnp.minimum(large, nb - 1)
    return ret + jnp.where(n < max_exact, n, large)


DEINTERLEAVE_STRIDE = 4


def _store_chunks(acc, o_ref, n_chunks):
    for c in range(n_chunks):
        o_ref[c] = acc[:, c * LANE:(c + 1) * LANE].astype(o_ref.dtype)


def _even_proj_kernel(x_ref, g_ref, w_ref, nat_ref, *rest, plan, n_chunks, tm, qkv_period):
    dil_refs, (xn_ref, slab_ref, tmp_ref) = rest[:-3], rest[-3:]
    j = pl.program_id(1)

    @pl.when(j == 0)
    def _():
        xn_ref[...] = _rms(x_ref[...], g_ref[...]).astype(BF16)

    factor = jnp.where(j % qkv_period == 0, QUERY_SCALE, 1.0)

    for lo, hi, dilation, out_idx in plan:
        @pl.when((j >= lo) & (j < hi))
        def _(dilation=dilation, out_idx=out_idx):
            acc = _dot(xn_ref[...], w_ref[...]) * factor
            if dilation == 1:
                _store_chunks(acc, nat_ref, n_chunks)
                return
            o_ref = dil_refs[out_idx]
            per = tm // dilation
            for c in range(n_chunks):
                slab_ref[c] = acc[:, c * LANE:(c + 1) * LANE]
            if dilation <= DEINTERLEAVE_STRIDE:
                for c in range(n_chunks):
                    for r in range(dilation):
                        o_ref[c, 0, r] = slab_ref[c, pl.ds(r, per, stride=dilation), :].astype(o_ref.dtype)
                return
            st = DEINTERLEAVE_STRIDE
            outer = dilation // st
            for c in range(n_chunks):
                for lo in range(st):
                    tmp_ref[c, lo] = slab_ref[c, pl.ds(lo, tm // st, stride=st), :]
                for hi in range(outer):
                    for lo in range(st):
                        o_ref[c, 0, hi * st + lo] = tmp_ref[c, lo, pl.ds(hi, per, stride=outer), :].astype(o_ref.dtype)


def even_norm_proj(x, gain, w, *, batch, seq, tm, tn):
    t, d = x.shape
    n = w.shape[1]
    w = w.astype(BF16)
    n_chunks = tn // LANE
    group_cols = 3 * A_HEADS * HEAD_DIM
    assert group_cols % tn == 0 and n % tn == 0 and seq % tm == 0 and t == batch * seq
    assert tn == A_HEADS * HEAD_DIM == B_HEADS * 2 * B_QK_DIM and B_QK_DIM == HEAD_DIM
    gt = group_cols // tn
    nj = n // tn
    tiles_per_seq = seq // tm
    dils = [dil for _, dil in DILATED_PATTERNS]
    assert dils[0] == 1 and all(tm % (16 * dil) == 0 for dil in dils)
    plan = [(0, gt, 1, None)]
    for gi in range(1, A_GROUPS):
        plan.append((gi * gt, (gi + 1) * gt, dils[gi], gi - 1))
    plan.append((A_GROUPS * gt, nj, 1, None))
    nat_tiles = gt + (nj - A_GROUPS * gt)

    def nat_map(i, j):
        return (jnp.where(j < gt, j, jnp.where(j < A_GROUPS * gt, gt - 1, j - (A_GROUPS - 1) * gt)), i, 0)

    def dil_map(gi):
        return lambda i, j: (jnp.clip(j - gi * gt, 0, gt - 1), i // tiles_per_seq, 0, i % tiles_per_seq, 0)

    out_specs = [pl.BlockSpec((n_chunks, tm, LANE), nat_map)]
    out_shape = [jax.ShapeDtypeStruct((nat_tiles * n_chunks, t, LANE), BF16)]
    for gi in range(1, A_GROUPS):
        dil = dils[gi]
        out_specs.append(pl.BlockSpec((n_chunks, 1, dil, tm // dil, LANE), dil_map(gi)))
        out_shape.append(jax.ShapeDtypeStruct((gt * n_chunks, batch, dil, seq // dil, LANE), BF16))
    return pl.pallas_call(
        functools.partial(_even_proj_kernel, plan=tuple(plan), n_chunks=n_chunks, tm=tm, qkv_period=gt),
        grid=(t // tm, nj),
        in_specs=[
            pl.BlockSpec((tm, d), lambda i, j: (i, 0)),
            pl.BlockSpec((1, d), lambda i, j: (0, 0)),
            pl.BlockSpec((d, tn), lambda i, j: (0, j)),
        ],
        out_specs=out_specs,
        out_shape=out_shape,
        scratch_shapes=[pltpu.VMEM((tm, d), BF16), pltpu.VMEM((n_chunks, tm, LANE), F32),
                        pltpu.VMEM((n_chunks, DEINTERLEAVE_STRIDE, tm // DEINTERLEAVE_STRIDE, LANE), F32)],
        compiler_params=_params("parallel", "arbitrary"),
        name="even_norm_proj",
    )(x, gain.reshape(1, d), w)


def _rope_chunk(y, hg, cosd, sins, even_lane):
    y = _rms(y, hg)
    partner = jnp.where(even_lane, pltpu.roll(y, LANE - 1, axis=1), pltpu.roll(y, 1, axis=1))
    return y * cosd + partner * sins


def _odd_proj_kernel(x_ref, g_ref, w_ref, hg_ref, cos_ref, sin_ref, o_ref, xn_ref, *,
                     n_chunks, n_rope, n_tiles, query_chunks):
    j = pl.program_id(1)

    @pl.when(j == 0)
    def _():
        xn_ref[...] = _rms(x_ref[...], g_ref[...]).astype(BF16)

    acc = _dot(xn_ref[...], w_ref[...])

    for jt in range(n_tiles):
        @pl.when(j == jt)
        def _(jt=jt):
            if jt * n_chunks < n_rope:
                cosd = cos_ref[...]
                sins = sin_ref[...]
                even_lane = (lax.broadcasted_iota(jnp.int32, cosd.shape, 1) % 2) == 0
            for c in range(n_chunks):
                gc = jt * n_chunks + c
                y = acc[:, c * LANE:(c + 1) * LANE]
                if gc < n_rope:
                    y = _rope_chunk(y, hg_ref[gc:gc + 1, :], cosd, sins, even_lane)
                if gc in query_chunks:
                    y = y * QUERY_SCALE
                o_ref[c] = y.astype(o_ref.dtype)


def odd_norm_proj(x, gain, w, head_gain, cos_dup, sin_signed, *, query_chunks, tm, tn):
    t, d = x.shape
    n = w.shape[1]
    assert t % tm == 0 and n % tn == 0 and tn % LANE == 0
    w = w.astype(BF16)
    n_chunks = tn // LANE
    n_rope = head_gain.shape[0]
    s = cos_dup.shape[0]
    assert s % tm == 0
    sb = s // tm
    return pl.pallas_call(
        functools.partial(_odd_proj_kernel, n_chunks=n_chunks, n_rope=n_rope, n_tiles=n // tn,
                          query_chunks=frozenset(query_chunks)),
        grid=(t // tm, n // tn),
        in_specs=[
            pl.BlockSpec((tm, d), lambda i, j: (i, 0)),
            pl.BlockSpec((1, d), lambda i, j: (0, 0)),
            pl.BlockSpec((d, tn), lambda i, j: (0, j)),
            pl.BlockSpec((n_rope, LANE), lambda i, j: (0, 0)),
            pl.BlockSpec((tm, LANE), lambda i, j: (i % sb, 0)),
            pl.BlockSpec((tm, LANE), lambda i, j: (i % sb, 0)),
        ],
        out_specs=pl.BlockSpec((n_chunks, tm, LANE), lambda i, j: (j, i, 0)),
        out_shape=jax.ShapeDtypeStruct((n // LANE, t, LANE), BF16),
        scratch_shapes=[pltpu.VMEM((tm, d), BF16)],
        compiler_params=_params("parallel", "arbitrary"),
        name="odd_norm_proj",
    )(x, gain.reshape(1, d), w, head_gain, cos_dup, sin_signed)


A_TQ = 256
A_BLOCKS_PER_ITER = 16


def _a_geometry(seq, window, dilation):
    sub = seq // dilation
    half = window // (2 * dilation)
    tq = min(A_TQ, sub)
    win = min(tq + 2 * half, sub)
    assert seq % dilation == 0 and sub % tq == 0 and half % 16 == 0 and win % LANE == 0
    assert win == sub or win == tq + 2 * half
    return sub, half, tq, win


def _dilated_kernel(*refs, geoms):
    ng = len(geoms)
    qkv = refs[:3 * ng]
    bias_rows = refs[3 * ng:4 * ng]
    o_ref = refs[4 * ng]
    m_s, l_s, acc_s = refs[4 * ng + 1:4 * ng + 4]
    bms = refs[4 * ng + 4:]

    for g in range(ng):
        _, _, _, tq, win = geoms[g]
        for variant in range(bms[g].shape[0]):
            bms[g][variant] = _circulant(bias_rows[g][variant, 0], tq)[:, :win]

    order = sorted(range(ng), key=lambda g: -geoms[g][0])
    for pos, g in enumerate(order):
        dilation, sub, half, tq, win = geoms[g]
        q_ref, k_ref, v_ref = qkv[3 * g:3 * g + 3]
        bm_ref = bms[g]
        nq = sub // tq
        first = pos == 0
        nblocks = dilation * nq
        batch = math.gcd(A_BLOCKS_PER_ITER, nblocks)

        def one_block(it, dilation=dilation, sub=sub, half=half, tq=tq, win=win, nq=nq,
                      q_ref=q_ref, k_ref=k_ref, v_ref=v_ref, bm_ref=bm_ref, first=first):
            r = it // nq
            qi = it % nq
            q0 = pl.multiple_of(qi * tq, tq)
            kstart = pl.multiple_of(jnp.clip(q0 - half, 0, sub - win), half)
            if dilation == 1:
                q = q_ref[0, pl.ds(q0, tq), :]
                kw = k_ref[0, pl.ds(kstart, win), :]
                vw = v_ref[0, pl.ds(kstart, win), :]
                rows = pl.ds(q0, tq)
            else:
                q = q_ref[0, 0, r, pl.ds(q0, tq), :]
                kw = k_ref[0, 0, r, pl.ds(kstart, win), :]
                vw = v_ref[0, 0, r, pl.ds(kstart, win), :]
                rows = pl.ds(q0 * dilation + r, tq, stride=dilation)
            variant = 0 if nq == 1 else jnp.where(qi == 0, 0, jnp.where(qi == nq - 1, 2, 1))
            s = _dot_nt(q, kw) + bm_ref[variant]
            mb = jnp.max(s, axis=-1, keepdims=True)
            if first:
                acc, denom = _pv_with_denominator(jnp.exp2(s - mb), vw)
                return rows, jnp.broadcast_to(mb, (tq, LANE)), denom, acc
            m_old = m_s[rows, :]
            m_new = jnp.maximum(m_old, mb)
            alpha = jnp.exp2(m_old - m_new)
            acc, denom = _pv_with_denominator(jnp.exp2(s - _lanes(m_new, win)), vw)
            return rows, m_new, alpha * l_s[rows, :] + denom, alpha * acc_s[rows, :] + acc

        def blocks(step, carry, batch=batch, one_block=one_block):
            results = [one_block(step * batch + i) for i in range(batch)]
            for rows, m_new, l_new, acc_new in results:
                m_s[rows, :] = m_new
                l_s[rows, :] = l_new
                acc_s[rows, :] = acc_new
            return carry

        lax.fori_loop(0, nblocks // batch, blocks, 0)

    o_ref[...] = (acc_s[...] / l_s[...]).astype(o_ref.dtype)


def _dilated_bias_rows(t5_cols, dilation, half, tq, win, sub):
    nq = sub // tq
    offs = [0] if nq == 1 else [0, -half, tq - win]
    width = -(-(tq + win - 1) // LANE) * LANE
    x, used = _circulant_positions(win, tq, width)
    rows = []
    for off in offs:
        rel = x + off
        ok = jnp.asarray(used & (np.abs(rel) <= half))[None]
        bias = LOG2E * t5_cols[_t5_bucket(jnp.asarray(dilation * rel, jnp.int32))].T.astype(F32)
        rows.append(jnp.where(ok, bias, NEG_INF))
    return jnp.stack(rows)[:, :, None, :]


def dilated_attention(nat, dil_projs, t5_table, *, batch, seq):
    t = batch * seq
    geoms, args, in_specs, bias_rows, tables = [], [], [], [], []
    for gi, (window, dilation) in enumerate(DILATED_PATTERNS):
        sub, half, tq, win = _a_geometry(seq, window, dilation)
        geoms.append((dilation, sub, half, tq, win))
        for part in range(3):
            if dilation == 1:
                args.append(nat)
                in_specs.append(pl.BlockSpec(
                    (1, seq, LANE), lambda b, h, part=part: (part * A_HEADS + h, b, 0)))
            else:
                args.append(dil_projs[gi - 1])
                in_specs.append(pl.BlockSpec(
                    (1, 1, dilation, sub, LANE),
                    lambda b, h, part=part: (part * A_HEADS + h, b, 0, 0, 0)))
        bias_rows.append(_dilated_bias_rows(t5_table[:, gi * A_HEADS:(gi + 1) * A_HEADS],
                                            dilation, half, tq, win, sub))
        tables.append(pltpu.VMEM((bias_rows[-1].shape[0], tq, win), F32))
    for rows in bias_rows:
        args.append(rows)
        in_specs.append(pl.BlockSpec((rows.shape[0], 1) + rows.shape[2:], lambda b, h: (0, h, 0, 0)))
    return pl.pallas_call(
        functools.partial(_dilated_kernel, geoms=tuple(geoms)),
        grid=(batch, A_HEADS),
        in_specs=in_specs,
        out_specs=pl.BlockSpec((seq, LANE), lambda b, h: (b, h)),
        out_shape=jax.ShapeDtypeStruct((t, A_HEADS * LANE), BF16),
        scratch_shapes=[pltpu.VMEM((seq, LANE), F32)] * 3 + tables,
        compiler_params=_params("parallel", "arbitrary"),
        name="dilated_attn",
    )(*args)


KEY_SPLITS = 4
FULL_ATTN_TQ = 256


def _partial_softmax(s, v):
    m = jnp.max(s, axis=-1, keepdims=True)
    p = jnp.exp2(s - m)
    return m, jnp.sum(p, axis=-1, keepdims=True), _dot(p.astype(BF16), v)


def _pv_with_denominator(p, v):
    ext = jnp.concatenate([v, jnp.ones(v.shape, v.dtype)], axis=1)
    r = _dot(p.astype(BF16), ext)
    return r[:, :HEAD_DIM], r[:, HEAD_DIM:]


def _partial_softmax_mxu_sum(s, v):
    m = jnp.max(s, axis=-1, keepdims=True)
    acc, denom = _pv_with_denominator(jnp.exp2(s - m), v)
    return m, denom, acc


def _merge_partials(parts):
    m = parts[0][0]
    for pm, _, _ in parts[1:]:
        m = jnp.maximum(m, pm)
    weights = [jnp.exp2(pm - m) for pm, _, _ in parts]
    denom = sum(w * pl_ for w, (_, pl_, _) in zip(weights, parts))
    acc = sum(w * pa for w, (_, _, pa) in zip(weights, parts))
    return acc / denom


T5_BAND = T5_MAX_DIST


def _diff_band(tq):
    lo = -((T5_BAND + LANE - 1) // LANE) - 1
    hi = (T5_BAND + tq - 1 + LANE - 1) // LANE
    return lo, hi


def _diff_kernel(q_ref, k_ref, v_ref, row_ref, lq1_ref, lk1_ref, lq2_ref, lk2_ref, sub_ref, o_ref,
                 tz_ref, *, tq, seq, lambda_init):
    qi = pl.program_id(2)
    lo, hi = _diff_band(tq)
    q_chunk0 = qi * (tq // LANE)
    span = seq // KEY_SPLITS

    @pl.when(qi == 0)
    def _():
        band = _circulant(row_ref[0], tq)
        for j in range(hi - lo + 1):
            tz_ref[j] = band[:, j * LANE:(j + 1) * LANE]

    def chain(mi, part):
        k0 = part * span
        bias = jnp.concatenate(
            [tz_ref[jnp.clip(cc - q_chunk0, lo, hi) - lo]
             for cc in range(k0 // LANE, (k0 + span) // LANE)], axis=1)
        s = _dot_nt(q_ref[mi], k_ref[mi, k0:k0 + span, :]) + bias
        vc = jnp.concatenate([v_ref[0, k0:k0 + span, :], v_ref[1, k0:k0 + span, :]], axis=1)
        return _partial_softmax(s, vc)

    chains = {(mi, part): chain(mi, part) for part in range(KEY_SPLITS) for mi in range(2)}

    def attend(mi):
        return _merge_partials([chains[mi, part] for part in range(KEY_SPLITS)])

    lam = (jnp.exp(jnp.sum(lq1_ref[...] * lk1_ref[...], axis=-1, keepdims=True))
           - jnp.exp(jnp.sum(lq2_ref[...] * lk2_ref[...], axis=-1, keepdims=True)) + lambda_init)
    o = attend(0) - lam * attend(1)
    o_ref[...] = (_rms(o, sub_ref[...]) * (1.0 - lambda_init)).astype(o_ref.dtype)


def _diff_bias_rows(t5_cols, tq):
    lo, hi = _diff_band(tq)
    cols = (hi - lo + 1) * LANE
    width = -(-(tq + cols - 1) // LANE) * LANE
    x, _ = _circulant_positions(cols, tq, width)
    rel = jnp.asarray(x + LANE * lo, jnp.int32)
    return (LOG2E * t5_cols[_t5_bucket(rel)].T.astype(F32))[:, None, :]


def diff_attention(proj, t5_cols, lq1, lk1, lq2, lk2, subln, *, chunk0, batch, seq, lambda_init):
    t = proj.shape[1]
    tq = FULL_ATTN_TQ
    assert seq % tq == 0 and seq % (KEY_SPLITS * LANE) == 0 and chunk0 % 2 == 0
    nq = seq // tq
    bias_rows = _diff_bias_rows(t5_cols, tq)
    lo, hi = _diff_band(tq)
    qb, kb, vb = chunk0 // 2, chunk0 // 2 + B_HEADS, chunk0 // 2 + 2 * B_HEADS
    vec = lambda a: a.reshape(1, -1).astype(F32)
    vspec = lambda n: pl.BlockSpec((1, n), lambda b, h, qi: (0, 0))
    return pl.pallas_call(
        functools.partial(_diff_kernel, tq=tq, seq=seq, lambda_init=lambda_init),
        grid=(batch, B_HEADS, nq),
        in_specs=[
            pl.BlockSpec((2, tq, LANE), lambda b, h, qi: (qb + h, b * nq + qi, 0)),
            pl.BlockSpec((2, seq, LANE), lambda b, h, qi: (kb + h, b, 0)),
            pl.BlockSpec((2, seq, LANE), lambda b, h, qi: (vb + h, b, 0)),
            pl.BlockSpec((1, 1, bias_rows.shape[2]), lambda b, h, qi: (h, 0, 0)),
            vspec(B_QK_DIM), vspec(B_QK_DIM), vspec(B_QK_DIM), vspec(B_QK_DIM), vspec(B_V_DIM),
        ],
        out_specs=pl.BlockSpec((tq, B_V_DIM), lambda b, h, qi: (b * nq + qi, h)),
        out_shape=jax.ShapeDtypeStruct((t, B_HEADS * B_V_DIM), BF16),
        scratch_shapes=[pltpu.VMEM((hi - lo + 1, tq, LANE), F32)],
        compiler_params=_params("parallel", "parallel", "arbitrary"),
        name="diff_attn",
    )(proj, proj, proj, bias_rows, vec(lq1), vec(lk1), vec(lq2), vec(lk2), vec(subln))


def _gqa_kernel(q_ref, k_ref, v_ref, o_ref, *, tq, seq, group):
    span = seq // KEY_SPLITS
    bounds = [0] + [span // 2 + i * span for i in range(KEY_SPLITS)] + [seq]
    q = q_ref[...].reshape(group * tq, HEAD_DIM)
    parts = []
    for k0, k1 in zip(bounds[:-1], bounds[1:]):
        parts.append(_partial_softmax_mxu_sum(_dot_nt(q, k_ref[0, k0:k1, :]), v_ref[0, k0:k1, :]))
    o = _merge_partials(parts)
    for g in range(group):
        o_ref[:, g * HEAD_DIM:(g + 1) * HEAD_DIM] = o[g * tq:(g + 1) * tq].astype(o_ref.dtype)


def gqa_attention(proj, *, batch, seq):
    t = proj.shape[1]
    tq = FULL_ATTN_TQ
    assert seq % tq == 0 and seq % (2 * KEY_SPLITS * LANE) == 0
    group = C_HEADS // C_KV_HEADS
    nq = seq // tq
    return pl.pallas_call(
        functools.partial(_gqa_kernel, tq=tq, seq=seq, group=group),
        grid=(batch, C_KV_HEADS, nq),
        in_specs=[
            pl.BlockSpec((group, tq, LANE), lambda b, n, qi: (n, b * nq + qi, 0)),
            pl.BlockSpec((1, seq, LANE), lambda b, n, qi: (C_HEADS + n, b, 0)),
            pl.BlockSpec((1, seq, LANE), lambda b, n, qi: (C_HEADS + C_KV_HEADS + n, b, 0)),
        ],
        out_specs=pl.BlockSpec((tq, group * HEAD_DIM), lambda b, n, qi: (b * nq + qi, n)),
        out_shape=jax.ShapeDtypeStruct((t, C_HEADS * HEAD_DIM), BF16),
        compiler_params=_params("parallel", "parallel", "arbitrary"),
        name="gqa_attn",
    )(proj, proj, proj)


NA_QROWS = 4
NA_KROWS = NA_QROWS + NA_ROWS
NA_BLOCKS_PER_STEP = 16


def _na_kernel(q_ref, k_ref, v_ref, rows_ref, o_ref, bm_ref, *, rows, nsub):
    step = pl.program_id(2)
    tq = NA_QROWS * GRID_W
    nkeys = NA_KROWS * GRID_W
    nblk = rows // NA_QROWS
    kr = min(NA_ROWS, rows)

    @pl.when(step == 0)
    def _():
        lane = lax.broadcasted_iota(jnp.int32, (GRID_W, LANE), 1)
        qj = lax.broadcasted_iota(jnp.int32, (GRID_W, LANE), 0)
        kj = lane % GRID_W
        cs = jnp.clip(qj - NA_COLS // 2, 0, GRID_W - NA_COLS)
        col_ok = (kj >= cs) & (kj < cs + NA_COLS)
        left = lane < GRID_W
        neg = jnp.full((GRID_W, LANE), NEG_INF, F32)
        bands = {}

        def band(dr, right):
            if (dr, right) not in bands:
                wide = jnp.broadcast_to(rows_ref[0, dr], (GRID_W, LANE))
                rolled = pltpu.roll(wide, GRID_W if right else 0, 1, stride=1, stride_axis=0)
                bands[dr, right] = jnp.where(col_ok, rolled, NEG_INF)
            return bands[dr, right]

        for variant, blk in enumerate((0, 1, nblk - 1)):
            wrow = min(max(blk * NA_QROWS - NA_ROWS // 2, 0), rows - NA_KROWS)
            for a in range(NA_QROWS):
                qi = blk * NA_QROWS + a
                rs = min(max(qi - kr // 2, 0), rows - kr)
                for pair in range(NA_KROWS // 2):
                    halves = []
                    for right in (False, True):
                        ki = wrow + 2 * pair + int(right)
                        halves.append(band(ki - qi + NA_ROWS - 1, right) if rs <= ki < rs + kr else neg)
                    bm_ref[variant, a * GRID_W:(a + 1) * GRID_W, pair * LANE:(pair + 1) * LANE] = (
                        jnp.where(left, halves[0], halves[1]))

    for c in range(nsub):
        blk = step * nsub + c
        wrow = jnp.clip(blk * NA_QROWS - NA_ROWS // 2, 0, rows - NA_KROWS)
        kstart = pl.multiple_of(wrow * GRID_W, NA_QROWS * GRID_W)
        kw = k_ref[0, pl.ds(kstart, nkeys), :]
        vw = v_ref[0, pl.ds(kstart, nkeys), :]
        variant = jnp.where(blk == 0, 0, jnp.where(blk == nblk - 1, 2, 1))
        s = _dot_nt(q_ref[0, c * tq:(c + 1) * tq, :], kw) + bm_ref[variant]
        m = jnp.max(s, axis=-1, keepdims=True)
        acc, denom = _pv_with_denominator(jnp.exp2(s - m), vw)
        o_ref[c * tq:(c + 1) * tq, :] = (acc / denom).astype(o_ref.dtype)


def _na_bias_rows(rpb):
    r = LOG2E * rpb.astype(F32)
    gap = jnp.zeros(r.shape[:-1] + (LANE - r.shape[-1],), F32)
    return jnp.concatenate([r[..., NA_COLS - 1:], gap, r[..., :NA_COLS - 1]], axis=-1)[:, :, None, :]


def na_attention(proj, rpb, *, chunk0, batch, seq):
    t = proj.shape[1]
    rows = seq // GRID_W
    nsub = NA_BLOCKS_PER_STEP
    assert rows % (NA_QROWS * nsub) == 0 and rows >= NA_KROWS and rows // NA_QROWS >= 3
    assert NA_KROWS % 2 == 0 and 2 * GRID_W == LANE and 2 * NA_COLS - 1 <= LANE
    nblk = rows // NA_QROWS
    nstep = nblk // nsub
    tq = NA_QROWS * GRID_W
    nkeys = NA_KROWS * GRID_W
    bias_rows = _na_bias_rows(rpb)
    return pl.pallas_call(
        functools.partial(_na_kernel, rows=rows, nsub=nsub),
        grid=(batch, D_HEADS, nstep),
        in_specs=[
            pl.BlockSpec((1, nsub * tq, LANE), lambda b, h, step: (chunk0 + h, b * nstep + step, 0)),
            pl.BlockSpec((1, seq, LANE), lambda b, h, step: (chunk0 + D_HEADS + h, b, 0)),
            pl.BlockSpec((1, seq, LANE), lambda b, h, step: (chunk0 + 2 * D_HEADS + h, b, 0)),
            pl.BlockSpec((1,) + bias_rows.shape[1:], lambda b, h, step: (h, 0, 0, 0)),
        ],
        out_specs=pl.BlockSpec((nsub * tq, LANE), lambda b, h, step: (b * nstep + step, h)),
        out_shape=jax.ShapeDtypeStruct((t, D_HEADS * HEAD_DIM), BF16),
        scratch_shapes=[pltpu.VMEM((3, tq, nkeys), F32)],
        compiler_params=_params("parallel", "parallel", "arbitrary"),
        name="na_attn",
    )(proj, proj, proj, bias_rows)


def _out_proj_kernel(a_ref, b_ref, w_ref, h_ref, out_ref, *, na):
    out_ref[...] = h_ref[...] + _dot(a_ref[...], w_ref[:na, :]) + _dot(b_ref[...], w_ref[na:, :])


def out_proj(a, b, w, h, *, tm, tn):
    t, d = h.shape
    na, nb = a.shape[1], b.shape[1]
    w = w.astype(BF16)
    return pl.pallas_call(
        functools.partial(_out_proj_kernel, na=na),
        grid=(t // tm, d // tn),
        in_specs=[pl.BlockSpec((tm, na), lambda i, j: (i, 0)),
                  pl.BlockSpec((tm, nb), lambda i, j: (i, 0)),
                  pl.BlockSpec((na + nb, tn), lambda i, j: (0, j)),
                  pl.BlockSpec((tm, tn), lambda i, j: (i, j))],
        out_specs=pl.BlockSpec((tm, tn), lambda i, j: (i, j)),
        out_shape=jax.ShapeDtypeStruct((t, d), F32),
        compiler_params=_params("parallel", "arbitrary"),
        name="out_proj",
    )(a, b, w, h)


FFN_HALO = 16


def _gelu_tanh(x):
    return 0.5 * x * (1.0 + jnp.tanh(math.sqrt(2.0 / math.pi) * (x + 0.044715 * (x * x * x))))


def _ffn_kernel(x_ref, xp_ref, xnx_ref, g_ref, wg_ref, wu_ref, cw_ref, cb_ref, wd_ref, gf_ref, out_ref,
                xn_ref, *, tm, tiles_per_seq, final_norm):
    i = pl.program_id(0)
    f = pl.program_id(1)
    nf = pl.num_programs(1)
    hl = FFN_HALO

    @pl.when(f == 0)
    def _():
        gain = g_ref[...]
        keep_prev = jnp.where(i % tiles_per_seq != 0, 1.0, 0.0)
        keep_next = jnp.where(i % tiles_per_seq != tiles_per_seq - 1, 1.0, 0.0)
        xn_ref[:hl, :] = (_rms(xp_ref[...], gain) * keep_prev).astype(BF16)
        xn_ref[hl:hl + tm, :] = _rms(x_ref[...], gain).astype(BF16)
        xn_ref[hl + tm:, :] = (_rms(xnx_ref[...], gain) * keep_next).astype(BF16)
        out_ref[...] = x_ref[...]

    ge = _dot(xn_ref[...], wg_ref[...])
    cw = cw_ref[...]
    g = (cw[0:1, :] * ge[hl - 1:hl - 1 + tm] + cw[1:2, :] * ge[hl:hl + tm]
         + cw[2:3, :] * ge[hl + 1:hl + 1 + tm] + cb_ref[...])
    u = _dot(xn_ref[hl:hl + tm, :], wu_ref[...])
    act = (_gelu_tanh(g) * u).astype(BF16)
    out_ref[...] += _dot(act, wd_ref[...])

    if final_norm:
        @pl.when(f == nf - 1)
        def _():
            out_ref[...] = _rms(out_ref[...], gf_ref[...])


def conv_ffn_block(h, gain, w_up, conv_w, conv_b, w_down, final_gain, *, seq, tm, tf, final_norm):
    t, d = h.shape
    ff = w_down.shape[0]
    assert t % tm == 0 and seq % tm == 0 and ff % tf == 0 and tm % FFN_HALO == 0
    nfb = ff // tf
    w_up = w_up.astype(BF16)
    w_down = w_down.astype(BF16)
    hb = tm // FFN_HALO
    last_halo = t // FFN_HALO - 1
    return pl.pallas_call(
        functools.partial(_ffn_kernel, tm=tm, tiles_per_seq=seq // tm, final_norm=final_norm),
        grid=(t // tm, nfb),
        in_specs=[
            pl.BlockSpec((tm, d), lambda i, f: (i, 0)),
            pl.BlockSpec((FFN_HALO, d), lambda i, f: (jnp.maximum(i * hb - 1, 0), 0)),
            pl.BlockSpec((FFN_HALO, d), lambda i, f: (jnp.minimum((i + 1) * hb, last_halo), 0)),
            pl.BlockSpec((1, d), lambda i, f: (0, 0)),
            pl.BlockSpec((d, tf), lambda i, f: (0, f)),
            pl.BlockSpec((d, tf), lambda i, f: (0, nfb + f)),
            pl.BlockSpec((3, tf), lambda i, f: (0, f)),
            pl.BlockSpec((1, tf), lambda i, f: (0, f)),
            pl.BlockSpec((tf, d), lambda i, f: (f, 0)),
            pl.BlockSpec((1, d), lambda i, f: (0, 0)),
        ],
        out_specs=pl.BlockSpec((tm, d), lambda i, f: (i, 0)),
        out_shape=jax.ShapeDtypeStruct((t, d), F32),
        scratch_shapes=[pltpu.VMEM((tm + 2 * FFN_HALO, d), BF16)],
        compiler_params=_params("parallel", "arbitrary", vmem_limit=V7X_VMEM_LIMIT_FFN),
        name="conv_ffn",
    )(h, h, h, gain.reshape(1, d), w_up, w_up, conv_w, conv_b.reshape(1, ff), w_down,
      final_gain.reshape(1, d))


def _rope_tables(seq):
    tpos = jnp.arange(seq, dtype=jnp.int32)
    row = (tpos // GRID_W).astype(F32)
    col = (tpos % GRID_W).astype(F32)
    inv_freq = ROPE_THETA ** (-(jnp.arange(0, ROPE_AXIS_DIM, 2, dtype=F32) / ROPE_AXIS_DIM))
    ang = jnp.concatenate([row[:, None] * inv_freq[None], col[:, None] * inv_freq[None]], axis=-1)
    cos, sin = jnp.cos(ang), jnp.sin(ang)
    cos_dup = jnp.repeat(cos, 2, axis=-1)
    sin_signed = jnp.stack([-sin, sin], axis=-1).reshape(seq, HEAD_DIM)
    return cos_dup, sin_signed


def kernel(x, ln_mix, ln_ffn, ln_final, t5_table, ev_w_in, ev_w_out, diff_lq1, diff_lk1, diff_lq2,
           diff_lk2, diff_subln, od_w_in, od_w_out, gqa_q_norm, gqa_k_norm, na_rpb, ffn_w_up,
           ffn_conv_w, ffn_conv_b, ffn_w_down):
    batch, seq, d = x.shape
    depth = ln_mix.shape[0]
    t = batch * seq
    h = x.reshape(t, d)
    proj_tm, even_tn, odd_tn = 1024, A_HEADS * HEAD_DIM, 12 * LANE
    out_tiles = dict(tm=512, tn=d)
    ffn_tiles = dict(tm=1024, tf=512)

    for layer in range(depth):
        if layer % 2 == 0:
            e = layer // 2
            lambda_init = 0.8 - 0.6 * math.exp(-0.3 * layer)
            nat, *dil_projs = even_norm_proj(h, ln_mix[layer], to_bf16(ev_w_in, e),
                                             batch=batch, seq=seq, tm=proj_tm, tn=even_tn)
            oa = dilated_attention(nat, dil_projs, t5_table, batch=batch, seq=seq)
            ob = diff_attention(nat, t5_table[:, A_GROUPS * A_HEADS:], diff_lq1[e], diff_lk1[e],
                                diff_lq2[e], diff_lk2[e], diff_subln[e], chunk0=3 * A_HEADS,
                                batch=batch, seq=seq, lambda_init=lambda_init)
            h = out_proj(oa, ob, to_bf16(ev_w_out, e), h, **out_tiles)
        else:
            o = layer // 2
            n_rope = C_HEADS + C_KV_HEADS
            head_gain = jnp.concatenate([jnp.tile(gqa_q_norm[o][None], (C_HEADS, 1)),
                                         jnp.tile(gqa_k_norm[o][None], (C_KV_HEADS, 1))], axis=0)
            cos_dup, sin_signed = _rope_tables(seq)
            na_chunk0 = n_rope + C_KV_HEADS
            query_chunks = list(range(C_HEADS)) + list(range(na_chunk0, na_chunk0 + D_HEADS))
            proj = odd_norm_proj(h, ln_mix[layer], to_bf16(od_w_in, o), head_gain.astype(F32),
                                 cos_dup, sin_signed, query_chunks=query_chunks, tm=proj_tm, tn=odd_tn)
            oc = gqa_attention(proj, batch=batch, seq=seq)
            od = na_attention(proj, na_rpb[o], chunk0=na_chunk0, batch=batch, seq=seq)
            h = out_proj(oc, od, to_bf16(od_w_out, o), h, **out_tiles)
        h = conv_ffn_block(h, ln_ffn[layer], to_bf16(ffn_w_up, layer), ffn_conv_w[layer],
                           ffn_conv_b[layer], to_bf16(ffn_w_down, layer), ln_final,
                           seq=seq, final_norm=(layer == depth - 1), **ffn_tiles)
    return h.reshape(batch, seq, d)
```

```python
import functools
import math

import numpy as np
import jax
import jax.numpy as jnp
from jax import lax
from jax.experimental import pallas as pl
from jax.experimental.pallas import tpu as pltpu

HEAD_DIM = 128
GRID_W = 64
RMS_EPS = 1e-6
NEG_INF = -1e30
T5_BUCKETS = 32
T5_MAX_DIST = 1024
DILATED_PATTERNS = ((128, 1), (512, 4), (2048, 16))
A_GROUPS = len(DILATED_PATTERNS)
A_HEADS = 8
B_HEADS = 4
B_QK_DIM = 128
B_V_DIM = 2 * B_QK_DIM
C_HEADS = 8
C_KV_HEADS = 2
ROPE_THETA = 10000.0
ROPE_AXIS_DIM = HEAD_DIM // 2
D_HEADS = 8
NA_ROWS = 8
NA_COLS = 16

LANE = 128
V7X_VMEM_LIMIT = 56 * 1024 * 1024
V7X_VMEM_LIMIT_FFN = 61 * 1024 * 1024

BF16 = jnp.bfloat16
F32 = jnp.float32

LOG2E = math.log2(math.e)
QUERY_SCALE = HEAD_DIM ** -0.5 * LOG2E


def _params(*sem, vmem_limit=V7X_VMEM_LIMIT):
    return pltpu.CompilerParams(dimension_semantics=sem, vmem_limit_bytes=vmem_limit)


def _rms(xf, gain):
    ms = jnp.mean(xf * xf, axis=-1, keepdims=True)
    return xf * lax.rsqrt(ms + RMS_EPS) * gain


def _dot_nt(a, b):
    return lax.dot_general(a, b, (((1,), (1,)), ((), ())), preferred_element_type=F32)


def _dot(a, b):
    return jnp.dot(a, b, preferred_element_type=F32)


def _circulant(row, rows):
    return pltpu.roll(jnp.broadcast_to(row, (rows, row.shape[1])), 0, 1, stride=1, stride_axis=0)


def _circulant_positions(cols, rows, width):
    assert width >= rows + cols - 1
    p = np.arange(width)
    x = np.where(p < cols, p, p - width)
    return x, (p < cols) | (p > width - rows)


def _lanes(x, width):
    return x if width == LANE else jnp.concatenate([x] * (width // LANE), axis=1)


CAST_BLOCK_BYTES = 8 * 1024 * 1024


def _cast_kernel(w_ref, o_ref):
    o_ref[...] = w_ref[0].astype(o_ref.dtype)


def to_bf16(stack, index):
    _, k, n = stack.shape
    fits = [r for r in range(16, k + 1, 16) if k % r == 0 and 4 * r * n <= CAST_BLOCK_BYTES]
    rows = max(fits) if fits else k
    return pl.pallas_call(
        _cast_kernel,
        grid=(k // rows,),
        in_specs=[pl.BlockSpec((1, rows, n), lambda i: (index, i, 0))],
        out_specs=pl.BlockSpec((rows, n), lambda i: (i, 0)),
        out_shape=jax.ShapeDtypeStruct((k, n), BF16),
        compiler_params=_params("parallel"),
        name="cast_bf16",
    )(stack)


def _t5_bucket(rel):
    nb = T5_BUCKETS // 2
    max_exact = nb // 2
    ret = jnp.where(rel > 0, nb, 0)
    n = jnp.abs(rel)
    n_f = jnp.maximum(n, 1).astype(F32)
    large = max_exact + (jnp.log(n_f / max_exact) / math.log(T5_MAX_DIST / max_exact)
                         * (nb - max_exact)).astype(jnp.int32)
    large = jnp.minimum(large, nb - 1)
    return ret + jnp.where(n < max_exact, n, large)


DEINTERLEAVE_STRIDE = 4


def _store_chunks(acc, o_ref, n_chunks):
    for c in range(n_chunks):
        o_ref[c] = acc[:, c * LANE:(c + 1) * LANE].astype(o_ref.dtype)


def _even_proj_kernel(x_ref, g_ref, w_ref, nat_ref, *rest, plan, n_chunks, tm, qkv_period):
    dil_refs, (xn_ref, slab_ref, tmp_ref) = rest[:-3], rest[-3:]
    j = pl.program_id(1)

    @pl.when(j == 0)
    def _():
        xn_ref[...] = _rms(x_ref[...], g_ref[...]).astype(BF16)

    factor = jnp.where(j % qkv_period == 0, QUERY_SCALE, 1.0)

    for lo, hi, dilation, out_idx in plan:
        @pl.when((j >= lo) & (j < hi))
        def _(dilation=dilation, out_idx=out_idx):
            acc = _dot(xn_ref[...], w_ref[...]) * factor
            if dilation == 1:
                _store_chunks(acc, nat_ref, n_chunks)
                return
            o_ref = dil_refs[out_idx]
            per = tm // dilation
            for c in range(n_chunks):
                slab_ref[c] = acc[:, c * LANE:(c + 1) * LANE]
            if dilation <= DEINTERLEAVE_STRIDE:
                for c in range(n_chunks):
                    for r in range(dilation):
                        o_ref[c, 0, r] = slab_ref[c, pl.ds(r, per, stride=dilation), :].astype(o_ref.dtype)
                return
            st = DEINTERLEAVE_STRIDE
            outer = dilation // st
            for c in range(n_chunks):
                for lo in range(st):
                    tmp_ref[c, lo] = slab_ref[c, pl.ds(lo, tm // st, stride=st), :]
                for hi in range(outer):
                    for lo in range(st):
                        o_ref[c, 0, hi * st + lo] = tmp_ref[c, lo, pl.ds(hi, per, stride=outer), :].astype(o_ref.dtype)


def even_norm_proj(x, gain, w, *, batch, seq, tm, tn):
    t, d = x.shape
    n = w.shape[1]
    w = w.astype(BF16)
    n_chunks = tn // LANE
    group_cols = 3 * A_HEADS * HEAD_DIM
    assert group_cols % tn == 0 and n % tn == 0 and seq % tm == 0 and t == batch * seq
    assert tn == A_HEADS * HEAD_DIM == B_HEADS * 2 * B_QK_DIM and B_QK_DIM == HEAD_DIM
    gt = group_cols // tn
    nj = n // tn
    tiles_per_seq = seq // tm
    dils = [dil for _, dil in DILATED_PATTERNS]
    assert dils[0] == 1 and all(tm % (16 * dil) == 0 for dil in dils)
    plan = [(0, gt, 1, None)]
    for gi in range(1, A_GROUPS):
        plan.append((gi * gt, (gi + 1) * gt, dils[gi], gi - 1))
    plan.append((A_GROUPS * gt, nj, 1, None))
    nat_tiles = gt + (nj - A_GROUPS * gt)

    def nat_map(i, j):
        return (jnp.where(j < gt, j, jnp.where(j < A_GROUPS * gt, gt - 1, j - (A_GROUPS - 1) * gt)), i, 0)

    def dil_map(gi):
        return lambda i, j: (jnp.clip(j - gi * gt, 0, gt - 1), i // tiles_per_seq, 0, i % tiles_per_seq, 0)

    out_specs = [pl.BlockSpec((n_chunks, tm, LANE), nat_map)]
    out_shape = [jax.ShapeDtypeStruct((nat_tiles * n_chunks, t, LANE), BF16)]
    for gi in range(1, A_GROUPS):
        dil = dils[gi]
        out_specs.append(pl.BlockSpec((n_chunks, 1, dil, tm // dil, LANE), dil_map(gi)))
        out_shape.append(jax.ShapeDtypeStruct((gt * n_chunks, batch, dil, seq // dil, LANE), BF16))
    return pl.pallas_call(
        functools.partial(_even_proj_kernel, plan=tuple(plan), n_chunks=n_chunks, tm=tm, qkv_period=gt),
        grid=(t // tm, nj),
        in_specs=[
            pl.BlockSpec((tm, d), lambda i, j: (i, 0)),
            pl.BlockSpec((1, d), lambda i, j: (0, 0)),
            pl.BlockSpec((d, tn), lambda i, j: (0, j)),
        ],
        out_specs=out_specs,
        out_shape=out_shape,
        scratch_shapes=[pltpu.VMEM((tm, d), BF16), pltpu.VMEM((n_chunks, tm, LANE), F32),
                        pltpu.VMEM((n_chunks, DEINTERLEAVE_STRIDE, tm // DEINTERLEAVE_STRIDE, LANE), F32)],
        compiler_params=_params("parallel", "arbitrary"),
        name="even_norm_proj",
    )(x, gain.reshape(1, d), w)


def _rope_chunk(y, hg, cosd, sins, swap):
    y = _rms(y, hg)
    partner = _dot(y.astype(BF16), swap)
    return y * cosd + partner * sins


def _odd_proj_kernel(x_ref, g_ref, w_ref, hg_ref, cos_ref, sin_ref, swap_ref, o_ref, xn_ref, *,
                     n_chunks, n_rope, n_tiles, query_chunks):
    j = pl.program_id(1)

    @pl.when(j == 0)
    def _():
        xn_ref[...] = _rms(x_ref[...], g_ref[...]).astype(BF16)

    acc = _dot(xn_ref[...], w_ref[...])

    for jt in range(n_tiles):
        @pl.when(j == jt)
        def _(jt=jt):
            if jt * n_chunks < n_rope:
                cosd = cos_ref[...]
                sins = sin_ref[...]
                swap = swap_ref[...]
            for c in range(n_chunks):
                gc = jt * n_chunks + c
                y = acc[:, c * LANE:(c + 1) * LANE]
                if gc < n_rope:
                    y = _rope_chunk(y, hg_ref[gc:gc + 1, :], cosd, sins, swap)
                if gc in query_chunks:
                    y = y * QUERY_SCALE
                o_ref[c] = y.astype(o_ref.dtype)


def odd_norm_proj(x, gain, w, head_gain, cos_dup, sin_signed, *, query_chunks, tm, tn):
    t, d = x.shape
    n = w.shape[1]
    assert t % tm == 0 and n % tn == 0 and tn % LANE == 0
    w = w.astype(BF16)
    n_chunks = tn // LANE
    n_rope = head_gain.shape[0]
    s = cos_dup.shape[0]
    assert s % tm == 0
    sb = s // tm
    return pl.pallas_call(
        functools.partial(_odd_proj_kernel, n_chunks=n_chunks, n_rope=n_rope, n_tiles=n // tn,
                          query_chunks=frozenset(query_chunks)),
        grid=(t // tm, n // tn),
        in_specs=[
            pl.BlockSpec((tm, d), lambda i, j: (i, 0)),
            pl.BlockSpec((1, d), lambda i, j: (0, 0)),
            pl.BlockSpec((d, tn), lambda i, j: (0, j)),
            pl.BlockSpec((n_rope, LANE), lambda i, j: (0, 0)),
            pl.BlockSpec((tm, LANE), lambda i, j: (i % sb, 0)),
            pl.BlockSpec((tm, LANE), lambda i, j: (i % sb, 0)),
            pl.BlockSpec((LANE, LANE), lambda i, j: (0, 0)),
        ],
        out_specs=pl.BlockSpec((n_chunks, tm, LANE), lambda i, j: (j, i, 0)),
        out_shape=jax.ShapeDtypeStruct((n // LANE, t, LANE), BF16),
        scratch_shapes=[pltpu.VMEM((tm, d), BF16)],
        compiler_params=_params("parallel", "arbitrary"),
        name="odd_norm_proj",
    )(x, gain.reshape(1, d), w, head_gain, cos_dup, sin_signed,
      jnp.asarray(np.eye(LANE, dtype=np.float32)[np.arange(LANE) ^ 1], BF16))


A_TQ = 256
A_BLOCKS_PER_ITER = 16


def _a_geometry(seq, window, dilation):
    sub = seq // dilation
    half = window // (2 * dilation)
    tq = min(A_TQ, sub)
    win = min(tq + 2 * half, sub)
    assert seq % dilation == 0 and sub % tq == 0 and half % 16 == 0 and win % LANE == 0
    assert win == sub or win == tq + 2 * half
    return sub, half, tq, win


def _dilated_kernel(*refs, geoms):
    ng = len(geoms)
    qkv = refs[:3 * ng]
    bias_rows = refs[3 * ng:4 * ng]
    o_ref = refs[4 * ng]
    m_s, l_s, acc_s = refs[4 * ng + 1:4 * ng + 4]
    bms = refs[4 * ng + 4:]

    for g in range(ng):
        _, _, _, tq, win = geoms[g]
        for variant in range(bms[g].shape[0]):
            bms[g][variant] = _circulant(bias_rows[g][variant, 0], tq)[:, :win]

    order = sorted(range(ng), key=lambda g: -geoms[g][0])
    for pos, g in enumerate(order):
        dilation, sub, half, tq, win = geoms[g]
        q_ref, k_ref, v_ref = qkv[3 * g:3 * g + 3]
        bm_ref = bms[g]
        nq = sub // tq
        first = pos == 0
        nblocks = dilation * nq
        batch = math.gcd(A_BLOCKS_PER_ITER, nblocks)

        def one_block(it, dilation=dilation, sub=sub, half=half, tq=tq, win=win, nq=nq,
                      q_ref=q_ref, k_ref=k_ref, v_ref=v_ref, bm_ref=bm_ref, first=first):
            r = it // nq
            qi = it % nq
            q0 = pl.multiple_of(qi * tq, tq)
            kstart = pl.multiple_of(jnp.clip(q0 - half, 0, sub - win), half)
            if dilation == 1:
                q = q_ref[0, pl.ds(q0, tq), :]
                kw = k_ref[0, pl.ds(kstart, win), :]
                vw = v_ref[0, pl.ds(kstart, win), :]
                rows = pl.ds(q0, tq)
            else:
                q = q_ref[0, 0, r, pl.ds(q0, tq), :]
                kw = k_ref[0, 0, r, pl.ds(kstart, win), :]
                vw = v_ref[0, 0, r, pl.ds(kstart, win), :]
                rows = pl.ds(q0 * dilation + r, tq, stride=dilation)
            variant = 0 if nq == 1 else jnp.where(qi == 0, 0, jnp.where(qi == nq - 1, 2, 1))
            s = _dot_nt(q, kw) + bm_ref[variant]
            mb = jnp.max(s, axis=-1, keepdims=True)
            if first:
                acc, denom = _pv_with_denominator(jnp.exp2(s - mb), vw)
                return rows, jnp.broadcast_to(mb, (tq, LANE)), denom, acc
            m_old = m_s[rows, :]
            m_new = jnp.maximum(m_old, mb)
            alpha = jnp.exp2(m_old - m_new)
            acc, denom = _pv_with_denominator(jnp.exp2(s - _lanes(m_new, win)), vw)
            return rows, m_new, alpha * l_s[rows, :] + denom, alpha * acc_s[rows, :] + acc

        def blocks(step, carry, batch=batch, one_block=one_block):
            results = [one_block(step * batch + i) for i in range(batch)]
            for rows, m_new, l_new, acc_new in results:
                m_s[rows, :] = m_new
                l_s[rows, :] = l_new
                acc_s[rows, :] = acc_new
            return carry

        lax.fori_loop(0, nblocks // batch, blocks, 0)

    o_ref[...] = (acc_s[...] / l_s[...]).astype(o_ref.dtype)


def _dilated_bias_rows(t5_cols, dilation, half, tq, win, sub):
    nq = sub // tq
    offs = [0] if nq == 1 else [0, -half, tq - win]
    width = -(-(tq + win - 1) // LANE) * LANE
    x, used = _circulant_positions(win, tq, width)
    rows = []
    for off in offs:
        rel = x + off
        ok = jnp.asarray(used & (np.abs(rel) <= half))[None]
        bias = LOG2E * t5_cols[_t5_bucket(jnp.asarray(dilation * rel, jnp.int32))].T.astype(F32)
        rows.append(jnp.where(ok, bias, NEG_INF))
    return jnp.stack(rows)[:, :, None, :]


def dilated_attention(nat, dil_projs, t5_table, *, batch, seq):
    t = batch * seq
    geoms, args, in_specs, bias_rows, tables = [], [], [], [], []
    for gi, (window, dilation) in enumerate(DILATED_PATTERNS):
        sub, half, tq, win = _a_geometry(seq, window, dilation)
        geoms.append((dilation, sub, half, tq, win))
        for part in range(3):
            if dilation == 1:
                args.append(nat)
                in_specs.append(pl.BlockSpec(
                    (1, seq, LANE), lambda b, h, part=part: (part * A_HEADS + h, b, 0)))
            else:
                args.append(dil_projs[gi - 1])
                in_specs.append(pl.BlockSpec(
                    (1, 1, dilation, sub, LANE),
                    lambda b, h, part=part: (part * A_HEADS + h, b, 0, 0, 0)))
        bias_rows.append(_dilated_bias_rows(t5_table[:, gi * A_HEADS:(gi + 1) * A_HEADS],
                                            dilation, half, tq, win, sub))
        tables.append(pltpu.VMEM((bias_rows[-1].shape[0], tq, win), F32))
    for rows in bias_rows:
        args.append(rows)
        in_specs.append(pl.BlockSpec((rows.shape[0], 1) + rows.shape[2:], lambda b, h: (0, h, 0, 0)))
    return pl.pallas_call(
        functools.partial(_dilated_kernel, geoms=tuple(geoms)),
        grid=(batch, A_HEADS),
        in_specs=in_specs,
        out_specs=pl.BlockSpec((seq, LANE), lambda b, h: (b, h)),
        out_shape=jax.ShapeDtypeStruct((t, A_HEADS * LANE), BF16),
        scratch_shapes=[pltpu.VMEM((seq, LANE), F32)] * 3 + tables,
        compiler_params=_params("parallel", "arbitrary"),
        name="dilated_attn",
    )(*args)


KEY_SPLITS = 4
FULL_ATTN_TQ = 256


def _partial_softmax(s, v):
    m = jnp.max(s, axis=-1, keepdims=True)
    p = jnp.exp2(s - m)
    return m, jnp.sum(p, axis=-1, keepdims=True), _dot(p.astype(BF16), v)


def _pv_with_denominator(p, v):
    ext = jnp.concatenate([v, jnp.ones(v.shape, v.dtype)], axis=1)
    r = _dot(p.astype(BF16), ext)
    return r[:, :HEAD_DIM], r[:, HEAD_DIM:]


def _partial_softmax_mxu_sum(s, v):
    m = jnp.max(s, axis=-1, keepdims=True)
    acc, denom = _pv_with_denominator(jnp.exp2(s - m), v)
    return m, denom, acc


def _merge_partials(parts):
    m = parts[0][0]
    for pm, _, _ in parts[1:]:
        m = jnp.maximum(m, pm)
    weights = [jnp.exp2(pm - m) for pm, _, _ in parts]
    denom = sum(w * pl_ for w, (_, pl_, _) in zip(weights, parts))
    acc = sum(w * pa for w, (_, _, pa) in zip(weights, parts))
    return acc / denom


T5_BAND = T5_MAX_DIST


def _diff_band(tq):
    lo = -((T5_BAND + LANE - 1) // LANE) - 1
    hi = (T5_BAND + tq - 1 + LANE - 1) // LANE
    return lo, hi


def _diff_kernel(q_ref, k_ref, v_ref, row_ref, lq1_ref, lk1_ref, lq2_ref, lk2_ref, sub_ref, o_ref,
                 tz_ref, *, tq, seq, lambda_init):
    qi = pl.program_id(2)
    lo, hi = _diff_band(tq)
    q_chunk0 = qi * (tq // LANE)
    span = seq // KEY_SPLITS

    @pl.when(qi == 0)
    def _():
        band = _circulant(row_ref[0], tq)
        for j in range(hi - lo + 1):
            tz_ref[j] = band[:, j * LANE:(j + 1) * LANE]

    def chain(mi, part):
        k0 = part * span
        bias = jnp.concatenate(
            [tz_ref[jnp.clip(cc - q_chunk0, lo, hi) - lo]
             for cc in range(k0 // LANE, (k0 + span) // LANE)], axis=1)
        s = _dot_nt(q_ref[mi], k_ref[mi, k0:k0 + span, :]) + bias
        vc = jnp.concatenate([v_ref[0, k0:k0 + span, :], v_ref[1, k0:k0 + span, :]], axis=1)
        return _partial_softmax(s, vc)

    chains = {(mi, part): chain(mi, part) for part in range(KEY_SPLITS) for mi in range(2)}

    def attend(mi):
        return _merge_partials([chains[mi, part] for part in range(KEY_SPLITS)])

    lam = (jnp.exp(jnp.sum(lq1_ref[...] * lk1_ref[...], axis=-1, keepdims=True))
           - jnp.exp(jnp.sum(lq2_ref[...] * lk2_ref[...], axis=-1, keepdims=True)) + lambda_init)
    o = attend(0) - lam * attend(1)
    o_ref[...] = (_rms(o, sub_ref[...]) * (1.0 - lambda_init)).astype(o_ref.dtype)


def _diff_bias_rows(t5_cols, tq):
    lo, hi = _diff_band(tq)
    cols = (hi - lo + 1) * LANE
    width = -(-(tq + cols - 1) // LANE) * LANE
    x, _ = _circulant_positions(cols, tq, width)
    rel = jnp.asarray(x + LANE * lo, jnp.int32)
    return (LOG2E * t5_cols[_t5_bucket(rel)].T.astype(F32))[:, None, :]


def diff_attention(proj, t5_cols, lq1, lk1, lq2, lk2, subln, *, chunk0, batch, seq, lambda_init):
    t = proj.shape[1]
    tq = FULL_ATTN_TQ
    assert seq % tq == 0 and seq % (KEY_SPLITS * LANE) == 0 and chunk0 % 2 == 0
    nq = seq // tq
    bias_rows = _diff_bias_rows(t5_cols, tq)
    lo, hi = _diff_band(tq)
    qb, kb, vb = chunk0 // 2, chunk0 // 2 + B_HEADS, chunk0 // 2 + 2 * B_HEADS
    vec = lambda a: a.reshape(1, -1).astype(F32)
    vspec = lambda n: pl.BlockSpec((1, n), lambda b, h, qi: (0, 0))
    return pl.pallas_call(
        functools.partial(_diff_kernel, tq=tq, seq=seq, lambda_init=lambda_init),
        grid=(batch, B_HEADS, nq),
        in_specs=[
            pl.BlockSpec((2, tq, LANE), lambda b, h, qi: (qb + h, b * nq + qi, 0)),
            pl.BlockSpec((2, seq, LANE), lambda b, h, qi: (kb + h, b, 0)),
            pl.BlockSpec((2, seq, LANE), lambda b, h, qi: (vb + h, b, 0)),
            pl.BlockSpec((1, 1, bias_rows.shape[2]), lambda b, h, qi: (h, 0, 0)),
            vspec(B_QK_DIM), vspec(B_QK_DIM), vspec(B_QK_DIM), vspec(B_QK_DIM), vspec(B_V_DIM),
        ],
        out_specs=pl.BlockSpec((tq, B_V_DIM), lambda b, h, qi: (b * nq + qi, h)),
        out_shape=jax.ShapeDtypeStruct((t, B_HEADS * B_V_DIM), BF16),
        scratch_shapes=[pltpu.VMEM((hi - lo + 1, tq, LANE), F32)],
        compiler_params=_params("parallel", "parallel", "arbitrary"),
        name="diff_attn",
    )(proj, proj, proj, bias_rows, vec(lq1), vec(lk1), vec(lq2), vec(lk2), vec(subln))


def _gqa_kernel(q_ref, k_ref, v_ref, o_ref, *, tq, seq, group):
    span = seq // KEY_SPLITS
    bounds = [0] + [span // 2 + i * span for i in range(KEY_SPLITS)] + [seq]
    q = q_ref[...].reshape(group * tq, HEAD_DIM)
    parts = []
    for k0, k1 in zip(bounds[:-1], bounds[1:]):
        parts.append(_partial_softmax_mxu_sum(_dot_nt(q, k_ref[0, k0:k1, :]), v_ref[0, k0:k1, :]))
    o = _merge_partials(parts)
    for g in range(group):
        o_ref[:, g * HEAD_DIM:(g + 1) * HEAD_DIM] = o[g * tq:(g + 1) * tq].astype(o_ref.dtype)


def gqa_attention(proj, *, batch, seq):
    t = proj.shape[1]
    tq = FULL_ATTN_TQ
    assert seq % tq == 0 and seq % (2 * KEY_SPLITS * LANE) == 0
    group = C_HEADS // C_KV_HEADS
    nq = seq // tq
    return pl.pallas_call(
        functools.partial(_gqa_kernel, tq=tq, seq=seq, group=group),
        grid=(batch, C_KV_HEADS, nq),
        in_specs=[
            pl.BlockSpec((group, tq, LANE), lambda b, n, qi: (n, b * nq + qi, 0)),
            pl.BlockSpec((1, seq, LANE), lambda b, n, qi: (C_HEADS + n, b, 0)),
            pl.BlockSpec((1, seq, LANE), lambda b, n, qi: (C_HEADS + C_KV_HEADS + n, b, 0)),
        ],
        out_specs=pl.BlockSpec((tq, group * HEAD_DIM), lambda b, n, qi: (b * nq + qi, n)),
        out_shape=jax.ShapeDtypeStruct((t, C_HEADS * HEAD_DIM), BF16),
        compiler_params=_params("parallel", "parallel", "arbitrary"),
        name="gqa_attn",
    )(proj, proj, proj)


NA_QROWS = 4
NA_KROWS = NA_QROWS + NA_ROWS
NA_BLOCKS_PER_STEP = 16


def _na_kernel(q_ref, k_ref, v_ref, rows_ref, o_ref, bm_ref, *, rows, nsub):
    step = pl.program_id(2)
    tq = NA_QROWS * GRID_W
    nkeys = NA_KROWS * GRID_W
    nblk = rows // NA_QROWS
    kr = min(NA_ROWS, rows)

    @pl.when(step == 0)
    def _():
        lane = lax.broadcasted_iota(jnp.int32, (GRID_W, LANE), 1)
        qj = lax.broadcasted_iota(jnp.int32, (GRID_W, LANE), 0)
        kj = lane % GRID_W
        cs = jnp.clip(qj - NA_COLS // 2, 0, GRID_W - NA_COLS)
        col_ok = (kj >= cs) & (kj < cs + NA_COLS)
        left = lane < GRID_W
        neg = jnp.full((GRID_W, LANE), NEG_INF, F32)
        bands = {}

        def band(dr, right):
            if (dr, right) not in bands:
                wide = jnp.broadcast_to(rows_ref[0, dr], (GRID_W, LANE))
                rolled = pltpu.roll(wide, GRID_W if right else 0, 1, stride=1, stride_axis=0)
                bands[dr, right] = jnp.where(col_ok, rolled, NEG_INF)
            return bands[dr, right]

        for variant, blk in enumerate((0, 1, nblk - 1)):
            wrow = min(max(blk * NA_QROWS - NA_ROWS // 2, 0), rows - NA_KROWS)
            for a in range(NA_QROWS):
                qi = blk * NA_QROWS + a
                rs = min(max(qi - kr // 2, 0), rows - kr)
                for pair in range(NA_KROWS // 2):
                    halves = []
                    for right in (False, True):
                        ki = wrow + 2 * pair + int(right)
                        halves.append(band(ki - qi + NA_ROWS - 1, right) if rs <= ki < rs + kr else neg)
                    bm_ref[variant, a * GRID_W:(a + 1) * GRID_W, pair * LANE:(pair + 1) * LANE] = (
                        jnp.where(left, halves[0], halves[1]))

    for c in range(nsub):
        blk = step * nsub + c
        wrow = jnp.clip(blk * NA_QROWS - NA_ROWS // 2, 0, rows - NA_KROWS)
        kstart = pl.multiple_of(wrow * GRID_W, NA_QROWS * GRID_W)
        kw = k_ref[0, pl.ds(kstart, nkeys), :]
        vw = v_ref[0, pl.ds(kstart, nkeys), :]
        variant = jnp.where(blk == 0, 0, jnp.where(blk == nblk - 1, 2, 1))
        s = _dot_nt(q_ref[0, c * tq:(c + 1) * tq, :], kw) + bm_ref[variant]
        m = jnp.max(s, axis=-1, keepdims=True)
        acc, denom = _pv_with_denominator(jnp.exp2(s - m), vw)
        o_ref[c * tq:(c + 1) * tq, :] = (acc / denom).astype(o_ref.dtype)


def _na_bias_rows(rpb):
    r = LOG2E * rpb.astype(F32)
    gap = jnp.zeros(r.shape[:-1] + (LANE - r.shape[-1],), F32)
    return jnp.concatenate([r[..., NA_COLS - 1:], gap, r[..., :NA_COLS - 1]], axis=-1)[:, :, None, :]


def na_attention(proj, rpb, *, chunk0, batch, seq):
    t = proj.shape[1]
    rows = seq // GRID_W
    nsub = NA_BLOCKS_PER_STEP
    assert rows % (NA_QROWS * nsub) == 0 and rows >= NA_KROWS and rows // NA_QROWS >= 3
    assert NA_KROWS % 2 == 0 and 2 * GRID_W == LANE and 2 * NA_COLS - 1 <= LANE
    nblk = rows // NA_QROWS
    nstep = nblk // nsub
    tq = NA_QROWS * GRID_W
    nkeys = NA_KROWS * GRID_W
    bias_rows = _na_bias_rows(rpb)
    return pl.pallas_call(
        functools.partial(_na_kernel, rows=rows, nsub=nsub),
        grid=(batch, D_HEADS, nstep),
        in_specs=[
            pl.BlockSpec((1, nsub * tq, LANE), lambda b, h, step: (chunk0 + h, b * nstep + step, 0)),
            pl.BlockSpec((1, seq, LANE), lambda b, h, step: (chunk0 + D_HEADS + h, b, 0)),
            pl.BlockSpec((1, seq, LANE), lambda b, h, step: (chunk0 + 2 * D_HEADS + h, b, 0)),
            pl.BlockSpec((1,) + bias_rows.shape[1:], lambda b, h, step: (h, 0, 0, 0)),
        ],
        out_specs=pl.BlockSpec((nsub * tq, LANE), lambda b, h, step: (b * nstep + step, h)),
        out_shape=jax.ShapeDtypeStruct((t, D_HEADS * HEAD_DIM), BF16),
        scratch_shapes=[pltpu.VMEM((3, tq, nkeys), F32)],
        compiler_params=_params("parallel", "parallel", "arbitrary"),
        name="na_attn",
    )(proj, proj, proj, bias_rows)


def _out_proj_kernel(a_ref, b_ref, w_ref, h_ref, out_ref, *, na):
    out_ref[...] = h_ref[...] + _dot(a_ref[...], w_ref[:na, :]) + _dot(b_ref[...], w_ref[na:, :])


def out_proj(a, b, w, h, *, tm, tn):
    t, d = h.shape
    na, nb = a.shape[1], b.shape[1]
    w = w.astype(BF16)
    return pl.pallas_call(
        functools.partial(_out_proj_kernel, na=na),
        grid=(t // tm, d // tn),
        in_specs=[pl.BlockSpec((tm, na), lambda i, j: (i, 0)),
                  pl.BlockSpec((tm, nb), lambda i, j: (i, 0)),
                  pl.BlockSpec((na + nb, tn), lambda i, j: (0, j)),
                  pl.BlockSpec((tm, tn), lambda i, j: (i, j))],
        out_specs=pl.BlockSpec((tm, tn), lambda i, j: (i, j)),
        out_shape=jax.ShapeDtypeStruct((t, d), F32),
        compiler_params=_params("parallel", "arbitrary"),
        name="out_proj",
    )(a, b, w, h)


FFN_HALO = 16


def _gelu_tanh(x):
    return 0.5 * x * (1.0 + jnp.tanh(math.sqrt(2.0 / math.pi) * (x + 0.044715 * (x * x * x))))


def _ffn_kernel(x_ref, xp_ref, xnx_ref, g_ref, wg_ref, wu_ref, cw_ref, cb_ref, wd_ref, gf_ref, out_ref,
                xn_ref, *, tm, tiles_per_seq, final_norm):
    i = pl.program_id(0)
    f = pl.program_id(1)
    nf = pl.num_programs(1)
    hl = FFN_HALO

    @pl.when(f == 0)
    def _():
        gain = g_ref[...]
        keep_prev = jnp.where(i % tiles_per_seq != 0, 1.0, 0.0)
        keep_next = jnp.where(i % tiles_per_seq != tiles_per_seq - 1, 1.0, 0.0)
        xn_ref[:hl, :] = (_rms(xp_ref[...], gain) * keep_prev).astype(BF16)
        xn_ref[hl:hl + tm, :] = _rms(x_ref[...], gain).astype(BF16)
        xn_ref[hl + tm:, :] = (_rms(xnx_ref[...], gain) * keep_next).astype(BF16)
        out_ref[...] = x_ref[...]

    ge = _dot(xn_ref[...], wg_ref[...])
    cw = cw_ref[...]
    g = (cw[0:1, :] * ge[hl - 1:hl - 1 + tm] + cw[1:2, :] * ge[hl:hl + tm]
         + cw[2:3, :] * ge[hl + 1:hl + 1 + tm] + cb_ref[...])
    u = _dot(xn_ref[hl:hl + tm, :], wu_ref[...])
    act = (_gelu_tanh(g) * u).astype(BF16)
    out_ref[...] += _dot(act, wd_ref[...])

    if final_norm:
        @pl.when(f == nf - 1)
        def _():
            out_ref[...] = _rms(out_ref[...], gf_ref[...])


def conv_ffn_block(h, gain, w_up, conv_w, conv_b, w_down, final_gain, *, seq, tm, tf, final_norm):
    t, d = h.shape
    ff = w_down.shape[0]
    assert t % tm == 0 and seq % tm == 0 and ff % tf == 0 and tm % FFN_HALO == 0
    nfb = ff // tf
    w_up = w_up.astype(BF16)
    w_down = w_down.astype(BF16)
    hb = tm // FFN_HALO
    last_halo = t // FFN_HALO - 1
    return pl.pallas_call(
        functools.partial(_ffn_kernel, tm=tm, tiles_per_seq=seq // tm, final_norm=final_norm),
        grid=(t // tm, nfb),
        in_specs=[
            pl.BlockSpec((tm, d), lambda i, f: (i, 0)),
            pl.BlockSpec((FFN_HALO, d), lambda i, f: (jnp.maximum(i * hb - 1, 0), 0)),
            pl.BlockSpec((FFN_HALO, d), lambda i, f: (jnp.minimum((i + 1) * hb, last_halo), 0)),
            pl.BlockSpec((1, d), lambda i, f: (0, 0)),
            pl.BlockSpec((d, tf), lambda i, f: (0, f)),
            pl.BlockSpec((d, tf), lambda i, f: (0, nfb + f)),
            pl.BlockSpec((3, tf), lambda i, f: (0, f)),
            pl.BlockSpec((1, tf), lambda i, f: (0, f)),
            pl.BlockSpec((tf, d), lambda i, f: (f, 0)),
            pl.BlockSpec((1, d), lambda i, f: (0, 0)),
        ],
        out_specs=pl.BlockSpec((tm, d), lambda i, f: (i, 0)),
        out_shape=jax.ShapeDtypeStruct((t, d), F32),
        scratch_shapes=[pltpu.VMEM((tm + 2 * FFN_HALO, d), BF16)],
        compiler_params=_params("parallel", "arbitrary", vmem_limit=V7X_VMEM_LIMIT_FFN),
        name="conv_ffn",
    )(h, h, h, gain.reshape(1, d), w_up, w_up, conv_w, conv_b.reshape(1, ff), w_down,
      final_gain.reshape(1, d))


def _rope_tables(seq):
    tpos = jnp.arange(seq, dtype=jnp.int32)
    row = (tpos // GRID_W).astype(F32)
    col = (tpos % GRID_W).astype(F32)
    inv_freq = ROPE_THETA ** (-(jnp.arange(0, ROPE_AXIS_DIM, 2, dtype=F32) / ROPE_AXIS_DIM))
    ang = jnp.concatenate([row[:, None] * inv_freq[None], col[:, None] * inv_freq[None]], axis=-1)
    cos, sin = jnp.cos(ang), jnp.sin(ang)
    cos_dup = jnp.repeat(cos, 2, axis=-1)
    sin_signed = jnp.stack([-sin, sin], axis=-1).reshape(seq, HEAD_DIM)
    return cos_dup, sin_signed


def kernel(x, ln_mix, ln_ffn, ln_final, t5_table, ev_w_in, ev_w_out, diff_lq1, diff_lk1, diff_lq2,
           diff_lk2, diff_subln, od_w_in, od_w_out, gqa_q_norm, gqa_k_norm, na_rpb, ffn_w_up,
           ffn_conv_w, ffn_conv_b, ffn_w_down):
    batch, seq, d = x.shape
    depth = ln_mix.shape[0]
    t = batch * seq
    h = x.reshape(t, d)
    proj_tm, even_tn, odd_tn = 1024, A_HEADS * HEAD_DIM, 12 * LANE
    out_tiles = dict(tm=512, tn=d)
    ffn_tiles = dict(tm=1024, tf=512)

    for layer in range(depth):
        if layer % 2 == 0:
            e = layer // 2
            lambda_init = 0.8 - 0.6 * math.exp(-0.3 * layer)
            nat, *dil_projs = even_norm_proj(h, ln_mix[layer], to_bf16(ev_w_in, e),
                                             batch=batch, seq=seq, tm=proj_tm, tn=even_tn)
            oa = dilated_attention(nat, dil_projs, t5_table, batch=batch, seq=seq)
            ob = diff_attention(nat, t5_table[:, A_GROUPS * A_HEADS:], diff_lq1[e], diff_lk1[e],
                                diff_lq2[e], diff_lk2[e], diff_subln[e], chunk0=3 * A_HEADS,
                                batch=batch, seq=seq, lambda_init=lambda_init)
            h = out_proj(oa, ob, to_bf16(ev_w_out, e), h, **out_tiles)
        else:
            o = layer // 2
            n_rope = C_HEADS + C_KV_HEADS
            head_gain = jnp.concatenate([jnp.tile(gqa_q_norm[o][None], (C_HEADS, 1)),
                                         jnp.tile(gqa_k_norm[o][None], (C_KV_HEADS, 1))], axis=0)
            cos_dup, sin_signed = _rope_tables(seq)
            na_chunk0 = n_rope + C_KV_HEADS
            query_chunks = list(range(C_HEADS)) + list(range(na_chunk0, na_chunk0 + D_HEADS))
            proj = odd_norm_proj(h, ln_mix[layer], to_bf16(od_w_in, o), head_gain.astype(F32),
                                 cos_dup, sin_signed, query_chunks=query_chunks, tm=proj_tm, tn=odd_tn)
            oc = gqa_attention(proj, batch=batch, seq=seq)
            od = na_attention(proj, na_rpb[o], chunk0=na_chunk0, batch=batch, seq=seq)
            h = out_proj(oc, od, to_bf16(od_w_out, o), h, **out_tiles)
        h = conv_ffn_block(h, ln_ffn[layer], to_bf16(ffn_w_up, layer), ffn_conv_w[layer],
                           ffn_conv_b[layer], to_bf16(ffn_w_down, layer), ln_final,
                           seq=seq, final_norm=(layer == depth - 1), **ffn_tiles)
    return h.reshape(batch, seq, d)
```
